```python
import math
import jax
import jax.numpy as jnp
from jax import lax
import numpy as np

D_MODEL = 1024
BATCH = 8
SEQ = 2048
DEPTH = 2
DEC_BATCH = 128
DEC_SEQ = 4
PAST_LEN = 2048
PAGE_SIZE = 128

NSA_HEADS = 16
NSA_KV_HEADS = 4
NSA_GROUP = NSA_HEADS // NSA_KV_HEADS
HEAD_DIM = D_MODEL // NSA_HEADS
NSA_KV_WIDTH = NSA_KV_HEADS * HEAD_DIM
CMP_STRIDE = 16
CMP_BLOCK = 2 * CMP_STRIDE
SEL_BLOCK = 64
TOP_N = 16
WINDOW = 512
Q_BLOCK = 128
ROPE_THETA = 10000.0
NSA_PROJ = NSA_HEADS * HEAD_DIM + 6 * NSA_KV_WIDTH + 3 * NSA_HEADS
GLA_HEADS = 4
GLA_DK = D_MODEL // 2 // GLA_HEADS
GLA_DV = D_MODEL // GLA_HEADS
GLA_GATE_RANK = 16
GLA_GATE_TEMP = 16.0
GLA_CHUNK = 64
GLA_PROJ = 2 * GLA_HEADS * GLA_DK + 2 * GLA_HEADS * GLA_DV + GLA_GATE_RANK
FFN_DIM = 2816
CONV_WIDTH = 3
N_NSA_LAYERS = (DEPTH + 1) // 2
N_GLA_LAYERS = DEPTH // 2
NORM_EPS = 1e-6
NEG_BIG = -1e30
FORCE = 1e6

kernel_name = "nsa_gla_convffn_hybrid_step"


def _rmsnorm(x, g):
    xf = x.astype(jnp.float32)
    y = xf * lax.rsqrt(jnp.mean(xf * xf, -1, keepdims=True) + NORM_EPS)
    return (y * g.astype(jnp.float32)).astype(x.dtype)


def _rope(x, pos):
    half = x.shape[-1] // 2
    inv = ROPE_THETA ** (-jnp.arange(half, dtype=jnp.float32) / half)
    ang = pos.astype(jnp.float32)[:, None] * inv[None, :]
    cos = jnp.cos(ang)[None, :, None, :]
    sin = jnp.sin(ang)[None, :, None, :]
    xf = x.astype(jnp.float32)
    x1, x2 = xf[..., :half], xf[..., half:]
    return jnp.concatenate([x1 * cos - x2 * sin, x2 * cos + x1 * sin], -1).astype(x.dtype)


def _masked_softmax(s, mask):
    s = jnp.where(mask, s, NEG_BIG)
    m = jnp.max(s, -1, keepdims=True)
    e = jnp.where(mask, jnp.exp(s - m), 0.0)
    return e / jnp.maximum(jnp.sum(e, -1, keepdims=True), 1e-30)


def _attend(q, k, v, mask):
    s = jnp.einsum('...tgrd,...sgd->...tgrs', q, k).astype(jnp.float32) * HEAD_DIM ** -0.5
    p = _masked_softmax(s, mask[..., :, None, None, :])
    return jnp.einsum('...tgrs,...sgd->...tgrd', p.astype(v.dtype), v)


def _compress(x, pe, w, n_blk):
    b = x.shape[0]
    c = x[:, :(n_blk + 1) * CMP_STRIDE].reshape(b, n_blk + 1, CMP_STRIDE, x.shape[2], x.shape[3])
    first = jnp.einsum('bcjgd,jde->bcge', c + pe[:CMP_STRIDE, None, :], w[:CMP_STRIDE])
    second = jnp.einsum('bcjgd,jde->bcge', c + pe[CMP_STRIDE:, None, :], w[CMP_STRIDE:])
    return first[:, :n_blk] + second[:, 1:]


def _cmp_branch(q, k_full, v_full, q_pos, pe, w):
    n_blk = k_full.shape[1] // CMP_STRIDE - 1
    kc = _compress(k_full, pe[0], w[0], n_blk)
    vc = _compress(v_full, pe[1], w[1], n_blk)
    s = jnp.einsum('btgrd,bngd->btgrn', q, kc).astype(jnp.float32) * HEAD_DIM ** -0.5
    blk_end = jnp.arange(n_blk) * CMP_STRIDE + CMP_BLOCK - 1
    mask = blk_end[None, :] <= q_pos[:, None]
    p = _masked_softmax(s, mask[:, None, None, :])
    o = jnp.einsum('btgrn,bngd->btgrd', p.astype(vc.dtype), vc)
    return o, p


def _select_blocks(p_cmp, q_pos, n_sel):
    n_blk = p_cmp.shape[-1]
    ps = jnp.sum(p_cmp, axis=3)
    i = jnp.arange(n_blk)[:, None]
    j = jnp.arange(n_sel)[None, :]
    overlap = ((i * CMP_STRIDE + CMP_BLOCK > j * SEL_BLOCK)
               & (i * CMP_STRIDE < (j + 1) * SEL_BLOCK)).astype(jnp.float32)
    sc = jnp.einsum('btgn,nj->btgj', ps, overlap)
    jj = jnp.arange(n_sel)[None, :]
    forced = (jj == (q_pos // SEL_BLOCK)[:, None]) | (jj == 0)
    future = jj * SEL_BLOCK > q_pos[:, None]
    sc = jnp.where(forced[:, None, :], FORCE, jnp.where(future[:, None, :], -FORCE, sc))
    vals, idx = lax.top_k(sc, min(TOP_N, n_sel))
    return idx, vals > -0.5 * FORCE


def _slc_seq(q, k, v, idx, valid, q_pos):
    n_kv = k.shape[1]
    kb = k.reshape(-1, SEL_BLOCK, n_kv, HEAD_DIM).transpose(2, 0, 1, 3)
    vb = v.reshape(-1, SEL_BLOCK, n_kv, HEAD_DIM).transpose(2, 0, 1, 3)
    g_ix = jnp.arange(n_kv)[None, :, None]

    def one_block(args):
        qb, ib, okb, pb = args
        ks = kb[g_ix, ib]
        vs = vb[g_ix, ib]
        kpos = ib[..., None] * SEL_BLOCK + jnp.arange(SEL_BLOCK)
        mask = okb[..., None] & (kpos <= pb[:, None, None, None])
        s = jnp.einsum('tgrd,tgnsd->tgrns', qb, ks).astype(jnp.float32) * HEAD_DIM ** -0.5
        t_, g_, r_, n_, sb_ = s.shape
        p = _masked_softmax(s.reshape(t_, g_, r_, n_ * sb_), mask.reshape(t_, g_, 1, n_ * sb_))
        return jnp.einsum('tgrm,tgmd->tgrd', p.astype(vs.dtype), vs.reshape(t_, g_, n_ * sb_, HEAD_DIM))

    t = q.shape[0]
    if t > Q_BLOCK and t % Q_BLOCK == 0:
        nb = t // Q_BLOCK
        split = lambda a: a.reshape(nb, Q_BLOCK, *a.shape[1:])
        o = lax.map(one_block, (split(q), split(idx), split(valid), split(q_pos)))
        return o.reshape(t, *o.shape[2:])
    return one_block((q, idx, valid, q_pos))


def _slc_branch(q, k_full, v_full, idx, valid, q_pos, n_sel):
    pad = n_sel * SEL_BLOCK - k_full.shape[1]
    kp = jnp.pad(k_full, ((0, 0), (0, pad), (0, 0), (0, 0)))
    vp = jnp.pad(v_full, ((0, 0), (0, pad), (0, 0), (0, 0)))
    return lax.map(lambda a: _slc_seq(a[0], a[1], a[2], a[3], a[4], q_pos), (q, kp, vp, idx, valid))


def _win_prompt(q, k, v):
    b, t, g, r, dh = q.shape
    nq = t // Q_BLOCK
    nw = WINDOW // Q_BLOCK
    kp = jnp.pad(k, ((0, 0), (WINDOW, 0), (0, 0), (0, 0))).reshape(b, nq + nw, Q_BLOCK, g, dh)
    vp = jnp.pad(v, ((0, 0), (WINDOW, 0), (0, 0), (0, 0))).reshape(b, nq + nw, Q_BLOCK, g, dh)
    bidx = jnp.arange(nq)[:, None] + jnp.arange(nw + 1)[None, :]
    span = (nw + 1) * Q_BLOCK
    kb = kp[:, bidx].reshape(b, nq, span, g, dh)
    vb = vp[:, bidx].reshape(b, nq, span, g, dh)
    qb = q.reshape(b, nq, Q_BLOCK, g, r, dh)
    a = jnp.arange(Q_BLOCK)[:, None]
    c = jnp.arange(span)[None, :]
    diff = a + WINDOW - c
    kpos = jnp.arange(nq)[:, None, None] * Q_BLOCK - WINDOW + c[None]
    mask = (diff >= 0)[None] & (diff <= WINDOW)[None] & (kpos >= 0)
    o = lax.map(lambda z: _attend(z[0], z[1], z[2], z[3]),
                (jnp.moveaxis(qb, 1, 0), jnp.moveaxis(kb, 1, 0), jnp.moveaxis(vb, 1, 0), mask))
    return jnp.moveaxis(o, 0, 1).reshape(b, t, g, r, dh)


def _win_sample(q, k_new, v_new, buf, pos):
    wb = buf.shape[1]
    k_all = jnp.concatenate([buf[:, :, 0], k_new], 1)
    v_all = jnp.concatenate([buf[:, :, 1], v_new], 1)
    k_pos = pos[0] - wb + jnp.arange(k_all.shape[1])
    diff = pos[:, None] - k_pos[None, :]
    mask = (diff >= 0) & (diff <= WINDOW)
    o = _attend(q, k_all, v_all, mask)
    return o, jnp.stack([k_all[:, -wb:], v_all[:, -wb:]], 2)


def _nsa_mixer(h, pos, w_in, cmp_pe, cmp_w, w_o, past, win_buf):
    b, t, _ = h.shape
    g, r, dh = NSA_KV_HEADS, NSA_GROUP, HEAD_DIM
    qd = NSA_HEADS * dh
    proj = h @ w_in
    q = _rope(proj[..., :qd].reshape(b, t, NSA_HEADS, dh), pos).reshape(b, t, g, r, dh)
    kvs = proj[..., qd:qd + 6 * NSA_KV_WIDTH].reshape(b, t, 6, g, dh)
    gates = jax.nn.sigmoid(proj[..., qd + 6 * NSA_KV_WIDTH:].astype(jnp.float32)).reshape(b, t, g, r, 3)
    k_c = _rope(kvs[:, :, 0], pos)
    v_c = kvs[:, :, 1]
    k_s = _rope(kvs[:, :, 2], pos)
    v_s = kvs[:, :, 3]
    k_w = _rope(kvs[:, :, 4], pos)
    v_w = kvs[:, :, 5]
    new_rows = jnp.stack([k_c, v_c, k_s, v_s], 2)
    full = new_rows if past is None else jnp.concatenate([past, new_rows], 1)
    o_c, p_c = _cmp_branch(q, full[:, :, 0], full[:, :, 1], pos, cmp_pe, cmp_w)
    n_sel = -(-full.shape[1] // SEL_BLOCK)
    idx, valid = _select_blocks(p_c, pos, n_sel)
    o_s = _slc_branch(q, full[:, :, 2], full[:, :, 3], idx, valid, pos, n_sel)
    if win_buf is None:
        o_w = _win_prompt(q, k_w, v_w)
        nb = min(WINDOW, t)
        new_win = jnp.stack([k_w[:, t - nb:], v_w[:, t - nb:]], 2)
    else:
        o_w, new_win = _win_sample(q, k_w, v_w, win_buf, pos)
    o = gates[..., 0:1] * o_c + gates[..., 1:2] * o_s + gates[..., 2:3] * o_w
    out = o.astype(h.dtype).reshape(b, t, qd) @ w_o
    return out, new_rows, new_win


def _gla_recurrence(q, k, v, log_a, s0):
    b, h, t, dk = q.shape
    dv = v.shape[-1]
    c = min(GLA_CHUNK, t)
    pad = (-t) % c
    if pad:
        padf = lambda a: jnp.pad(a, ((0, 0), (0, 0), (0, pad), (0, 0)))
        q, k, v, log_a = padf(q), padf(k), padf(v), padf(log_a)
    nc = (t + pad) // c
    to_chunks = lambda a: a.reshape(b, h, nc, c, a.shape[-1]).transpose(2, 0, 1, 3, 4)
    causal = jnp.tril(jnp.ones((c, c), bool))

    def step(s, xs):
        qc, kc, vc, ac = xs
        cum = jnp.cumsum(ac, axis=2)
        inter = jnp.einsum('bhtd,bhde->bhte', qc * jnp.exp(cum), s)
        diff = cum[:, :, :, None, :] - cum[:, :, None, :, :]
        dec = jnp.exp(jnp.where(causal[:, :, None], diff, -jnp.inf))
        att = jnp.einsum('bhtd,bhsd,bhtsd->bhts', qc, kc, dec)
        out = inter + jnp.einsum('bhts,bhse->bhte', att, vc)
        last = cum[:, :, -1]
        s = jnp.exp(last)[..., None] * s + jnp.einsum('bhsd,bhse->bhde', kc * jnp.exp(last[:, :, None, :] - cum), vc)
        return s, out

    s, o = lax.scan(step, s0, (to_chunks(q), to_chunks(k), to_chunks(v), to_chunks(log_a)))
    o = o.transpose(1, 2, 0, 3, 4).reshape(b, h, nc * c, dv)[:, :, :t]
    return o, s


def _gla_mixer(h, w_in, w_gate_up, b_gate, norm_g, w_o, s0):
    b, t, _ = h.shape
    nh, dk, dv = GLA_HEADS, GLA_DK, GLA_DV
    nk, nv = nh * dk, nh * dv
    proj = h @ w_in
    heads = lambda a, d: a.reshape(b, t, nh, d).transpose(0, 2, 1, 3).astype(jnp.float32)
    q = heads(proj[..., :nk], dk) * dk ** -0.5
    k = heads(proj[..., nk:2 * nk], dk)
    v = heads(proj[..., 2 * nk:2 * nk + nv], dv)
    r = proj[..., 2 * nk + nv:2 * nk + 2 * nv]
    gz = proj[..., 2 * nk + 2 * nv:] @ w_gate_up + b_gate
    log_a = heads(jax.nn.log_sigmoid(gz.astype(jnp.float32)) / GLA_GATE_TEMP, dk)
    if s0 is None:
        s0 = jnp.zeros((b, nh, dk, dv), jnp.float32)
    o, s = _gla_recurrence(q, k, v, log_a, s0.astype(jnp.float32))
    o = _rmsnorm(o.transpose(0, 2, 1, 3), norm_g) * jax.nn.silu(r.astype(jnp.float32)).reshape(b, t, nh, dv)
    out = o.reshape(b, t, nv).astype(h.dtype) @ w_o
    return out, s.astype(h.dtype)


def _conv_ffn(h, w_up, conv_w, conv_b, w_down, buf):
    b, t, _ = h.shape
    u = h @ w_up
    gate, val = u[..., :FFN_DIM], u[..., FFN_DIM:]
    if buf is None:
        buf = jnp.zeros((b, CONV_WIDTH - 1, FFN_DIM), gate.dtype)
    gc = jnp.concatenate([buf.astype(gate.dtype), gate], 1)
    conv = conv_b
    for j in range(CONV_WIDTH):
        conv = conv + conv_w[j] * gc[:, j:j + t]
    out = (jax.nn.gelu(conv) * val) @ w_down
    return out, gc[:, -(CONV_WIDTH - 1):]


def _trunk(x, pos, cache_nsa_kv, page_table, state_win_kv, state_gla, state_ffn_conv,
           norm_gain, nsa_w_in, nsa_cmp_pe, nsa_cmp_w, nsa_w_o,
           gla_w_in, gla_w_gate_up, gla_b_gate, gla_norm_gain, gla_w_o,
           ffn_w_up, ffn_conv_w, ffn_conv_b, ffn_w_down):
    sample = cache_nsa_kv is not None
    nsa_rows, win_bufs, gla_states, ffn_bufs = [], [], [], []
    for i in range(DEPTH):
        g = norm_gain[i]
        h = _rmsnorm(x, g[0])
        if i % 2 == 0:
            a = i // 2
            past, wbuf = None, None
            if sample:
                pages = cache_nsa_kv[a][page_table]
                past = pages.reshape(pages.shape[0], -1, *pages.shape[3:])
                wbuf = state_win_kv[a]
            out, rows, wb = _nsa_mixer(h, pos, nsa_w_in[a], nsa_cmp_pe[a], nsa_cmp_w[a], nsa_w_o[a], past, wbuf)
            nsa_rows.append(rows)
            win_bufs.append(wb)
        else:
            a = i // 2
            s0 = state_gla[a] if sample else None
            out, s = _gla_mixer(h, gla_w_in[a], gla_w_gate_up[a], gla_b_gate[a], gla_norm_gain[a], gla_w_o[a], s0)
            gla_states.append(s)
        x = x + _rmsnorm(out, g[1])
        h = _rmsnorm(x, g[2])
        out, fb = _conv_ffn(h, ffn_w_up[i], ffn_conv_w[i], ffn_conv_b[i], ffn_w_down[i],
                            state_ffn_conv[i] if sample else None)
        ffn_bufs.append(fb)
        x = x + _rmsnorm(out, g[3])
    return x, jnp.stack(nsa_rows), jnp.stack(win_bufs), jnp.stack(gla_states), jnp.stack(ffn_bufs)


def setup_inputs(seed: int = 0) -> dict:
    key = jax.random.key(seed)
    ks = jax.random.split(key, 24)
    f32 = jnp.float32
    nrm = lambda k, shape, scale: jax.random.normal(k, shape, f32) * scale
    n_pages = PAST_LEN // PAGE_SIZE
    n_phys = (5 * DEC_BATCH * n_pages + 3) // 4
    win_len = min(WINDOW, PAST_LEN)
    page_table = jax.random.permutation(ks[6], n_phys)[:DEC_BATCH * n_pages].reshape(DEC_BATCH, n_pages).astype(jnp.int32)
    return {
        'x_prompt': nrm(ks[0], (BATCH, SEQ, D_MODEL), 1.0),
        'x_sample': nrm(ks[1], (DEC_BATCH, DEC_SEQ, D_MODEL), 1.0),
        'cache_nsa_kv': nrm(ks[2], (N_NSA_LAYERS, n_phys, PAGE_SIZE, 4, NSA_KV_HEADS, HEAD_DIM), 1.0),
        'state_win_kv': nrm(ks[3], (N_NSA_LAYERS, DEC_BATCH, win_len, 2, NSA_KV_HEADS, HEAD_DIM), 1.0),
        'state_gla': nrm(ks[4], (N_GLA_LAYERS, DEC_BATCH, GLA_HEADS, GLA_DK, GLA_DV), 1.0),
        'state_ffn_conv': nrm(ks[5], (DEPTH, DEC_BATCH, CONV_WIDTH - 1, FFN_DIM), 1.0),
        'page_table': page_table,
        'norm_gain': 1.0 + nrm(ks[7], (DEPTH, 4, D_MODEL), 0.02),
        'nsa_w_in': nrm(ks[8], (N_NSA_LAYERS, D_MODEL, NSA_PROJ), D_MODEL ** -0.5),
        'nsa_cmp_pe': nrm(ks[9], (N_NSA_LAYERS, 2, CMP_BLOCK, HEAD_DIM), 0.5),
        'nsa_cmp_w': nrm(ks[10], (N_NSA_LAYERS, 2, CMP_BLOCK, HEAD_DIM, HEAD_DIM), (CMP_BLOCK * HEAD_DIM) ** -0.5),
        'nsa_w_o': nrm(ks[11], (N_NSA_LAYERS, NSA_HEADS * HEAD_DIM, D_MODEL), (NSA_HEADS * HEAD_DIM) ** -0.5),
        'gla_w_in': nrm(ks[12], (N_GLA_LAYERS, D_MODEL, GLA_PROJ), D_MODEL ** -0.5),
        'gla_w_gate_up': nrm(ks[13], (N_GLA_LAYERS, GLA_GATE_RANK, GLA_HEADS * GLA_DK), GLA_GATE_RANK ** -0.5),
        'gla_b_gate': nrm(ks[14], (N_GLA_LAYERS, GLA_HEADS * GLA_DK), 0.1),
        'gla_norm_gain': 1.0 + nrm(ks[15], (N_GLA_LAYERS, GLA_DV), 0.02),
        'gla_w_o': nrm(ks[16], (N_GLA_LAYERS, GLA_HEADS * GLA_DV, D_MODEL), (GLA_HEADS * GLA_DV) ** -0.5),
        'ffn_w_up': nrm(ks[17], (DEPTH, D_MODEL, 2 * FFN_DIM), D_MODEL ** -0.5),
        'ffn_conv_w': nrm(ks[18], (DEPTH, CONV_WIDTH, FFN_DIM), CONV_WIDTH ** -0.5),
        'ffn_conv_b': nrm(ks[19], (DEPTH, FFN_DIM), 0.02),
        'ffn_w_down': nrm(ks[20], (DEPTH, FFN_DIM, D_MODEL), FFN_DIM ** -0.5),
    }


def reference(x_prompt, x_sample, cache_nsa_kv, state_win_kv, state_gla, state_ffn_conv, page_table,
              norm_gain, nsa_w_in, nsa_cmp_pe, nsa_cmp_w, nsa_w_o,
              gla_w_in, gla_w_gate_up, gla_b_gate, gla_norm_gain, gla_w_o,
              ffn_w_up, ffn_conv_w, ffn_conv_b, ffn_w_down):
    pos_p = jnp.arange(x_prompt.shape[1], dtype=jnp.int32)
    past_len = page_table.shape[1] * cache_nsa_kv.shape[2]
    pos_s = past_len + jnp.arange(x_sample.shape[1], dtype=jnp.int32)
    y_prompt, nsa_p, win_p, gla_p, ffn_p = _trunk(
        x_prompt, pos_p, None, None, None, None, None,
        norm_gain, nsa_w_in, nsa_cmp_pe, nsa_cmp_w, nsa_w_o,
        gla_w_in, gla_w_gate_up, gla_b_gate, gla_norm_gain, gla_w_o,
        ffn_w_up, ffn_conv_w, ffn_conv_b, ffn_w_down)
    y_sample, nsa_s, win_s, gla_s, ffn_s = _trunk(
        x_sample, pos_s, cache_nsa_kv, page_table, state_win_kv, state_gla, state_ffn_conv,
        norm_gain, nsa_w_in, nsa_cmp_pe, nsa_cmp_w, nsa_w_o,
        gla_w_in, gla_w_gate_up, gla_b_gate, gla_norm_gain, gla_w_o,
        ffn_w_up, ffn_conv_w, ffn_conv_b, ffn_w_down)
    return (y_prompt, y_sample, nsa_p, nsa_s, win_p, win_s, gla_p, gla_s, ffn_p, ffn_s)
```

```python
import functools
import math

import jax
import jax.numpy as jnp
from jax import lax
from jax.experimental import pallas as pl
from jax.experimental.pallas import tpu as pltpu

F32, BF16 = jnp.float32, jnp.bfloat16

HEAD_DIM = 64
NSA_HEADS = 16
NSA_KV_HEADS = 4
NSA_GROUP = 4
CMP_STRIDE = 16
SEL_BLOCK = 64
TOP_N = 16
WINDOW = 512
ROPE_THETA = 10000.0
GLA_HEADS = 4
GLA_DK = 128
GLA_DV = 256
GLA_GATE_RANK = 16
GLA_GATE_TEMP = 16.0
NORM_EPS = 1e-6
FORCE = 1e6
NEG = -1e30

LANES = 128
SUBLANES = 8
VMEM_LIMIT = 48 * 1024 * 1024

TM = 256
TQ = 256
CMP_ROWS = 256
GLA_CHUNK = 128
FFN_FCHUNK = 256


def _cparams(sem):
    return pltpu.CompilerParams(dimension_semantics=sem, vmem_limit_bytes=VMEM_LIMIT)


def _resident(shape):
    nd = len(shape)
    return pl.BlockSpec(shape, lambda *_: (0,) * nd, pipeline_mode=pl.Buffered(1))


def _rms(x, g):
    return x * lax.rsqrt(jnp.mean(x * x, axis=-1, keepdims=True) + NORM_EPS) * g


def _nt(a, b):
    return lax.dot_general(a, b, (((1,), (1,)), ((), ())), preferred_element_type=F32)


def _tn(a, b):
    return lax.dot_general(a, b, (((0,), (0,)), ((), ())), preferred_element_type=F32)


def _dot(a, b):
    return jnp.dot(a, b, preferred_element_type=F32)


def _split_dot(w, x):
    hi = x.astype(BF16)
    lo = (x - hi.astype(F32)).astype(BF16)
    return _dot(w, hi) + _dot(w, lo)


def _softmax_masked(s, mask, axis):
    s = jnp.where(mask, s, NEG)
    m = jnp.max(s, axis=axis, keepdims=True)
    e = jnp.where(mask, jnp.exp(s - m), 0.0)
    return e / jnp.maximum(jnp.sum(e, axis=axis, keepdims=True), 1e-30)


def _nsa_proj_body(x_ref, g_ref, w_ref, cos_ref, sa_ref, sb_ref, q_ref, rows_ref, win_ref, gate_ref):
    h = _rms(x_ref[...], g_ref[...]).astype(BF16)
    cos, sa, sb = cos_ref[...], sa_ref[...], sb_ref[...]

    def rope(z):
        return z * cos + pltpu.roll(z, 96, axis=1) * sa + pltpu.roll(z, 32, axis=1) * sb

    def proj(c0):
        return _dot(h, w_ref[:, c0:c0 + 256])

    def rope256(z):
        return jnp.concatenate([rope(z[:, :LANES]), rope(z[:, LANES:])], axis=1)

    for c in range(4):
        q_ref[:, c * 256:(c + 1) * 256] = (rope256(proj(c * 256)) * (HEAD_DIM ** -0.5)).astype(BF16)
    rows_ref[:, 0:256] = rope256(proj(1024))
    rows_ref[:, 256:512] = proj(1280)
    rows_ref[:, 512:768] = rope256(proj(1536))
    rows_ref[:, 768:1024] = proj(1792)
    win_ref[:, 0:256] = rope256(proj(2048))
    win_ref[:, 256:512] = proj(2304)
    gate_ref[...] = jax.nn.sigmoid(_dot(h, w_ref[:, 2560:2688]))


def _nsa_proj(x, g, w, cos, sa, sb):
    m, d = x.shape
    ntab = cos.shape[0] // TM
    row = lambda i: (i, 0)
    tab = lambda i: (i % ntab, 0)
    return pl.pallas_call(
        _nsa_proj_body,
        grid=(m // TM,),
        in_specs=[pl.BlockSpec((TM, d), row), _resident(g.shape), _resident(w.shape),
                  pl.BlockSpec((TM, LANES), tab), pl.BlockSpec((TM, LANES), tab), pl.BlockSpec((TM, LANES), tab)],
        out_specs=[pl.BlockSpec((TM, 1024), row), pl.BlockSpec((TM, 1024), row),
                   pl.BlockSpec((TM, 512), row), pl.BlockSpec((TM, LANES), row)],
        out_shape=[jax.ShapeDtypeStruct((m, 1024), BF16), jax.ShapeDtypeStruct((m, 1024), F32),
                   jax.ShapeDtypeStruct((m, 512), F32), jax.ShapeDtypeStruct((m, LANES), F32)],
        compiler_params=_cparams(("arbitrary",)),
        name="nsa_proj",
    )(x, g, w, cos, sa, sb)


def _cmp_bias_body(pe_ref, w_ref, o_ref):
    for s in range(2):
        for half in range(2):
            acc = jnp.zeros((SUBLANES, 256), F32)
            for j in range(CMP_STRIDE):
                acc = acc + _dot(pe_ref[s, half, j].astype(BF16), w_ref[s, j, :, half * 256:(half + 1) * 256])
            o_ref[:, s * 512 + half * 256:s * 512 + (half + 1) * 256] = acc


def _cmp_bias(pe_t, wbd):
    return pl.pallas_call(_cmp_bias_body, out_shape=jax.ShapeDtypeStruct((SUBLANES, 1024), F32),
                          compiler_params=_cparams(None), name="cmp_bias")(pe_t, wbd)


def _compress_body(x_ref, w_ref, b_ref, o_ref):
    for s in range(2):
        acc = jnp.zeros((x_ref.shape[0], 512), F32)
        for j in range(CMP_STRIDE):
            acc = acc + _dot(x_ref[:, j, s * 256:(s + 1) * 256].astype(BF16), w_ref[s, j])
        o_ref[:, s * 512:(s + 1) * 512] = acc + b_ref[0:1, s * 512:(s + 1) * 512]


def _compress(x3, wbd, bias):
    r = x3.shape[0]
    return pl.pallas_call(
        _compress_body,
        grid=(r // CMP_ROWS,),
        in_specs=[pl.BlockSpec((CMP_ROWS, CMP_STRIDE, 512), lambda i: (i, 0, 0)),
                  _resident(wbd.shape), _resident(bias.shape)],
        out_specs=pl.BlockSpec((CMP_ROWS, 1024), lambda i: (i, 0)),
        out_shape=jax.ShapeDtypeStruct((r, 1024), F32),
        compiler_params=_cparams(("arbitrary",)),
        name="compress",
    )(x3, wbd, bias)


def _finish_compress(fs):
    n = fs.shape[0]
    kc = fs[:, 0:256] + pltpu.roll(fs[:, 256:512], n - 1, axis=0)
    vc = fs[:, 512:768] + pltpu.roll(fs[:, 768:1024], n - 1, axis=0)
    return kc.astype(BF16), vc.astype(BF16)


def _select_mask(sc_raw, jr, pos, n_real):
    sc = jnp.where(jr * SEL_BLOCK > pos, -FORCE, sc_raw)
    sc = jnp.where(jr == pos // SEL_BLOCK, FORCE, jnp.where(jr == 0, FORCE, sc))
    sc = jnp.where(jr >= n_real, -2.0 * FORCE, sc)
    cnt = jnp.zeros(sc.shape, F32)
    for i in range(n_real):
        ri = sc[i:i + 1, :]
        ge = jnp.where(ri >= sc, 1.0, 0.0)
        gt = jnp.where(ri > sc, 1.0, 0.0)
        cnt = cnt + jnp.where(jr > i, ge, gt)
    keep = jnp.where(sc > -0.5 * FORCE, 0.0, NEG)
    return jnp.where(cnt < float(min(TOP_N, n_real)), keep, NEG)


def _cmp_prompt_body(q_ref, fs_ref, ov_ref, oc_ref, qp_ref):
    i = pl.program_id(1)
    tq = q_ref.shape[0]
    nblk = fs_ref.shape[0]
    kc, vc = _finish_compress(fs_ref[...])
    t_col = i * tq + lax.broadcasted_iota(jnp.int32, (tq, nblk), 0)
    n_row = lax.broadcasted_iota(jnp.int32, (tq, nblk), 1)
    vis = n_row * CMP_STRIDE + 2 * CMP_STRIDE - 1 <= t_col
    t_row = i * tq + lax.broadcasted_iota(jnp.int32, (nblk, tq), 1)
    n_col = lax.broadcasted_iota(jnp.int32, (nblk, tq), 0)
    vis_t = n_col * CMP_STRIDE + 2 * CMP_STRIDE - 1 <= t_row
    nsel = ov_ref.shape[0]
    jr = lax.broadcasted_iota(jnp.int32, (nsel, tq), 0)
    pos = i * tq + lax.broadcasted_iota(jnp.int32, (nsel, tq), 1)
    eye = jnp.where(lax.broadcasted_iota(jnp.int32, (tq, tq), 0) == lax.broadcasted_iota(jnp.int32, (tq, tq), 1),
                    1.0, 0.0).astype(BF16)
    for g in range(NSA_KV_HEADS):
        kg = kc[:, g * HEAD_DIM:(g + 1) * HEAD_DIM]
        vg = vc[:, g * HEAD_DIM:(g + 1) * HEAD_DIM]
        ps_t = jnp.zeros((nblk, tq), F32)
        qs = []
        for r in range(NSA_GROUP):
            hd = g * NSA_GROUP + r
            qh = q_ref[:, hd * HEAD_DIM:(hd + 1) * HEAD_DIM]
            qs.append(qh)
            p = _softmax_masked(_nt(qh, kg), vis, -1)
            oc_ref[:, hd * HEAD_DIM:(hd + 1) * HEAD_DIM] = _dot(p.astype(BF16), vg).astype(oc_ref.dtype)
            ps_t = ps_t + _softmax_masked(_nt(kg, qh), vis_t, 0)
        a_t = _select_mask(_split_dot(ov_ref[...], ps_t), jr, pos, nsel)
        a_t = jnp.concatenate([a_t, jnp.zeros((HEAD_DIM - nsel, tq), F32)], axis=0).astype(BF16)
        a = _nt(eye, a_t).astype(BF16)
        for r in range(NSA_GROUP):
            qp_ref[0, g * NSA_GROUP + r] = jnp.concatenate([qs[r], a], axis=1)


def _cmp_prompt(q, fs, ov_t, b, t):
    nq = t // TQ
    nblk = t // CMP_STRIDE
    return pl.pallas_call(
        _cmp_prompt_body,
        grid=(b, nq),
        in_specs=[pl.BlockSpec((TQ, 1024), lambda bi, i: (bi * nq + i, 0)),
                  pl.BlockSpec((nblk, 1024), lambda bi, i: (bi, 0)),
                  _resident(ov_t.shape)],
        out_specs=[pl.BlockSpec((TQ, 1024), lambda bi, i: (bi * nq + i, 0)),
                   pl.BlockSpec((1, NSA_HEADS, TQ, LANES), lambda bi, i: (bi, 0, i, 0))],
        out_shape=[jax.ShapeDtypeStruct((b * t, 1024), BF16),
                   jax.ShapeDtypeStruct((b, NSA_HEADS, t, LANES), BF16)],
        compiler_params=_cparams(("arbitrary", "arbitrary")),
        name="cmp_prompt",
    )(q, fs, ov_t)


def _flash_body(qp_ref, kp_ref, v_ref, o_ref, *, window):
    i = pl.program_id(2)
    tq = qp_ref.shape[2]
    rows = NSA_GROUP * tq
    q = qp_ref[0].reshape(rows, LANES)
    t = i * tq + lax.broadcasted_iota(jnp.int32, (tq, tq), 0)
    t = jnp.concatenate([t] * NSA_GROUP, axis=0)
    n0 = lax.broadcasted_iota(jnp.int32, (rows, tq), 1)

    def chunk(c, carry):
        m, l, acc = carry
        off = pl.multiple_of(c * tq, tq)
        k = kp_ref[0, 0, pl.ds(off, tq), :]
        v = v_ref[0, 0, pl.ds(off, tq), :]
        s = _nt(q, k)
        d = t - (n0 + c * tq)
        valid = d >= 0
        if window is not None:
            valid = jnp.logical_and(valid, d <= window)
        s = jnp.where(valid, s, NEG)
        m_new = jnp.maximum(m, jnp.max(s, axis=-1, keepdims=True))
        alpha = jnp.exp(m - m_new)
        p = jnp.exp(s - m_new)
        l = alpha * l + jnp.sum(p, axis=-1, keepdims=True)
        acc = alpha * acc + _dot(p.astype(BF16), v)
        return m_new, l, acc

    lo = 0 if window is None else jnp.maximum(i - window // tq, 0)
    m0 = jnp.full((rows, 1), NEG, F32)
    l0 = jnp.zeros((rows, 1), F32)
    a0 = jnp.zeros((rows, HEAD_DIM), F32)
    _, l, acc = lax.fori_loop(lo, i + 1, chunk, (m0, l0, a0))
    o = acc / l
    for r in range(NSA_GROUP):
        o_ref[:, r * HEAD_DIM:(r + 1) * HEAD_DIM] = o[r * tq:(r + 1) * tq].astype(o_ref.dtype)


def _flash(qp, kp, v, window):
    b, _, t, _ = qp.shape
    nq = t // TQ
    return pl.pallas_call(
        functools.partial(_flash_body, window=window),
        grid=(b, NSA_KV_HEADS, nq),
        in_specs=[pl.BlockSpec((1, NSA_GROUP, TQ, LANES), lambda bi, g, i: (bi, g, i, 0)),
                  pl.BlockSpec((1, 1, t, LANES), lambda bi, g, i: (bi, g, 0, 0)),
                  pl.BlockSpec((1, 1, t, HEAD_DIM), lambda bi, g, i: (bi, g, 0, 0))],
        out_specs=pl.BlockSpec((TQ, NSA_GROUP * HEAD_DIM), lambda bi, g, i: (bi * nq + i, g)),
        out_shape=jax.ShapeDtypeStruct((b * t, 1024), BF16),
        compiler_params=_cparams(("arbitrary", "arbitrary", "arbitrary")),
        name="flash_win" if window is not None else "flash_slc",
    )(qp, kp, v)


def _diag_blocks(o_full, o_ref):
    rows = o_ref.shape[1] // NSA_KV_HEADS
    for g in range(NSA_KV_HEADS):
        o_ref[0, g * rows:(g + 1) * rows, :] = o_full[g * rows:(g + 1) * rows, g * HEAD_DIM:(g + 1) * HEAD_DIM]


def _cmp_sample_body(pt_ref, *refs, n_pages, past_len, dec):
    fs_refs = refs[:n_pages]
    qt_ref, ov_ref, msum_ref, oc_ref, at_ref = refs[n_pages:]
    fs = jnp.concatenate([r[...] for r in fs_refs], axis=0)
    nblk = fs.shape[0]
    kc, vc = _finish_compress(fs)
    qt = qt_ref[0]
    nq = qt.shape[0]
    pos_c = past_len + (lax.broadcasted_iota(jnp.int32, (nq, nblk), 0) // NSA_GROUP) % dec
    n_r = lax.broadcasted_iota(jnp.int32, (nq, nblk), 1)
    vis = n_r * CMP_STRIDE + 2 * CMP_STRIDE - 1 <= pos_c
    vis = jnp.logical_and(vis, n_r < nblk - 1)
    p = _softmax_masked(_nt(qt, kc), vis, -1)
    _diag_blocks(_dot(p.astype(BF16), vc), oc_ref)
    pos_l = past_len + (lax.broadcasted_iota(jnp.int32, (nblk, nq), 1) // NSA_GROUP) % dec
    n_c = lax.broadcasted_iota(jnp.int32, (nblk, nq), 0)
    vis_t = jnp.logical_and(n_c * CMP_STRIDE + 2 * CMP_STRIDE - 1 <= pos_l, n_c < nblk - 1)
    p_t = _softmax_masked(_nt(kc, qt), vis_t, 0)
    a1 = _split_dot(ov_ref[...], p_t)
    hi = a1.astype(BF16)
    lo = (a1 - hi.astype(F32)).astype(BF16)
    sc = _dot(hi, msum_ref[...]) + _dot(lo, msum_ref[...])
    nj = ov_ref.shape[0]
    jr = lax.broadcasted_iota(jnp.int32, (nj, nq), 0)
    pos = past_len + (lax.broadcasted_iota(jnp.int32, (nj, nq), 1) // NSA_GROUP) % dec
    n_sel = -(-(past_len + dec) // SEL_BLOCK)
    at_ref[0] = _select_mask(sc, jr, pos, n_sel)


def _cmp_sample(page_table, fs_phys, qt, ov_t, msum, past_len, dec):
    nb, n_pages = page_table.shape
    nj = ov_t.shape[0]
    fs_specs = [pl.BlockSpec((SUBLANES, 1024), functools.partial(lambda b, pt, k: (pt[b, k], 0), k=k))
                for k in range(n_pages)]
    grid_spec = pltpu.PrefetchScalarGridSpec(
        num_scalar_prefetch=1,
        grid=(nb,),
        in_specs=fs_specs + [pl.BlockSpec((1, LANES, 256), lambda b, pt: (b, 0, 0)),
                             pl.BlockSpec(ov_t.shape, lambda b, pt: (0, 0)),
                             pl.BlockSpec(msum.shape, lambda b, pt: (0, 0))],
        out_specs=[pl.BlockSpec((1, 64, HEAD_DIM), lambda b, pt: (b, 0, 0)),
                   pl.BlockSpec((1, nj, LANES), lambda b, pt: (b, 0, 0))],
    )
    return pl.pallas_call(
        functools.partial(_cmp_sample_body, n_pages=n_pages, past_len=past_len, dec=dec),
        grid_spec=grid_spec,
        out_shape=[jax.ShapeDtypeStruct((nb, 64, HEAD_DIM), F32),
                   jax.ShapeDtypeStruct((nb, nj, LANES), F32)],
        compiler_params=_cparams(("arbitrary",)),
        name="cmp_sample",
    )(page_table, *([fs_phys] * n_pages), qt, ov_t, msum)


def _attn_sample_body(*refs, n_blocks, has_pt, has_bm):
    refs = refs[1:] if has_pt else refs
    kv_refs = refs[:n_blocks]
    rest = list(refs[n_blocks:])
    new_ref, qt_ref = rest[0], rest[1]
    bm_ref = rest[2] if has_bm else None
    rm_ref, o_ref, st_ref = rest[-3], rest[-2], rest[-1]
    qt = qt_ref[0]
    rows = kv_refs[0].shape[1]
    per = rows // SEL_BLOCK

    def blk_mask(j0, count):
        parts = [jnp.broadcast_to(bm_ref[0, j0 + u:j0 + u + 1, :], (SEL_BLOCK, LANES)) for u in range(count)]
        return parts[0] if count == 1 else jnp.concatenate(parts, axis=0)

    for kb in range(n_blocks):
        k = kv_refs[kb][0, :, 0:256].astype(BF16)
        s = _nt(k, qt) + rm_ref[kb * rows:(kb + 1) * rows, :]
        if has_bm:
            s = s + blk_mask(kb * per, per)
        st_ref[kb * rows:(kb + 1) * rows, :] = s
    base = n_blocks * rows
    s_new = _nt(new_ref[0, :, 0:256].astype(BF16), qt) + rm_ref[base:base + SUBLANES, :]
    if has_bm:
        s_new = s_new + bm_ref[0, n_blocks * per:n_blocks * per + 1, :]
    st_ref[base:base + SUBLANES, :] = s_new
    st = st_ref[...]
    m = jnp.max(st, axis=0, keepdims=True)
    e = jnp.exp(st - m)
    p = (e / jnp.maximum(jnp.sum(e, axis=0, keepdims=True), 1e-30)).astype(BF16)
    acc = _tn(p[base:base + SUBLANES], new_ref[0, :, 256:512].astype(BF16))
    for kb in range(n_blocks):
        acc = acc + _tn(p[kb * rows:(kb + 1) * rows], kv_refs[kb][0, :, 256:512].astype(BF16))
    _diag_blocks(acc, o_ref)


def _attn_sample(kv, kv_rows, n_blocks, new, qt, row_mask, page_table=None, blk_mask=None):
    nb = qt.shape[0]
    has_pt = page_table is not None
    has_bm = blk_mask is not None
    ntot = n_blocks * kv_rows + SUBLANES
    if has_pt:
        kv_specs = [pl.BlockSpec((1, kv_rows, 512), functools.partial(lambda b, pt, k: (pt[b, k], 0, 1), k=k))
                    for k in range(n_blocks)]
        im = lambda b, pt: (b, 0, 0)
        cm = lambda b, pt: (0, 0)
    else:
        kv_specs = [pl.BlockSpec((1, kv_rows, 512), lambda b: (b, 0, 0))]
        im = lambda b: (b, 0, 0)
        cm = lambda b: (0, 0)
    in_specs = kv_specs + [pl.BlockSpec((1, SUBLANES, 512), im), pl.BlockSpec((1, LANES, 256), im)]
    args = [kv] * n_blocks + [new, qt]
    if has_bm:
        in_specs.append(pl.BlockSpec((1,) + blk_mask.shape[1:], im))
        args.append(blk_mask)
    in_specs.append(pl.BlockSpec(row_mask.shape, cm))
    args.append(row_mask)
    out_spec = pl.BlockSpec((1, 64, HEAD_DIM), im)
    scratch = [pltpu.VMEM((ntot, LANES), F32)]
    body = functools.partial(_attn_sample_body, n_blocks=n_blocks, has_pt=has_pt, has_bm=has_bm)
    out_shape = jax.ShapeDtypeStruct((nb, 64, HEAD_DIM), F32)
    if has_pt:
        gs = pltpu.PrefetchScalarGridSpec(num_scalar_prefetch=1, grid=(nb,), in_specs=in_specs,
                                          out_specs=out_spec, scratch_shapes=scratch)
        return pl.pallas_call(body, grid_spec=gs, out_shape=out_shape, compiler_params=_cparams(("arbitrary",)),
                              name="slc_sample")(page_table, *args)
    return pl.pallas_call(body, grid=(nb,), in_specs=in_specs, out_specs=out_spec, scratch_shapes=scratch,
                          out_shape=out_shape, compiler_params=_cparams(("arbitrary",)), name="win_sample")(*args)


def _out_proj_body(*refs, n_in, gated):
    o_refs = refs[:n_in]
    rest = refs[n_in:]
    if gated:
        gate_ref, rest = rest[0], rest[1:]
    x_ref, w_ref, g_ref, y_ref = rest
    tm = x_ref.shape[0]
    if gated:
        lane = lax.broadcasted_iota(jnp.int32, (tm, LANES), 1)
        gates = gate_ref[...]
        chunks = []
        for c in range(1024 // LANES):
            acc = jnp.zeros((tm, LANES), F32)
            for kbr in range(n_in):
                c0 = (2 * c) * 3 + kbr
                c1 = (2 * c + 1) * 3 + kbr
                gexp = jnp.where(lane < HEAD_DIM, gates[:, c0:c0 + 1], gates[:, c1:c1 + 1])
                acc = acc + gexp * o_refs[kbr][:, c * LANES:(c + 1) * LANES].astype(F32)
            chunks.append(acc.astype(BF16))
        o = jnp.concatenate(chunks, axis=1)
    else:
        o = o_refs[0][...].astype(BF16)
    y = _dot(o, w_ref[...])
    y_ref[...] = x_ref[...] + _rms(y, g_ref[...])


def _out_proj(os_, gates, x, w, g):
    m, d = x.shape
    row = lambda i: (i, 0)
    gated = gates is not None
    in_specs = [pl.BlockSpec((TM, 1024), row) for _ in os_]
    args = list(os_)
    if gated:
        in_specs.append(pl.BlockSpec((TM, LANES), row))
        args.append(gates)
    in_specs += [pl.BlockSpec((TM, d), row), _resident(w.shape), _resident(g.shape)]
    args += [x, w, g]
    return pl.pallas_call(
        functools.partial(_out_proj_body, n_in=len(os_), gated=gated),
        grid=(m // TM,), in_specs=in_specs, out_specs=pl.BlockSpec((TM, d), row),
        out_shape=jax.ShapeDtypeStruct((m, d), F32),
        compiler_params=_cparams(("arbitrary",)), name="out_proj",
    )(*args)


def _ffn_body(*refs, sample, tiles_per_seq, nb):
    if sample:
        x_ref, g2_ref, g3_ref, wu_ref, cw_ref, cb_ref, wd_ref, buf_ref, y_ref, tail_ref = refs
    else:
        x_ref, g2_ref, g3_ref, wu_ref, cw_ref, cb_ref, wd_ref, y_ref, tail_ref, carry_ref = refs
    x = x_ref[...]
    tm = x.shape[0]
    f = wd_ref.shape[0]
    h = _rms(x, g2_ref[...]).astype(BF16)
    acc = jnp.zeros(x.shape, F32)
    if not sample:
        first = pl.program_id(0) % tiles_per_seq == 0
        row = lax.broadcasted_iota(jnp.int32, (tm, FFN_FCHUNK), 0)
    for fc in range(f // FFN_FCHUNK):
        sl = slice(fc * FFN_FCHUNK, (fc + 1) * FFN_FCHUNK)
        gate = _dot(h, wu_ref[:, sl])
        val = _dot(h, wu_ref[:, f + fc * FFN_FCHUNK:f + (fc + 1) * FFN_FCHUNK])
        if sample:
            prev1 = jnp.concatenate([buf_ref[nb:2 * nb, sl], gate[0:tm - nb]], axis=0)
            prev2 = jnp.concatenate([buf_ref[0:2 * nb, sl], gate[0:tm - 2 * nb]], axis=0)
            tail_ref[:, sl] = gate[tm - 2 * nb:tm]
        else:
            carry = carry_ref[:, sl]
            c6 = jnp.where(first, 0.0, carry[6:7, :])
            c7 = jnp.where(first, 0.0, carry[7:8, :])
            prev1 = jnp.where(row == 0, c7, pltpu.roll(gate, 1, axis=0))
            prev2 = jnp.where(row == 0, c6, jnp.where(row == 1, c7, pltpu.roll(gate, 2, axis=0)))
            carry_ref[:, sl] = gate[tm - SUBLANES:tm]
            tail_ref[:, sl] = gate[tm - SUBLANES:tm]
        conv = cb_ref[0:1, sl] + cw_ref[0:1, sl] * prev2 + cw_ref[1:2, sl] * prev1 + cw_ref[2:3, sl] * gate
        act = (jax.nn.gelu(conv) * val).astype(BF16)
        acc = acc + _dot(act, wd_ref[sl, :])
    y_ref[...] = x + _rms(acc, g3_ref[...])


def _ffn(x, g2, g3, wu, cw, cb, wd, seq_len=None, buf=None):
    m, d = x.shape
    f = wd.shape[0]
    sample = buf is not None
    row = lambda i: (i, 0)
    if sample:
        tm = m
        nb = buf.shape[0] // 2
        tail_shape, tail_block = (2 * nb, f), (2 * nb, f)
        scratch = []
    else:
        tm = TM
        nb = 0
        tail_shape, tail_block = (m // tm * SUBLANES, f), (SUBLANES, f)
        scratch = [pltpu.VMEM((SUBLANES, f), F32)]
    in_specs = [pl.BlockSpec((tm, d), row), _resident(g2.shape), _resident(g3.shape), _resident(wu.shape),
                _resident(cw.shape), _resident(cb.shape), _resident(wd.shape)]
    args = [x, g2, g3, wu, cw, cb, wd]
    if sample:
        in_specs.append(_resident(buf.shape))
        args.append(buf)
    return pl.pallas_call(
        functools.partial(_ffn_body, sample=sample, tiles_per_seq=(seq_len // tm if not sample else 1), nb=nb),
        grid=(m // tm,), in_specs=in_specs,
        out_specs=[pl.BlockSpec((tm, d), row), pl.BlockSpec(tail_block, row)],
        out_shape=[jax.ShapeDtypeStruct((m, d), F32), jax.ShapeDtypeStruct(tail_shape, F32)],
        scratch_shapes=scratch,
        compiler_params=_cparams(("arbitrary",)), name="ffn_sample" if sample else "ffn_prompt",
    )(*args)


def _gla_proj_body(x_ref, g_ref, w_ref, wg_ref, bg_ref, q_ref, k_ref, v_ref, sr_ref, la_ref):
    h = _rms(x_ref[...], g_ref[...]).astype(BF16)
    nk = GLA_HEADS * GLA_DK
    nv = GLA_HEADS * GLA_DV
    for c in range(nk // 256):
        q_ref[:, c * 256:(c + 1) * 256] = _dot(h, w_ref[:, c * 256:(c + 1) * 256]) * (GLA_DK ** -0.5)
        k_ref[:, c * 256:(c + 1) * 256] = _dot(h, w_ref[:, nk + c * 256:nk + (c + 1) * 256])
    for c in range(nv // 256):
        v_ref[:, c * 256:(c + 1) * 256] = _dot(h, w_ref[:, 2 * nk + c * 256:2 * nk + (c + 1) * 256]).astype(BF16)
        r = _dot(h, w_ref[:, 2 * nk + nv + c * 256:2 * nk + nv + (c + 1) * 256])
        sr_ref[:, c * 256:(c + 1) * 256] = jax.nn.silu(r).astype(BF16)
    low = _dot(h, w_ref[:, 2 * nk + 2 * nv:2 * nk + 2 * nv + LANES]).astype(BF16)
    gz = _dot(low, wg_ref[...]) + bg_ref[...]
    log_sig = jnp.minimum(gz, 0.0) - jnp.log1p(jnp.exp(-jnp.abs(gz)))
    la_ref[...] = log_sig / GLA_GATE_TEMP


def _gla_proj(x, g, w, wg, bg):
    m, d = x.shape
    nk = GLA_HEADS * GLA_DK
    nv = GLA_HEADS * GLA_DV
    row = lambda i: (i, 0)
    return pl.pallas_call(
        _gla_proj_body, grid=(m // TM,),
        in_specs=[pl.BlockSpec((TM, d), row), _resident(g.shape), _resident(w.shape),
                  _resident(wg.shape), _resident(bg.shape)],
        out_specs=[pl.BlockSpec((TM, nk), row), pl.BlockSpec((TM, nk), row), pl.BlockSpec((TM, nv), row),
                   pl.BlockSpec((TM, nv), row), pl.BlockSpec((TM, nk), row)],
        out_shape=[jax.ShapeDtypeStruct((m, nk), F32), jax.ShapeDtypeStruct((m, nk), F32),
                   jax.ShapeDtypeStruct((m, nv), BF16), jax.ShapeDtypeStruct((m, nv), BF16),
                   jax.ShapeDtypeStruct((m, nk), F32)],
        compiler_params=_cparams(("arbitrary",)), name="gla_proj",
    )(x, g, w, wg, bg)


def _cumsum_rows(x):
    n = x.shape[0]
    row = lax.broadcasted_iota(jnp.int32, x.shape, 0)
    sh = 1
    while sh < n:
        x = x + jnp.where(row >= sh, pltpu.roll(x, sh, axis=0), 0.0)
        sh *= 2
    return x


def _gla_rec_body(*refs, has_s0):
    if has_s0:
        q_ref, k_ref, la_ref, v_ref, sr_ref, gn_ref, s0_ref, o_ref, so_ref, st_ref = refs
    else:
        q_ref, k_ref, la_ref, v_ref, sr_ref, gn_ref, o_ref, so_ref, st_ref = refs
    c = pl.program_id(1)
    cs = q_ref.shape[0]

    @pl.when(c == 0)
    def _():
        st_ref[...] = s0_ref[0] if has_s0 else jnp.zeros(st_ref.shape, F32)

    row = lax.broadcasted_iota(jnp.int32, (cs, 1), 0)
    trow = lax.broadcasted_iota(jnp.int32, (cs, cs), 0)
    scol = lax.broadcasted_iota(jnp.int32, (cs, cs), 1)
    r8 = lax.broadcasted_iota(jnp.int32, (SUBLANES, 1), 0)
    lane8 = lax.broadcasted_iota(jnp.int32, (SUBLANES, cs), 1)
    for hd in range(GLA_HEADS):
        ksl = slice(hd * GLA_DK, (hd + 1) * GLA_DK)
        vsl = slice(hd * GLA_DV, (hd + 1) * GLA_DV)
        q, k, v = q_ref[:, ksl], k_ref[:, ksl], v_ref[:, vsl]
        cum = _cumsum_rows(la_ref[:, ksl])
        last = cum[cs - 1:cs, :]
        s_old = st_ref[hd]
        out = _dot((q * jnp.exp(cum)).astype(BF16), s_old.astype(BF16))
        att = None
        hh = cs // 2
        while hh >= SUBLANES:
            nblk = cs // (2 * hh)
            ref = jnp.concatenate(
                [jnp.broadcast_to(cum[u * 2 * hh + hh - 1:u * 2 * hh + hh, :], (2 * hh, GLA_DK)) for u in range(nblk)],
                axis=0) if nblk > 1 else jnp.broadcast_to(cum[hh - 1:hh, :], (cs, GLA_DK))
            second = (row % (2 * hh)) >= hh
            qh = jnp.where(second, q * jnp.exp(jnp.minimum(cum - ref, 0.0)), 0.0).astype(BF16)
            kh = jnp.where(second, 0.0, k * jnp.exp(jnp.minimum(ref - cum, 0.0))).astype(BF16)
            a = _nt(qh, kh)
            if nblk > 1:
                a = jnp.where(trow // (2 * hh) == scol // (2 * hh), a, 0.0)
            att = a if att is None else att + a
            hh //= 2
        slabs = []
        for g8 in range(cs // SUBLANES):
            r0 = g8 * SUBLANES
            cg, qg, kg = cum[r0:r0 + SUBLANES], q[r0:r0 + SUBLANES], k[r0:r0 + SUBLANES]
            slab = jnp.zeros((SUBLANES, GLA_DV if att is None else cs), F32)
            for s in range(SUBLANES):
                e = jnp.exp(jnp.minimum(cg - cg[s:s + 1, :], 0.0))
                col = jnp.sum(e * qg * kg[s:s + 1, :], axis=1, keepdims=True)
                col = jnp.where(r8 >= s, col, 0.0)
                if att is None:
                    slab = slab + col * v[r0 + s:r0 + s + 1, :].astype(F32)
                else:
                    slab = jnp.where(lane8 == r0 + s, col, slab)
            slabs.append(slab)
        if att is None:
            out = out + slabs[0]
        else:
            att = att + jnp.concatenate(slabs, axis=0)
            out = out + _dot(att.astype(BF16), v)
        kt = (k * jnp.exp(last - cum)).astype(BF16)
        dcol = jnp.broadcast_to(jnp.exp(last), (SUBLANES, GLA_DK)).T[:, 0:1]
        st_ref[hd] = dcol * s_old + _tn(kt, v)
        on = _rms(out, gn_ref[...])
        o_ref[:, vsl] = (on * sr_ref[:, vsl].astype(F32)).astype(o_ref.dtype)

    @pl.when(c == pl.num_programs(1) - 1)
    def _():
        so_ref[0] = st_ref[...]


def _gla_rec(q, k, la, v, sr, gn, nb, chunk, s0=None):
    m = q.shape[0]
    nc = m // nb // chunk
    nk = GLA_HEADS * GLA_DK
    nv = GLA_HEADS * GLA_DV
    row = lambda b, c: (b * nc + c, 0)
    st_spec = pl.BlockSpec((1, GLA_HEADS, GLA_DK, GLA_DV), lambda b, c: (b, 0, 0, 0))
    in_specs = [pl.BlockSpec((chunk, nk), row), pl.BlockSpec((chunk, nk), row), pl.BlockSpec((chunk, nk), row),
                pl.BlockSpec((chunk, nv), row), pl.BlockSpec((chunk, nv), row),
                pl.BlockSpec(gn.shape, lambda b, c: (0, 0))]
    args = [q, k, la, v, sr, gn]
    if s0 is not None:
        in_specs.append(st_spec)
        args.append(s0)
    return pl.pallas_call(
        functools.partial(_gla_rec_body, has_s0=s0 is not None),
        grid=(nb, nc), in_specs=in_specs,
        out_specs=[pl.BlockSpec((chunk, nv), row), st_spec],
        out_shape=[jax.ShapeDtypeStruct((m, nv), BF16),
                   jax.ShapeDtypeStruct((nb, GLA_HEADS, GLA_DK, GLA_DV), F32)],
        scratch_shapes=[pltpu.VMEM((GLA_HEADS, GLA_DK, GLA_DV), F32)],
        compiler_params=_cparams(("arbitrary", "arbitrary")),
        name="gla_rec_sample" if s0 is not None else "gla_rec_prompt",
    )(*args)


def _rope_tables(pos):
    half = HEAD_DIM // 2
    inv = ROPE_THETA ** (-jnp.arange(half, dtype=F32) / half)
    ang = pos.astype(F32)[:, None] * inv[None, :]
    cos, sin = jnp.cos(ang), jnp.sin(ang)
    z = jnp.zeros_like(sin)
    cos_t = jnp.tile(cos, (1, 4))
    sa = jnp.tile(jnp.concatenate([-sin, z], axis=1), (1, 2))
    sb = jnp.tile(jnp.concatenate([z, sin], axis=1), (1, 2))
    return cos_t, sa, sb


def _pad_cols(w, n):
    return jnp.pad(w, ((0, 0), (0, n - w.shape[1])))


def _cmp_weights(cmp_pe, cmp_w):
    g = NSA_KV_HEADS
    w = cmp_w.reshape(2, 2, CMP_STRIDE, HEAD_DIM, HEAD_DIM)
    eye = jnp.eye(g, dtype=F32)
    wbd = jnp.einsum('shjde,gk->sjgdhke', w, eye)
    wbd = wbd.reshape(2, CMP_STRIDE, g * HEAD_DIM, 2 * g * HEAD_DIM).astype(BF16)
    pe = cmp_pe.reshape(2, 2, CMP_STRIDE, 1, 1, HEAD_DIM)
    pe_t = jnp.broadcast_to(pe, (2, 2, CMP_STRIDE, SUBLANES, g, HEAD_DIM)).reshape(2, 2, CMP_STRIDE, SUBLANES, g * HEAD_DIM)
    return wbd, pe_t


def _overlap_t(n_blk, n_sel, rows, cols):
    i = jnp.arange(cols)[None, :]
    j = jnp.arange(rows)[:, None]
    ov = (i * CMP_STRIDE + 2 * CMP_STRIDE > j * SEL_BLOCK) & (i * CMP_STRIDE < (j + 1) * SEL_BLOCK)
    ov = ov & (i < n_blk) & (j < n_sel)
    return ov.astype(BF16)


def _nsa_layer_prompt(x, g, w_in, wbd, cbias, w_o, tabs, b, t):
    q, rows, win, gates = _nsa_proj(x, g[0:1], w_in, *tabs)
    fs = _compress(rows.reshape(b * t // CMP_STRIDE, CMP_STRIDE, 1024), wbd, cbias)
    n_blk = t // CMP_STRIDE - 1
    n_sel = -(-t // SEL_BLOCK)
    oc, qp = _cmp_prompt(q, fs, _overlap_t(n_blk, n_sel, n_sel, t // CMP_STRIDE), b, t)

    def heads(a):
        return a.reshape(b, t, NSA_KV_HEADS, HEAD_DIM).transpose(0, 2, 1, 3).astype(BF16)

    onehot = (jnp.arange(t)[:, None] // SEL_BLOCK == jnp.arange(HEAD_DIM)[None, :]).astype(BF16)
    onehot = jnp.where(jnp.arange(HEAD_DIM)[None, :] < n_sel, onehot, 0)
    k_slc = jnp.concatenate([heads(rows[:, 512:768]),
                             jnp.broadcast_to(onehot, (b, NSA_KV_HEADS, t, HEAD_DIM))], axis=-1)
    k_win = jnp.concatenate([heads(win[:, 0:256]), jnp.zeros((b, NSA_KV_HEADS, t, HEAD_DIM), BF16)], axis=-1)
    o_s = _flash(qp, k_slc, heads(rows[:, 768:1024]), None)
    o_w = _flash(qp, k_win, heads(win[:, 256:512]), WINDOW)
    x = _out_proj([oc, o_s, o_w], gates, x, w_o, g[1:2])
    return x, rows, win


def _to_rows(o, nb, dec):
    o = o.reshape(nb, NSA_KV_HEADS, dec, NSA_GROUP, HEAD_DIM).transpose(2, 0, 1, 3, 4)
    return o.reshape(dec * nb, NSA_HEADS * HEAD_DIM).astype(BF16)


def _nsa_layer_sample(x, g, w_in, wbd, cbias, w_o, tabs, cache, page_table, win_state, nb, dec):
    n_phys, page = cache.shape[0], cache.shape[1]
    n_pages = page_table.shape[1]
    past_len = n_pages * page
    q, rows, win, gates = _nsa_proj(x, g[0:1], w_in, *tabs)
    fs_phys = _compress(cache.reshape(n_phys * page // CMP_STRIDE, CMP_STRIDE, 1024), wbd, cbias)
    q5 = q.reshape(dec, nb, NSA_KV_HEADS, NSA_GROUP, HEAD_DIM).transpose(1, 2, 0, 3, 4)
    eye = jnp.eye(NSA_KV_HEADS, dtype=BF16)
    qt = q5.reshape(nb, NSA_KV_HEADS, dec * NSA_GROUP, 1, HEAD_DIM) * eye[None, :, None, :, None]
    qt = qt.reshape(nb, NSA_KV_HEADS * dec * NSA_GROUP, NSA_KV_HEADS * HEAD_DIM)
    qt = jnp.pad(qt, ((0, 0), (0, LANES - qt.shape[1]), (0, 0)))
    n_blk = (past_len + dec) // CMP_STRIDE - 1
    n_sel = -(-(past_len + dec) // SEL_BLOCK)
    nj = -(-n_sel // SUBLANES) * SUBLANES
    lane = jnp.arange(LANES)
    msum = (lane[:, None] // NSA_GROUP == lane[None, :] // NSA_GROUP).astype(BF16)
    oc, a_t = _cmp_sample(page_table, fs_phys, qt, _overlap_t(n_blk, n_sel, nj, past_len // CMP_STRIDE), msum, past_len, dec)
    t_lane = (lane // NSA_GROUP) % dec
    j8 = jnp.arange(SUBLANES)[:, None]
    new_ok = (j8 < dec) & (j8 <= t_lane[None, :])
    rm_slc = jnp.concatenate([jnp.zeros((past_len, LANES), F32), jnp.where(new_ok, 0.0, NEG)], axis=0)
    wb = win_state.shape[1]
    idx = jnp.arange(wb + SUBLANES)[:, None]
    win_ok = (idx >= t_lane[None, :]) & (idx <= wb + t_lane[None, :]) & (idx < wb + dec)
    rm_win = jnp.where(win_ok, 0.0, NEG).astype(F32)

    def new_rows(a):
        a = a.reshape(dec, nb, 512).transpose(1, 0, 2)
        return jnp.pad(a, ((0, 0), (0, SUBLANES - dec), (0, 0)))

    o_s = _attn_sample(cache.reshape(n_phys, page, 1024), page, n_pages, new_rows(rows[:, 512:1024]), qt, rm_slc,
                       page_table=page_table, blk_mask=a_t)
    o_w = _attn_sample(win_state.reshape(nb, wb, 512), wb, 1, new_rows(win), qt, rm_win)
    x = _out_proj([_to_rows(oc, nb, dec), _to_rows(o_s, nb, dec), _to_rows(o_w, nb, dec)], gates, x, w_o, g[1:2])
    return x, rows, win


def _gla_layer(x, g, w_in, wg, bg, gn, w_o, nb, seq, s0=None):
    q, k, v, sr, la = _gla_proj(x, g[0:1], w_in, wg, bg)
    if s0 is None:
        o, s = _gla_rec(q, k, la, v, sr, gn, nb, GLA_CHUNK)
    else:
        def seqs(a):
            a = a.reshape(seq, nb, a.shape[1]).transpose(1, 0, 2)
            return jnp.pad(a, ((0, 0), (0, SUBLANES - seq), (0, 0))).reshape(nb * SUBLANES, a.shape[2])
        o, s = _gla_rec(seqs(q), seqs(k), seqs(la), seqs(v), seqs(sr), gn, nb, SUBLANES, s0=s0)
        o = o.reshape(nb, SUBLANES, o.shape[1])[:, :seq].transpose(1, 0, 2).reshape(seq * nb, o.shape[1])
    x = _out_proj([o], None, x, w_o, g[1:2])
    return x, s


def kernel(x_prompt, x_sample, cache_nsa_kv, state_win_kv, state_gla, state_ffn_conv, page_table, norm_gain, nsa_w_in, nsa_cmp_pe, nsa_cmp_w, nsa_w_o, gla_w_in, gla_w_gate_up, gla_b_gate, gla_norm_gain, gla_w_o, ffn_w_up, ffn_conv_w, ffn_conv_b, ffn_w_down):
    b, t, d = x_prompt.shape
    nb, dec, _ = x_sample.shape
    depth = norm_gain.shape[0]
    f = ffn_w_down.shape[1]
    past_len = page_table.shape[1] * cache_nsa_kv.shape[2]

    xp = x_prompt.reshape(b * t, d)
    xs = x_sample.transpose(1, 0, 2).reshape(dec * nb, d)
    tabs_p = _rope_tables(jnp.arange(t, dtype=jnp.int32))
    tabs_s = _rope_tables(past_len + jnp.repeat(jnp.arange(dec, dtype=jnp.int32), nb))

    nsa_p, nsa_s, win_p, win_s, gla_p, gla_s, ffn_p, ffn_s = [], [], [], [], [], [], [], []
    for i in range(depth):
        g = norm_gain[i]
        a = i // 2
        if i % 2 == 0:
            w_in = _pad_cols(nsa_w_in[a], 2688).astype(BF16)
            w_o = nsa_w_o[a].astype(BF16)
            wbd, pe_t = _cmp_weights(nsa_cmp_pe[a], nsa_cmp_w[a])
            cbias = _cmp_bias(pe_t, wbd)
            xp, rows, win = _nsa_layer_prompt(xp, g, w_in, wbd, cbias, w_o, tabs_p, b, t)
            nsa_p.append(rows.reshape(b, t, 4, NSA_KV_HEADS, HEAD_DIM))
            nw = min(WINDOW, t)
            win_p.append(win.reshape(b, t, 2, NSA_KV_HEADS, HEAD_DIM)[:, t - nw:])
            xs, rows, win = _nsa_layer_sample(xs, g, w_in, wbd, cbias, w_o, tabs_s, cache_nsa_kv[a], page_table,
                                              state_win_kv[a], nb, dec)
            nsa_s.append(rows.reshape(dec, nb, 4, NSA_KV_HEADS, HEAD_DIM).transpose(1, 0, 2, 3, 4))
            new_w = win.reshape(dec, nb, 2, NSA_KV_HEADS, HEAD_DIM).transpose(1, 0, 2, 3, 4)
            wb = state_win_kv.shape[2]
            win_s.append(jnp.concatenate([state_win_kv[a], new_w], axis=1)[:, -wb:])
        else:
            w_in = _pad_cols(gla_w_in[a], 3200).astype(BF16)
            wg = jnp.pad(gla_w_gate_up[a], ((0, LANES - GLA_GATE_RANK), (0, 0))).astype(BF16)
            bg = gla_b_gate[a][None, :]
            gn = gla_norm_gain[a][None, :]
            w_o = gla_w_o[a].astype(BF16)
            xp, s = _gla_layer(xp, g, w_in, wg, bg, gn, w_o, b, t)
            gla_p.append(s)
            xs, s = _gla_layer(xs, g, w_in, wg, bg, gn, w_o, nb, dec, s0=state_gla[a])
            gla_s.append(s)
        wu = ffn_w_up[i].astype(BF16)
        wd = ffn_w_down[i].astype(BF16)
        cw = jnp.pad(ffn_conv_w[i], ((0, SUBLANES - ffn_conv_w.shape[1]), (0, 0)))
        cb = ffn_conv_b[i][None, :]
        xp, tail = _ffn(xp, g[2:3], g[3:4], wu, cw, cb, wd, seq_len=t)
        tail = tail.reshape(b, t // TM, SUBLANES, f)[:, -1, SUBLANES - 2:]
        ffn_p.append(tail)
        buf = state_ffn_conv[i].transpose(1, 0, 2).reshape(2 * nb, f)
        xs, tail = _ffn(xs, g[2:3], g[3:4], wu, cw, cb, wd, buf=buf)
        ffn_s.append(tail.reshape(2, nb, f).transpose(1, 0, 2))

    y_prompt = xp.reshape(b, t, d)
    y_sample = xs.reshape(dec, nb, d).transpose(1, 0, 2)
    return (y_prompt, y_sample, jnp.stack(nsa_p), jnp.stack(nsa_s), jnp.stack(win_p), jnp.stack(win_s),
            jnp.stack(gla_p), jnp.stack(gla_s), jnp.stack(ffn_p), jnp.stack(ffn_s))
```

```python
import functools
import math

import jax
import jax.numpy as jnp
from jax import lax
from jax.experimental import pallas as pl
from jax.experimental.pallas import tpu as pltpu

F32, BF16 = jnp.float32, jnp.bfloat16

HEAD_DIM = 64
NSA_HEADS = 16
NSA_KV_HEADS = 4
NSA_GROUP = 4
CMP_STRIDE = 16
SEL_BLOCK = 64
TOP_N = 16
WINDOW = 512
ROPE_THETA = 10000.0
GLA_HEADS = 4
GLA_DK = 128
GLA_DV = 256
GLA_GATE_RANK = 16
GLA_GATE_TEMP = 16.0
NORM_EPS = 1e-6
FORCE = 1e6
NEG = -1e30

LANES = 128
SUBLANES = 8
VMEM_LIMIT = 48 * 1024 * 1024

TM = 256
TQ = 256
CMP_ROWS = 256
GLA_CHUNK = 128
FFN_FCHUNK = 256


def _cparams(sem):
    return pltpu.CompilerParams(dimension_semantics=sem, vmem_limit_bytes=VMEM_LIMIT)


def _resident(shape):
    nd = len(shape)
    return pl.BlockSpec(shape, lambda *_: (0,) * nd, pipeline_mode=pl.Buffered(1))


def _rms(x, g):
    return x * lax.rsqrt(jnp.mean(x * x, axis=-1, keepdims=True) + NORM_EPS) * g


def _nt(a, b):
    return lax.dot_general(a, b, (((1,), (1,)), ((), ())), preferred_element_type=F32)


def _tn(a, b):
    return lax.dot_general(a, b, (((0,), (0,)), ((), ())), preferred_element_type=F32)


def _dot(a, b):
    return jnp.dot(a, b, preferred_element_type=F32)


def _split_dot(w, x):
    hi = x.astype(BF16)
    lo = (x - hi.astype(F32)).astype(BF16)
    return _dot(w, hi) + _dot(w, lo)


def _softmax_masked(s, mask, axis):
    s = jnp.where(mask, s, NEG)
    m = jnp.max(s, axis=axis, keepdims=True)
    e = jnp.where(mask, jnp.exp(s - m), 0.0)
    return e / jnp.maximum(jnp.sum(e, axis=axis, keepdims=True), 1e-30)


def _nsa_proj_body(x_ref, g_ref, w_ref, cos_ref, sa_ref, sb_ref, q_ref, rows_ref, win_ref, gate_ref):
    h = _rms(x_ref[...], g_ref[...]).astype(BF16)
    cos, sa, sb = cos_ref[...], sa_ref[...], sb_ref[...]

    def rope(z):
        return z * cos + pltpu.roll(z, 96, axis=1) * sa + pltpu.roll(z, 32, axis=1) * sb

    def proj(c0):
        return _dot(h, w_ref[:, c0:c0 + 256])

    def rope256(z):
        return jnp.concatenate([rope(z[:, :LANES]), rope(z[:, LANES:])], axis=1)

    for c in range(4):
        q_ref[:, c * 256:(c + 1) * 256] = (rope256(proj(c * 256)) * (HEAD_DIM ** -0.5)).astype(BF16)
    rows_ref[:, 0:256] = rope256(proj(1024))
    rows_ref[:, 256:512] = proj(1280)
    rows_ref[:, 512:768] = rope256(proj(1536))
    rows_ref[:, 768:1024] = proj(1792)
    win_ref[:, 0:256] = rope256(proj(2048))
    win_ref[:, 256:512] = proj(2304)
    gate_ref[...] = jax.nn.sigmoid(_dot(h, w_ref[:, 2560:2688]))


def _nsa_proj(x, g, w, cos, sa, sb):
    m, d = x.shape
    ntab = cos.shape[0] // TM
    row = lambda i: (i, 0)
    tab = lambda i: (i % ntab, 0)
    return pl.pallas_call(
        _nsa_proj_body,
        grid=(m // TM,),
        in_specs=[pl.BlockSpec((TM, d), row), _resident(g.shape), _resident(w.shape),
                  pl.BlockSpec((TM, LANES), tab), pl.BlockSpec((TM, LANES), tab), pl.BlockSpec((TM, LANES), tab)],
        out_specs=[pl.BlockSpec((TM, 1024), row), pl.BlockSpec((TM, 1024), row),
                   pl.BlockSpec((TM, 512), row), pl.BlockSpec((TM, LANES), row)],
        out_shape=[jax.ShapeDtypeStruct((m, 1024), BF16), jax.ShapeDtypeStruct((m, 1024), F32),
                   jax.ShapeDtypeStruct((m, 512), F32), jax.ShapeDtypeStruct((m, LANES), F32)],
        compiler_params=_cparams(("arbitrary",)),
        name="nsa_proj",
    )(x, g, w, cos, sa, sb)


def _cmp_bias_body(pe_ref, w_ref, o_ref):
    for s in range(2):
        for half in range(2):
            acc = jnp.zeros((SUBLANES, 256), F32)
            for j in range(CMP_STRIDE):
                acc = acc + _dot(pe_ref[s, half, j].astype(BF16), w_ref[s, j, :, half * 256:(half + 1) * 256])
            o_ref[:, s * 512 + half * 256:s * 512 + (half + 1) * 256] = acc


def _cmp_bias(pe_t, wbd):
    return pl.pallas_call(_cmp_bias_body, out_shape=jax.ShapeDtypeStruct((SUBLANES, 1024), F32),
                          compiler_params=_cparams(None), name="cmp_bias")(pe_t, wbd)


def _compress_body(x_ref, w_ref, b_ref, o_ref):
    for s in range(2):
        acc = jnp.zeros((x_ref.shape[0], 512), F32)
        for j in range(CMP_STRIDE):
            acc = acc + _dot(x_ref[:, j, s * 256:(s + 1) * 256].astype(BF16), w_ref[s, j])
        o_ref[:, s * 512:(s + 1) * 512] = acc + b_ref[0:1, s * 512:(s + 1) * 512]


def _compress(x3, wbd, bias):
    r = x3.shape[0]
    return pl.pallas_call(
        _compress_body,
        grid=(r // CMP_ROWS,),
        in_specs=[pl.BlockSpec((CMP_ROWS, CMP_STRIDE, 512), lambda i: (i, 0, 0)),
                  _resident(wbd.shape), _resident(bias.shape)],
        out_specs=pl.BlockSpec((CMP_ROWS, 1024), lambda i: (i, 0)),
        out_shape=jax.ShapeDtypeStruct((r, 1024), F32),
        compiler_params=_cparams(("arbitrary",)),
        name="compress",
    )(x3, wbd, bias)


CMP_PITCH = CMP_ROWS + SUBLANES


def _compress_pages_body(x_ref, w_ref, b_ref, o_ref, xs_ref):
    per_page = x_ref.shape[3] // CMP_STRIDE

    def move(p, carry):
        for s in range(2):
            xt = x_ref[p, s].T
            for c in range(per_page):
                for half in range(2):
                    xs_ref[2 * s + half, pl.ds(p * per_page + c, CMP_STRIDE, stride=CMP_PITCH), :] = (
                        xt[c * CMP_STRIDE:(c + 1) * CMP_STRIDE, half * LANES:(half + 1) * LANES])
        return carry

    lax.fori_loop(0, x_ref.shape[0], move, 0, unroll=4)
    rows = o_ref.shape[0]
    for s in range(2):
        acc = jnp.zeros((rows, 512), F32)
        for j in range(CMP_STRIDE):
            xj = jnp.concatenate([xs_ref[2 * s, j * CMP_PITCH:j * CMP_PITCH + rows, :],
                                  xs_ref[2 * s + 1, j * CMP_PITCH:j * CMP_PITCH + rows, :]], axis=1)
            acc = acc + _dot(xj.astype(BF16), w_ref[s, j])
        o_ref[:, s * 512:(s + 1) * 512] = acc + b_ref[0:1, s * 512:(s + 1) * 512]


def _compress_pages(pages_t, wbd, bias):
    n, _, _, page = pages_t.shape
    per_page = page // CMP_STRIDE
    pp = CMP_ROWS // per_page
    return pl.pallas_call(
        _compress_pages_body,
        grid=(n // pp,),
        in_specs=[pl.BlockSpec((pp, 2, 256, page), lambda i: (i, 0, 0, 0)),
                  _resident(wbd.shape), _resident(bias.shape)],
        out_specs=pl.BlockSpec((CMP_ROWS, 1024), lambda i: (i, 0)),
        out_shape=jax.ShapeDtypeStruct((n * per_page, 1024), F32),
        scratch_shapes=[pltpu.VMEM((4, CMP_STRIDE * CMP_PITCH, LANES), F32)],
        compiler_params=_cparams(("arbitrary",)),
        name="compress_pages",
    )(pages_t, wbd, bias)


def _finish_compress(fs):
    n = fs.shape[0]
    kc = fs[:, 0:256] + pltpu.roll(fs[:, 256:512], n - 1, axis=0)
    vc = fs[:, 512:768] + pltpu.roll(fs[:, 768:1024], n - 1, axis=0)
    return kc.astype(BF16), vc.astype(BF16)


def _select_mask(sc_raw, jr, pos, n_real):
    sc = jnp.where(jr * SEL_BLOCK > pos, -FORCE, sc_raw)
    sc = jnp.where(jr == pos // SEL_BLOCK, FORCE, jnp.where(jr == 0, FORCE, sc))
    sc = jnp.where(jr >= n_real, -2.0 * FORCE, sc)
    cnt = jnp.zeros(sc.shape, F32)
    for i in range(n_real):
        ri = sc[i:i + 1, :]
        ge = jnp.where(ri >= sc, 1.0, 0.0)
        gt = jnp.where(ri > sc, 1.0, 0.0)
        cnt = cnt + jnp.where(jr > i, ge, gt)
    keep = jnp.where(sc > -0.5 * FORCE, 0.0, NEG)
    return jnp.where(cnt < float(min(TOP_N, n_real)), keep, NEG)


def _cmp_prompt_body(q_ref, fs_ref, ov_ref, oc_ref, qp_ref):
    i = pl.program_id(1)
    tq = q_ref.shape[0]
    nblk = fs_ref.shape[0]
    kc, vc = _finish_compress(fs_ref[...])
    t_col = i * tq + lax.broadcasted_iota(jnp.int32, (tq, nblk), 0)
    n_row = lax.broadcasted_iota(jnp.int32, (tq, nblk), 1)
    vis = n_row * CMP_STRIDE + 2 * CMP_STRIDE - 1 <= t_col
    t_row = i * tq + lax.broadcasted_iota(jnp.int32, (nblk, tq), 1)
    n_col = lax.broadcasted_iota(jnp.int32, (nblk, tq), 0)
    vis_t = n_col * CMP_STRIDE + 2 * CMP_STRIDE - 1 <= t_row
    nsel = ov_ref.shape[0]
    jr = lax.broadcasted_iota(jnp.int32, (nsel, tq), 0)
    pos = i * tq + lax.broadcasted_iota(jnp.int32, (nsel, tq), 1)
    eye = jnp.where(lax.broadcasted_iota(jnp.int32, (tq, tq), 0) == lax.broadcasted_iota(jnp.int32, (tq, tq), 1),
                    1.0, 0.0).astype(BF16)
    for g in range(NSA_KV_HEADS):
        kg = kc[:, g * HEAD_DIM:(g + 1) * HEAD_DIM]
        vg = vc[:, g * HEAD_DIM:(g + 1) * HEAD_DIM]
        ps_t = jnp.zeros((nblk, tq), F32)
        qs = []
        for r in range(NSA_GROUP):
            hd = g * NSA_GROUP + r
            qh = q_ref[:, hd * HEAD_DIM:(hd + 1) * HEAD_DIM]
            qs.append(qh)
            p = _softmax_masked(_nt(qh, kg), vis, -1)
            oc_ref[:, hd * HEAD_DIM:(hd + 1) * HEAD_DIM] = _dot(p.astype(BF16), vg).astype(oc_ref.dtype)
            ps_t = ps_t + _softmax_masked(_nt(kg, qh), vis_t, 0)
        a_t = _select_mask(_split_dot(ov_ref[...], ps_t), jr, pos, nsel)
        a_t = jnp.concatenate([a_t, jnp.zeros((HEAD_DIM - nsel, tq), F32)], axis=0).astype(BF16)
        a = _nt(eye, a_t).astype(BF16)
        for r in range(NSA_GROUP):
            qp_ref[0, g * NSA_GROUP + r] = jnp.concatenate([qs[r], a], axis=1)


def _cmp_prompt(q, fs, ov_t, b, t):
    nq = t // TQ
    nblk = t // CMP_STRIDE
    return pl.pallas_call(
        _cmp_prompt_body,
        grid=(b, nq),
        in_specs=[pl.BlockSpec((TQ, 1024), lambda bi, i: (bi * nq + i, 0)),
                  pl.BlockSpec((nblk, 1024), lambda bi, i: (bi, 0)),
                  _resident(ov_t.shape)],
        out_specs=[pl.BlockSpec((TQ, 1024), lambda bi, i: (bi * nq + i, 0)),
                   pl.BlockSpec((1, NSA_HEADS, TQ, LANES), lambda bi, i: (bi, 0, i, 0))],
        out_shape=[jax.ShapeDtypeStruct((b * t, 1024), BF16),
                   jax.ShapeDtypeStruct((b, NSA_HEADS, t, LANES), BF16)],
        compiler_params=_cparams(("arbitrary", "arbitrary")),
        name="cmp_prompt",
    )(q, fs, ov_t)


def _flash_body(qp_ref, kp_ref, v_ref, o_ref, *, window):
    i = pl.program_id(2)
    tq = qp_ref.shape[2]
    rows = NSA_GROUP * tq
    q = qp_ref[0].reshape(rows, LANES)
    t = i * tq + lax.broadcasted_iota(jnp.int32, (tq, tq), 0)
    t = jnp.concatenate([t] * NSA_GROUP, axis=0)
    n0 = lax.broadcasted_iota(jnp.int32, (rows, tq), 1)

    def chunk(c, carry):
        m, l, acc = carry
        off = pl.multiple_of(c * tq, tq)
        k = kp_ref[0, 0, pl.ds(off, tq), :]
        v = v_ref[0, 0, pl.ds(off, tq), :]
        s = _nt(q, k)
        d = t - (n0 + c * tq)
        valid = d >= 0
        if window is not None:
            valid = jnp.logical_and(valid, d <= window)
        s = jnp.where(valid, s, NEG)
        m_new = jnp.maximum(m, jnp.max(s, axis=-1, keepdims=True))
        alpha = jnp.exp(m - m_new)
        p = jnp.exp(s - m_new)
        l = alpha * l + jnp.sum(p, axis=-1, keepdims=True)
        acc = alpha * acc + _dot(p.astype(BF16), v)
        return m_new, l, acc

    lo = 0 if window is None else jnp.maximum(i - window // tq, 0)
    m0 = jnp.full((rows, 1), NEG, F32)
    l0 = jnp.zeros((rows, 1), F32)
    a0 = jnp.zeros((rows, HEAD_DIM), F32)
    _, l, acc = lax.fori_loop(lo, i + 1, chunk, (m0, l0, a0))
    o = acc / l
    for r in range(NSA_GROUP):
        o_ref[:, r * HEAD_DIM:(r + 1) * HEAD_DIM] = o[r * tq:(r + 1) * tq].astype(o_ref.dtype)


def _flash(qp, kp, v, window):
    b, _, t, _ = qp.shape
    nq = t // TQ
    return pl.pallas_call(
        functools.partial(_flash_body, window=window),
        grid=(b, NSA_KV_HEADS, nq),
        in_specs=[pl.BlockSpec((1, NSA_GROUP, TQ, LANES), lambda bi, g, i: (bi, g, i, 0)),
                  pl.BlockSpec((1, 1, t, LANES), lambda bi, g, i: (bi, g, 0, 0)),
                  pl.BlockSpec((1, 1, t, HEAD_DIM), lambda bi, g, i: (bi, g, 0, 0))],
        out_specs=pl.BlockSpec((TQ, NSA_GROUP * HEAD_DIM), lambda bi, g, i: (bi * nq + i, g)),
        out_shape=jax.ShapeDtypeStruct((b * t, 1024), BF16),
        compiler_params=_cparams(("arbitrary", "arbitrary", "arbitrary")),
        name="flash_win" if window is not None else "flash_slc",
    )(qp, kp, v)


def _diag_blocks(o_full, o_ref):
    rows = o_ref.shape[1] // NSA_KV_HEADS
    for g in range(NSA_KV_HEADS):
        o_ref[0, g * rows:(g + 1) * rows, :] = o_full[g * rows:(g + 1) * rows, g * HEAD_DIM:(g + 1) * HEAD_DIM]


def _cmp_sample_body(pt_ref, *refs, n_pages, past_len, dec):
    fs_refs = refs[:n_pages]
    qt_ref, ov_ref, msum_ref, oc_ref, at_ref = refs[n_pages:]
    fs = jnp.concatenate([r[...] for r in fs_refs], axis=0)
    nblk = fs.shape[0]
    kc, vc = _finish_compress(fs)
    qt = qt_ref[0]
    nq = qt.shape[0]
    pos_c = past_len + (lax.broadcasted_iota(jnp.int32, (nq, nblk), 0) // NSA_GROUP) % dec
    n_r = lax.broadcasted_iota(jnp.int32, (nq, nblk), 1)
    vis = n_r * CMP_STRIDE + 2 * CMP_STRIDE - 1 <= pos_c
    vis = jnp.logical_and(vis, n_r < nblk - 1)
    p = _softmax_masked(_nt(qt, kc), vis, -1)
    _diag_blocks(_dot(p.astype(BF16), vc), oc_ref)
    pos_l = past_len + (lax.broadcasted_iota(jnp.int32, (nblk, nq), 1) // NSA_GROUP) % dec
    n_c = lax.broadcasted_iota(jnp.int32, (nblk, nq), 0)
    vis_t = jnp.logical_and(n_c * CMP_STRIDE + 2 * CMP_STRIDE - 1 <= pos_l, n_c < nblk - 1)
    p_t = _softmax_masked(_nt(kc, qt), vis_t, 0)
    a1 = _split_dot(ov_ref[...], p_t)
    hi = a1.astype(BF16)
    lo = (a1 - hi.astype(F32)).astype(BF16)
    sc = _dot(hi, msum_ref[...]) + _dot(lo, msum_ref[...])
    nj = ov_ref.shape[0]
    jr = lax.broadcasted_iota(jnp.int32, (nj, nq), 0)
    pos = past_len + (lax.broadcasted_iota(jnp.int32, (nj, nq), 1) // NSA_GROUP) % dec
    n_sel = -(-(past_len + dec) // SEL_BLOCK)
    a_t = _select_mask(sc, jr, pos, n_sel)
    a_t = jnp.concatenate([a_t, jnp.zeros((LANES - nj, nq), F32)], axis=0)
    at_ref[0] = a_t.T


def _cmp_sample(page_table, fs_phys, qt, ov_t, msum, past_len, dec):
    nb, n_pages = page_table.shape
    nj = ov_t.shape[0]
    fs_specs = [pl.BlockSpec((SUBLANES, 1024), functools.partial(lambda b, pt, k: (pt[b, k], 0), k=k))
                for k in range(n_pages)]
    grid_spec = pltpu.PrefetchScalarGridSpec(
        num_scalar_prefetch=1,
        grid=(nb,),
        in_specs=fs_specs + [pl.BlockSpec((1, LANES, 256), lambda b, pt: (b, 0, 0)),
                             pl.BlockSpec(ov_t.shape, lambda b, pt: (0, 0)),
                             pl.BlockSpec(msum.shape, lambda b, pt: (0, 0))],
        out_specs=[pl.BlockSpec((1, 64, HEAD_DIM), lambda b, pt: (b, 0, 0)),
                   pl.BlockSpec((1, LANES, LANES), lambda b, pt: (b, 0, 0))],
    )
    return pl.pallas_call(
        functools.partial(_cmp_sample_body, n_pages=n_pages, past_len=past_len, dec=dec),
        grid_spec=grid_spec,
        out_shape=[jax.ShapeDtypeStruct((nb, 64, HEAD_DIM), F32),
                   jax.ShapeDtypeStruct((nb, LANES, LANES), F32)],
        compiler_params=_cparams(("arbitrary",)),
        name="cmp_sample",
    )(page_table, *([fs_phys] * n_pages), qt, ov_t, msum)


def _attn_sample_body(*refs, n_blocks, has_pt, has_bm, has_pm):
    refs = list(refs[1:] if has_pt else refs)
    kv_refs = refs[:n_blocks]
    rest = refs[n_blocks:]
    new_ref, qt_ref = rest[0], rest[1]
    rest = rest[2:]
    bm_ref = rest.pop(0) if has_bm else None
    pm_ref = rest.pop(0) if has_pm else None
    nm_ref, o_ref, st_ref = rest
    qt = qt_ref[0]
    nq = qt.shape[0]
    rows = kv_refs[0].shape[3]
    per = rows // SEL_BLOCK
    if has_bm:
        blk_of_lane = lax.broadcasted_iota(jnp.int32, (nq, rows), 1) // SEL_BLOCK
    m = jnp.full((nq, 1), NEG, F32)
    for kb in range(n_blocks):
        s = _dot(qt, kv_refs[kb][0, 0].astype(BF16))
        if has_pm:
            s = s + pm_ref[:, kb * rows:(kb + 1) * rows]
        if has_bm:
            add = bm_ref[0, :, kb * per:kb * per + 1]
            for u in range(1, per):
                add = jnp.where(blk_of_lane == u, bm_ref[0, :, kb * per + u:kb * per + u + 1], add)
            s = s + add
        st_ref[:, kb * rows:(kb + 1) * rows] = s
        m = jnp.maximum(m, jnp.max(s, axis=-1, keepdims=True))
    s_new = _nt(qt, new_ref[0, :, 0:256].astype(BF16)) + nm_ref[...]
    if has_bm:
        s_new = s_new + bm_ref[0, :, n_blocks * per:n_blocks * per + 1]
    m = jnp.maximum(m, jnp.max(s_new, axis=-1, keepdims=True))
    e_new = jnp.exp(s_new - m)
    l = jnp.sum(e_new, axis=-1, keepdims=True)
    acc = _dot(e_new.astype(BF16), new_ref[0, :, 256:512].astype(BF16))
    for kb in range(n_blocks):
        e = jnp.exp(st_ref[:, kb * rows:(kb + 1) * rows] - m)
        l = l + jnp.sum(e, axis=-1, keepdims=True)
        acc = acc + _nt(e.astype(BF16), kv_refs[kb][0, 1].astype(BF16))
    _diag_blocks(acc / jnp.maximum(l, 1e-30), o_ref)


def _attn_sample(kv, slot_blk, n_blocks, new, qt, new_mask, page_table=None, blk_mask=None, past_mask=None):
    nb = qt.shape[0]
    kv_rows = kv.shape[3]
    has_pt = page_table is not None
    has_bm = blk_mask is not None
    has_pm = past_mask is not None
    if has_pt:
        kv_specs = [pl.BlockSpec((1, 2, 256, kv_rows),
                                 functools.partial(lambda b, pt, k: (pt[b, k], slot_blk, 0, 0), k=k))
                    for k in range(n_blocks)]
        im = lambda b, pt: (b, 0, 0)
        cm = lambda b, pt: (0, 0)
    else:
        kv_specs = [pl.BlockSpec((1, 2, 256, kv_rows), lambda b: (b, slot_blk, 0, 0))]
        im = lambda b: (b, 0, 0)
        cm = lambda b: (0, 0)
    in_specs = kv_specs + [pl.BlockSpec((1, SUBLANES, 512), im), pl.BlockSpec((1, LANES, 256), im)]
    args = [kv] * n_blocks + [new, qt]
    if has_bm:
        in_specs.append(pl.BlockSpec((1,) + blk_mask.shape[1:], im))
        args.append(blk_mask)
    if has_pm:
        in_specs.append(pl.BlockSpec(past_mask.shape, cm))
        args.append(past_mask)
    in_specs.append(pl.BlockSpec(new_mask.shape, cm))
    args.append(new_mask)
    out_spec = pl.BlockSpec((1, 64, HEAD_DIM), im)
    scratch = [pltpu.VMEM((LANES, n_blocks * kv_rows), F32)]
    body = functools.partial(_attn_sample_body, n_blocks=n_blocks, has_pt=has_pt, has_bm=has_bm, has_pm=has_pm)
    out_shape = jax.ShapeDtypeStruct((nb, 64, HEAD_DIM), F32)
    if has_pt:
        gs = pltpu.PrefetchScalarGridSpec(num_scalar_prefetch=1, grid=(nb,), in_specs=in_specs,
                                          out_specs=out_spec, scratch_shapes=scratch)
        return pl.pallas_call(body, grid_spec=gs, out_shape=out_shape, compiler_params=_cparams(("arbitrary",)),
                              name="slc_sample")(page_table, *args)
    return pl.pallas_call(body, grid=(nb,), in_specs=in_specs, out_specs=out_spec, scratch_shapes=scratch,
                          out_shape=out_shape, compiler_params=_cparams(("arbitrary",)), name="win_sample")(*args)


def _out_proj_body(*refs, n_in, gated):
    o_refs = refs[:n_in]
    rest = refs[n_in:]
    if gated:
        gate_ref, rest = rest[0], rest[1:]
    x_ref, w_ref, g_ref, y_ref = rest
    tm = x_ref.shape[0]
    if gated:
        lane = lax.broadcasted_iota(jnp.int32, (tm, LANES), 1)
        gates = gate_ref[...]
        chunks = []
        for c in range(1024 // LANES):
            acc = jnp.zeros((tm, LANES), F32)
            for kbr in range(n_in):
                c0 = (2 * c) * 3 + kbr
                c1 = (2 * c + 1) * 3 + kbr
                gexp = jnp.where(lane < HEAD_DIM, gates[:, c0:c0 + 1], gates[:, c1:c1 + 1])
                acc = acc + gexp * o_refs[kbr][:, c * LANES:(c + 1) * LANES].astype(F32)
            chunks.append(acc.astype(BF16))
        o = jnp.concatenate(chunks, axis=1)
    else:
        o = o_refs[0][...].astype(BF16)
    y = _dot(o, w_ref[...])
    y_ref[...] = x_ref[...] + _rms(y, g_ref[...])


def _out_proj(os_, gates, x, w, g):
    m, d = x.shape
    row = lambda i: (i, 0)
    gated = gates is not None
    in_specs = [pl.BlockSpec((TM, 1024), row) for _ in os_]
    args = list(os_)
    if gated:
        in_specs.append(pl.BlockSpec((TM, LANES), row))
        args.append(gates)
    in_specs += [pl.BlockSpec((TM, d), row), _resident(w.shape), _resident(g.shape)]
    args += [x, w, g]
    return pl.pallas_call(
        functools.partial(_out_proj_body, n_in=len(os_), gated=gated),
        grid=(m // TM,), in_specs=in_specs, out_specs=pl.BlockSpec((TM, d), row),
        out_shape=jax.ShapeDtypeStruct((m, d), F32),
        compiler_params=_cparams(("arbitrary",)), name="out_proj",
    )(*args)


def _ffn_body(*refs, sample, tiles_per_seq, nb):
    if sample:
        x_ref, g2_ref, g3_ref, wu_ref, cw_ref, cb_ref, wd_ref, buf_ref, y_ref, tail_ref = refs
    else:
        x_ref, g2_ref, g3_ref, wu_ref, cw_ref, cb_ref, wd_ref, y_ref, tail_ref, carry_ref = refs
    x = x_ref[...]
    tm = x.shape[0]
    f = wd_ref.shape[0]
    h = _rms(x, g2_ref[...]).astype(BF16)
    acc = jnp.zeros(x.shape, F32)
    if not sample:
        first = pl.program_id(0) % tiles_per_seq == 0
        row = lax.broadcasted_iota(jnp.int32, (tm, FFN_FCHUNK), 0)
    for fc in range(f // FFN_FCHUNK):
        sl = slice(fc * FFN_FCHUNK, (fc + 1) * FFN_FCHUNK)
        gate = _dot(h, wu_ref[:, sl])
        val = _dot(h, wu_ref[:, f + fc * FFN_FCHUNK:f + (fc + 1) * FFN_FCHUNK])
        if sample:
            prev1 = jnp.concatenate([buf_ref[nb:2 * nb, sl], gate[0:tm - nb]], axis=0)
            prev2 = jnp.concatenate([buf_ref[0:2 * nb, sl], gate[0:tm - 2 * nb]], axis=0)
            tail_ref[:, sl] = gate[tm - 2 * nb:tm]
        else:
            carry = carry_ref[:, sl]
            c6 = jnp.where(first, 0.0, carry[6:7, :])
            c7 = jnp.where(first, 0.0, carry[7:8, :])
            prev1 = jnp.where(row == 0, c7, pltpu.roll(gate, 1, axis=0))
            prev2 = jnp.where(row == 0, c6, jnp.where(row == 1, c7, pltpu.roll(gate, 2, axis=0)))
            carry_ref[:, sl] = gate[tm - SUBLANES:tm]
            tail_ref[:, sl] = gate[tm - SUBLANES:tm]
        conv = cb_ref[0:1, sl] + cw_ref[0:1, sl] * prev2 + cw_ref[1:2, sl] * prev1 + cw_ref[2:3, sl] * gate
        act = (jax.nn.gelu(conv) * val).astype(BF16)
        acc = acc + _dot(act, wd_ref[sl, :])
    y_ref[...] = x + _rms(acc, g3_ref[...])


def _ffn(x, g2, g3, wu, cw, cb, wd, seq_len=None, buf=None):
    m, d = x.shape
    f = wd.shape[0]
    sample = buf is not None
    row = lambda i: (i, 0)
    if sample:
        tm = m
        nb = buf.shape[0] // 2
        tail_shape, tail_block = (2 * nb, f), (2 * nb, f)
        scratch = []
    else:
        tm = TM
        nb = 0
        tail_shape, tail_block = (m // tm * SUBLANES, f), (SUBLANES, f)
        scratch = [pltpu.VMEM((SUBLANES, f), F32)]
    in_specs = [pl.BlockSpec((tm, d), row), _resident(g2.shape), _resident(g3.shape), _resident(wu.shape),
                _resident(cw.shape), _resident(cb.shape), _resident(wd.shape)]
    args = [x, g2, g3, wu, cw, cb, wd]
    if sample:
        in_specs.append(_resident(buf.shape))
        args.append(buf)
    return pl.pallas_call(
        functools.partial(_ffn_body, sample=sample, tiles_per_seq=(seq_len // tm if not sample else 1), nb=nb),
        grid=(m // tm,), in_specs=in_specs,
        out_specs=[pl.BlockSpec((tm, d), row), pl.BlockSpec(tail_block, row)],
        out_shape=[jax.ShapeDtypeStruct((m, d), F32), jax.ShapeDtypeStruct(tail_shape, F32)],
        scratch_shapes=scratch,
        compiler_params=_cparams(("arbitrary",)), name="ffn_sample" if sample else "ffn_prompt",
    )(*args)


def _gla_proj_body(x_ref, g_ref, w_ref, wg_ref, bg_ref, q_ref, k_ref, v_ref, sr_ref, la_ref):
    h = _rms(x_ref[...], g_ref[...]).astype(BF16)
    nk = GLA_HEADS * GLA_DK
    nv = GLA_HEADS * GLA_DV
    for c in range(nk // 256):
        q_ref[:, c * 256:(c + 1) * 256] = _dot(h, w_ref[:, c * 256:(c + 1) * 256]) * (GLA_DK ** -0.5)
        k_ref[:, c * 256:(c + 1) * 256] = _dot(h, w_ref[:, nk + c * 256:nk + (c + 1) * 256])
    for c in range(nv // 256):
        v_ref[:, c * 256:(c + 1) * 256] = _dot(h, w_ref[:, 2 * nk + c * 256:2 * nk + (c + 1) * 256]).astype(BF16)
        r = _dot(h, w_ref[:, 2 * nk + nv + c * 256:2 * nk + nv + (c + 1) * 256])
        sr_ref[:, c * 256:(c + 1) * 256] = jax.nn.silu(r).astype(BF16)
    low = _dot(h, w_ref[:, 2 * nk + 2 * nv:2 * nk + 2 * nv + LANES]).astype(BF16)
    gz = _dot(low, wg_ref[...]) + bg_ref[...]
    log_sig = jnp.minimum(gz, 0.0) - jnp.log1p(jnp.exp(-jnp.abs(gz)))
    la_ref[...] = log_sig / GLA_GATE_TEMP


def _gla_proj(x, g, w, wg, bg):
    m, d = x.shape
    nk = GLA_HEADS * GLA_DK
    nv = GLA_HEADS * GLA_DV
    row = lambda i: (i, 0)
    return pl.pallas_call(
        _gla_proj_body, grid=(m // TM,),
        in_specs=[pl.BlockSpec((TM, d), row), _resident(g.shape), _resident(w.shape),
                  _resident(wg.shape), _resident(bg.shape)],
        out_specs=[pl.BlockSpec((TM, nk), row), pl.BlockSpec((TM, nk), row), pl.BlockSpec((TM, nv), row),
                   pl.BlockSpec((TM, nv), row), pl.BlockSpec((TM, nk), row)],
        out_shape=[jax.ShapeDtypeStruct((m, nk), F32), jax.ShapeDtypeStruct((m, nk), F32),
                   jax.ShapeDtypeStruct((m, nv), BF16), jax.ShapeDtypeStruct((m, nv), BF16),
                   jax.ShapeDtypeStruct((m, nk), F32)],
        compiler_params=_cparams(("arbitrary",)), name="gla_proj",
    )(x, g, w, wg, bg)


def _cumsum_rows(x):
    n = x.shape[0]
    row = lax.broadcasted_iota(jnp.int32, x.shape, 0)
    sh = 1
    while sh < n:
        x = x + jnp.where(row >= sh, pltpu.roll(x, sh, axis=0), 0.0)
        sh *= 2
    return x


def _gla_rec_body(*refs, has_s0):
    if has_s0:
        q_ref, k_ref, la_ref, v_ref, sr_ref, gn_ref, s0_ref, o_ref, so_ref, st_ref = refs
    else:
        q_ref, k_ref, la_ref, v_ref, sr_ref, gn_ref, o_ref, so_ref, st_ref = refs
    c = pl.program_id(1)
    cs = q_ref.shape[0]

    @pl.when(c == 0)
    def _():
        st_ref[...] = s0_ref[0] if has_s0 else jnp.zeros(st_ref.shape, F32)

    row = lax.broadcasted_iota(jnp.int32, (cs, 1), 0)
    trow = lax.broadcasted_iota(jnp.int32, (cs, cs), 0)
    scol = lax.broadcasted_iota(jnp.int32, (cs, cs), 1)
    r8 = lax.broadcasted_iota(jnp.int32, (SUBLANES, 1), 0)
    lane8 = lax.broadcasted_iota(jnp.int32, (SUBLANES, cs), 1)
    for hd in range(GLA_HEADS):
        ksl = slice(hd * GLA_DK, (hd + 1) * GLA_DK)
        vsl = slice(hd * GLA_DV, (hd + 1) * GLA_DV)
        q, k, v = q_ref[:, ksl], k_ref[:, ksl], v_ref[:, vsl]
        cum = _cumsum_rows(la_ref[:, ksl])
        last = cum[cs - 1:cs, :]
        s_old = st_ref[hd]
        out = _dot((q * jnp.exp(cum)).astype(BF16), s_old.astype(BF16))
        att = None
        hh = cs // 2
        while hh >= SUBLANES:
            nblk = cs // (2 * hh)
            ref = jnp.concatenate(
                [jnp.broadcast_to(cum[u * 2 * hh + hh - 1:u * 2 * hh + hh, :], (2 * hh, GLA_DK)) for u in range(nblk)],
                axis=0) if nblk > 1 else jnp.broadcast_to(cum[hh - 1:hh, :], (cs, GLA_DK))
            second = (row % (2 * hh)) >= hh
            qh = jnp.where(second, q * jnp.exp(jnp.minimum(cum - ref, 0.0)), 0.0).astype(BF16)
            kh = jnp.where(second, 0.0, k * jnp.exp(jnp.minimum(ref - cum, 0.0))).astype(BF16)
            a = _nt(qh, kh)
            if nblk > 1:
                a = jnp.where(trow // (2 * hh) == scol // (2 * hh), a, 0.0)
            att = a if att is None else att + a
            hh //= 2
        slabs = []
        for g8 in range(cs // SUBLANES):
            r0 = g8 * SUBLANES
            cg, qg, kg = cum[r0:r0 + SUBLANES], q[r0:r0 + SUBLANES], k[r0:r0 + SUBLANES]
            slab = jnp.zeros((SUBLANES, GLA_DV if att is None else cs), F32)
            for s in range(SUBLANES):
                e = jnp.exp(jnp.minimum(cg - cg[s:s + 1, :], 0.0))
                col = jnp.sum(e * qg * kg[s:s + 1, :], axis=1, keepdims=True)
                col = jnp.where(r8 >= s, col, 0.0)
                if att is None:
                    slab = slab + col * v[r0 + s:r0 + s + 1, :].astype(F32)
                else:
                    slab = jnp.where(lane8 == r0 + s, col, slab)
            slabs.append(slab)
        if att is None:
            out = out + slabs[0]
        else:
            att = att + jnp.concatenate(slabs, axis=0)
            out = out + _dot(att.astype(BF16), v)
        kt = (k * jnp.exp(last - cum)).astype(BF16)
        dcol = jnp.broadcast_to(jnp.exp(last), (SUBLANES, GLA_DK)).T[:, 0:1]
        st_ref[hd] = dcol * s_old + _tn(kt, v)
        on = _rms(out, gn_ref[...])
        o_ref[:, vsl] = (on * sr_ref[:, vsl].astype(F32)).astype(o_ref.dtype)

    @pl.when(c == pl.num_programs(1) - 1)
    def _():
        so_ref[0] = st_ref[...]


def _gla_rec(q, k, la, v, sr, gn, nb, chunk, s0=None):
    m = q.shape[0]
    nc = m // nb // chunk
    nk = GLA_HEADS * GLA_DK
    nv = GLA_HEADS * GLA_DV
    row = lambda b, c: (b * nc + c, 0)
    st_spec = pl.BlockSpec((1, GLA_HEADS, GLA_DK, GLA_DV), lambda b, c: (b, 0, 0, 0))
    in_specs = [pl.BlockSpec((chunk, nk), row), pl.BlockSpec((chunk, nk), row), pl.BlockSpec((chunk, nk), row),
                pl.BlockSpec((chunk, nv), row), pl.BlockSpec((chunk, nv), row),
                pl.BlockSpec(gn.shape, lambda b, c: (0, 0))]
    args = [q, k, la, v, sr, gn]
    if s0 is not None:
        in_specs.append(st_spec)
        args.append(s0)
    return pl.pallas_call(
        functools.partial(_gla_rec_body, has_s0=s0 is not None),
        grid=(nb, nc), in_specs=in_specs,
        out_specs=[pl.BlockSpec((chunk, nv), row), st_spec],
        out_shape=[jax.ShapeDtypeStruct((m, nv), BF16),
                   jax.ShapeDtypeStruct((nb, GLA_HEADS, GLA_DK, GLA_DV), F32)],
        scratch_shapes=[pltpu.VMEM((GLA_HEADS, GLA_DK, GLA_DV), F32)],
        compiler_params=_cparams(("arbitrary", "arbitrary")),
        name="gla_rec_sample" if s0 is not None else "gla_rec_prompt",
    )(*args)


def _rope_tables(pos):
    half = HEAD_DIM // 2
    inv = ROPE_THETA ** (-jnp.arange(half, dtype=F32) / half)
    ang = pos.astype(F32)[:, None] * inv[None, :]
    cos, sin = jnp.cos(ang), jnp.sin(ang)
    z = jnp.zeros_like(sin)
    cos_t = jnp.tile(cos, (1, 4))
    sa = jnp.tile(jnp.concatenate([-sin, z], axis=1), (1, 2))
    sb = jnp.tile(jnp.concatenate([z, sin], axis=1), (1, 2))
    return cos_t, sa, sb


def _pad_cols(w, n):
    return jnp.pad(w, ((0, 0), (0, n - w.shape[1])))


def _cmp_weights(cmp_pe, cmp_w):
    g = NSA_KV_HEADS
    w = cmp_w.reshape(2, 2, CMP_STRIDE, HEAD_DIM, HEAD_DIM)
    eye = jnp.eye(g, dtype=F32)
    wbd = jnp.einsum('shjde,gk->sjgdhke', w, eye)
    wbd = wbd.reshape(2, CMP_STRIDE, g * HEAD_DIM, 2 * g * HEAD_DIM).astype(BF16)
    pe = cmp_pe.reshape(2, 2, CMP_STRIDE, 1, 1, HEAD_DIM)
    pe_t = jnp.broadcast_to(pe, (2, 2, CMP_STRIDE, SUBLANES, g, HEAD_DIM)).reshape(2, 2, CMP_STRIDE, SUBLANES, g * HEAD_DIM)
    return wbd, pe_t


def _overlap_t(n_blk, n_sel, rows, cols):
    i = jnp.arange(cols)[None, :]
    j = jnp.arange(rows)[:, None]
    ov = (i * CMP_STRIDE + 2 * CMP_STRIDE > j * SEL_BLOCK) & (i * CMP_STRIDE < (j + 1) * SEL_BLOCK)
    ov = ov & (i < n_blk) & (j < n_sel)
    return ov.astype(BF16)


def _nsa_layer_prompt(x, g, w_in, wbd, cbias, w_o, tabs, b, t):
    q, rows, win, gates = _nsa_proj(x, g[0:1], w_in, *tabs)
    fs = _compress(rows.reshape(b * t // CMP_STRIDE, CMP_STRIDE, 1024), wbd, cbias)
    n_blk = t // CMP_STRIDE - 1
    n_sel = -(-t // SEL_BLOCK)
    oc, qp = _cmp_prompt(q, fs, _overlap_t(n_blk, n_sel, n_sel, t // CMP_STRIDE), b, t)

    def heads(a):
        return a.reshape(b, t, NSA_KV_HEADS, HEAD_DIM).transpose(0, 2, 1, 3).astype(BF16)

    onehot = (jnp.arange(t)[:, None] // SEL_BLOCK == jnp.arange(HEAD_DIM)[None, :]).astype(BF16)
    onehot = jnp.where(jnp.arange(HEAD_DIM)[None, :] < n_sel, onehot, 0)
    k_slc = jnp.concatenate([heads(rows[:, 512:768]),
                             jnp.broadcast_to(onehot, (b, NSA_KV_HEADS, t, HEAD_DIM))], axis=-1)
    k_win = jnp.concatenate([heads(win[:, 0:256]), jnp.zeros((b, NSA_KV_HEADS, t, HEAD_DIM), BF16)], axis=-1)
    o_s = _flash(qp, k_slc, heads(rows[:, 768:1024]), None)
    o_w = _flash(qp, k_win, heads(win[:, 256:512]), WINDOW)
    x = _out_proj([oc, o_s, o_w], gates, x, w_o, g[1:2])
    return x, rows, win


def _to_rows(o, nb, dec):
    o = o.reshape(nb, NSA_KV_HEADS, dec, NSA_GROUP, HEAD_DIM).transpose(2, 0, 1, 3, 4)
    return o.reshape(dec * nb, NSA_HEADS * HEAD_DIM).astype(BF16)


def _nsa_layer_sample(x, g, w_in, wbd, cbias, w_o, tabs, cache, page_table, win_state, nb, dec):
    n_phys, page = cache.shape[0], cache.shape[1]
    n_pages = page_table.shape[1]
    past_len = n_pages * page
    q, rows, win, gates = _nsa_proj(x, g[0:1], w_in, *tabs)
    pages_t = cache.transpose(0, 2, 3, 4, 1).reshape(n_phys, 4, NSA_KV_HEADS * HEAD_DIM, page)
    wb = win_state.shape[1]
    win_t = win_state.transpose(0, 2, 3, 4, 1).reshape(nb, 2, NSA_KV_HEADS * HEAD_DIM, wb)
    fs_phys = _compress_pages(pages_t, wbd, cbias)
    q5 = q.reshape(dec, nb, NSA_KV_HEADS, NSA_GROUP, HEAD_DIM).transpose(1, 2, 0, 3, 4)
    eye = jnp.eye(NSA_KV_HEADS, dtype=BF16)
    qt = q5.reshape(nb, NSA_KV_HEADS, dec * NSA_GROUP, 1, HEAD_DIM) * eye[None, :, None, :, None]
    qt = qt.reshape(nb, NSA_KV_HEADS * dec * NSA_GROUP, NSA_KV_HEADS * HEAD_DIM)
    qt = jnp.pad(qt, ((0, 0), (0, LANES - qt.shape[1]), (0, 0)))
    n_blk = (past_len + dec) // CMP_STRIDE - 1
    n_sel = -(-(past_len + dec) // SEL_BLOCK)
    nj = -(-n_sel // SUBLANES) * SUBLANES
    lane = jnp.arange(LANES)
    msum = (lane[:, None] // NSA_GROUP == lane[None, :] // NSA_GROUP).astype(BF16)
    oc, a_t = _cmp_sample(page_table, fs_phys, qt, _overlap_t(n_blk, n_sel, nj, past_len // CMP_STRIDE), msum, past_len, dec)
    t_row = ((lane // NSA_GROUP) % dec)[:, None]
    j8 = jnp.arange(SUBLANES)[None, :]
    new_mask = jnp.where((j8 < dec) & (j8 <= t_row), 0.0, NEG).astype(F32)
    win_mask = jnp.where(jnp.arange(wb)[None, :] >= t_row + wb - WINDOW, 0.0, NEG).astype(F32)

    def new_rows(a):
        a = a.reshape(dec, nb, 512).transpose(1, 0, 2)
        return jnp.pad(a, ((0, 0), (0, SUBLANES - dec), (0, 0)))

    o_s = _attn_sample(pages_t, 1, n_pages, new_rows(rows[:, 512:1024]), qt, new_mask,
                       page_table=page_table, blk_mask=a_t)
    o_w = _attn_sample(win_t, 0, 1, new_rows(win), qt, new_mask, past_mask=win_mask)
    x = _out_proj([_to_rows(oc, nb, dec), _to_rows(o_s, nb, dec), _to_rows(o_w, nb, dec)], gates, x, w_o, g[1:2])
    return x, rows, win


def _gla_layer(x, g, w_in, wg, bg, gn, w_o, nb, seq, s0=None):
    q, k, v, sr, la = _gla_proj(x, g[0:1], w_in, wg, bg)
    if s0 is None:
        o, s = _gla_rec(q, k, la, v, sr, gn, nb, GLA_CHUNK)
    else:
        def seqs(a):
            a = a.reshape(seq, nb, a.shape[1]).transpose(1, 0, 2)
            return jnp.pad(a, ((0, 0), (0, SUBLANES - seq), (0, 0))).reshape(nb * SUBLANES, a.shape[2])
        o, s = _gla_rec(seqs(q), seqs(k), seqs(la), seqs(v), seqs(sr), gn, nb, SUBLANES, s0=s0)
        o = o.reshape(nb, SUBLANES, o.shape[1])[:, :seq].transpose(1, 0, 2).reshape(seq * nb, o.shape[1])
    x = _out_proj([o], None, x, w_o, g[1:2])
    return x, s


def kernel(x_prompt, x_sample, cache_nsa_kv, state_win_kv, state_gla, state_ffn_conv, page_table, norm_gain, nsa_w_in, nsa_cmp_pe, nsa_cmp_w, nsa_w_o, gla_w_in, gla_w_gate_up, gla_b_gate, gla_norm_gain, gla_w_o, ffn_w_up, ffn_conv_w, ffn_conv_b, ffn_w_down):
    b, t, d = x_prompt.shape
    nb, dec, _ = x_sample.shape
    depth = norm_gain.shape[0]
    f = ffn_w_down.shape[1]
    past_len = page_table.shape[1] * cache_nsa_kv.shape[2]

    xp = x_prompt.reshape(b * t, d)
    xs = x_sample.transpose(1, 0, 2).reshape(dec * nb, d)
    tabs_p = _rope_tables(jnp.arange(t, dtype=jnp.int32))
    tabs_s = _rope_tables(past_len + jnp.repeat(jnp.arange(dec, dtype=jnp.int32), nb))

    nsa_p, nsa_s, win_p, win_s, gla_p, gla_s, ffn_p, ffn_s = [], [], [], [], [], [], [], []
    for i in range(depth):
        g = norm_gain[i]
        a = i // 2
        if i % 2 == 0:
            w_in = _pad_cols(nsa_w_in[a], 2688).astype(BF16)
            w_o = nsa_w_o[a].astype(BF16)
            wbd, pe_t = _cmp_weights(nsa_cmp_pe[a], nsa_cmp_w[a])
            cbias = _cmp_bias(pe_t, wbd)
            xp, rows, win = _nsa_layer_prompt(xp, g, w_in, wbd, cbias, w_o, tabs_p, b, t)
            nsa_p.append(rows.reshape(b, t, 4, NSA_KV_HEADS, HEAD_DIM))
            nw = min(WINDOW, t)
            win_p.append(win.reshape(b, t, 2, NSA_KV_HEADS, HEAD_DIM)[:, t - nw:])
            xs, rows, win = _nsa_layer_sample(xs, g, w_in, wbd, cbias, w_o, tabs_s, cache_nsa_kv[a], page_table,
                                              state_win_kv[a], nb, dec)
            nsa_s.append(rows.reshape(dec, nb, 4, NSA_KV_HEADS, HEAD_DIM).transpose(1, 0, 2, 3, 4))
            new_w = win.reshape(dec, nb, 2, NSA_KV_HEADS, HEAD_DIM).transpose(1, 0, 2, 3, 4)
            wb = state_win_kv.shape[2]
            win_s.append(jnp.concatenate([state_win_kv[a], new_w], axis=1)[:, -wb:])
        else:
            w_in = _pad_cols(gla_w_in[a], 3200).astype(BF16)
            wg = jnp.pad(gla_w_gate_up[a], ((0, LANES - GLA_GATE_RANK), (0, 0))).astype(BF16)
            bg = gla_b_gate[a][None, :]
            gn = gla_norm_gain[a][None, :]
            w_o = gla_w_o[a].astype(BF16)
            xp, s = _gla_layer(xp, g, w_in, wg, bg, gn, w_o, b, t)
            gla_p.append(s)
            xs, s = _gla_layer(xs, g, w_in, wg, bg, gn, w_o, nb, dec, s0=state_gla[a])
            gla_s.append(s)
        wu = ffn_w_up[i].astype(BF16)
        wd = ffn_w_down[i].astype(BF16)
        cw = jnp.pad(ffn_conv_w[i], ((0, SUBLANES - ffn_conv_w.shape[1]), (0, 0)))
        cb = ffn_conv_b[i][None, :]
        xp, tail = _ffn(xp, g[2:3], g[3:4], wu, cw, cb, wd, seq_len=t)
        tail = tail.reshape(b, t // TM, SUBLANES, f)[:, -1, SUBLANES - 2:]
        ffn_p.append(tail)
        buf = state_ffn_conv[i].transpose(1, 0, 2).reshape(2 * nb, f)
        xs, tail = _ffn(xs, g[2:3], g[3:4], wu, cw, cb, wd, buf=buf)
        ffn_s.append(tail.reshape(2, nb, f).transpose(1, 0, 2))

    y_prompt = xp.reshape(b, t, d)
    y_sample = xs.reshape(dec, nb, d).transpose(1, 0, 2)
    return (y_prompt, y_sample, jnp.stack(nsa_p), jnp.stack(nsa_s), jnp.stack(win_p), jnp.stack(win_s),
            jnp.stack(gla_p), jnp.stack(gla_s), jnp.stack(ffn_p), jnp.stack(ffn_s))
```

```python
import functools
import math

import jax
import jax.numpy as jnp
from jax import lax
from jax.experimental import pallas as pl
from jax.experimental.pallas import tpu as pltpu

F32, BF16 = jnp.float32, jnp.bfloat16

HEAD_DIM = 64
NSA_HEADS = 16
NSA_KV_HEADS = 4
NSA_GROUP = 4
CMP_STRIDE = 16
SEL_BLOCK = 64
TOP_N = 16
WINDOW = 512
ROPE_THETA = 10000.0
GLA_HEADS = 4
GLA_DK = 128
GLA_DV = 256
GLA_GATE_RANK = 16
GLA_GATE_TEMP = 16.0
NORM_EPS = 1e-6
FORCE = 1e6
NEG = -1e30

LANES = 128
SUBLANES = 8
VMEM_LIMIT = 48 * 1024 * 1024

TM = 256
TQ = 256
CMP_ROWS = 256
GLA_CHUNK = 128
FFN_FCHUNK = 1408
FFN_TM = 512


def _cparams(sem):
    return pltpu.CompilerParams(dimension_semantics=sem, vmem_limit_bytes=VMEM_LIMIT)


def _resident(shape):
    nd = len(shape)
    return pl.BlockSpec(shape, lambda *_: (0,) * nd, pipeline_mode=pl.Buffered(1))


def _rms(x, g):
    return x * lax.rsqrt(jnp.mean(x * x, axis=-1, keepdims=True) + NORM_EPS) * g


def _nt(a, b):
    return lax.dot_general(a, b, (((1,), (1,)), ((), ())), preferred_element_type=F32)


def _tn(a, b):
    return lax.dot_general(a, b, (((0,), (0,)), ((), ())), preferred_element_type=F32)


def _dot(a, b):
    return jnp.dot(a, b, preferred_element_type=F32)


def _split_dot(w, x):
    hi = x.astype(BF16)
    lo = (x - hi.astype(F32)).astype(BF16)
    return _dot(w, hi) + _dot(w, lo)


def _softmax_masked(s, mask, axis):
    s = jnp.where(mask, s, NEG)
    m = jnp.max(s, axis=axis, keepdims=True)
    e = jnp.where(mask, jnp.exp2(s - m), 0.0)
    return e / jnp.maximum(jnp.sum(e, axis=axis, keepdims=True), 1e-30)


Q_SCALE = HEAD_DIM ** -0.5 * math.log2(math.e)


def _nsa_proj_parts(x_ref, g_ref, w_ref, cos_ref, sa_ref, sb_ref, q_ref, gate_ref):
    h = _rms(x_ref[...], g_ref[...]).astype(BF16)
    cos, sa, sb = cos_ref[...], sa_ref[...], sb_ref[...]

    def rope(z):
        return z * cos + pltpu.roll(z, 96, axis=1) * sa + pltpu.roll(z, 32, axis=1) * sb

    def proj(c0):
        return _dot(h, w_ref[:, c0:c0 + 256])

    def rope256(z):
        return jnp.concatenate([rope(z[:, :LANES]), rope(z[:, LANES:])], axis=1)

    for c in range(4):
        q_ref[:, c * 256:(c + 1) * 256] = (rope256(proj(c * 256)) * Q_SCALE).astype(BF16)
    gate_ref[...] = jax.nn.sigmoid(_dot(h, w_ref[:, 2560:2688]))
    return (rope256(proj(1024)), proj(1280), rope256(proj(1536)), proj(1792), rope256(proj(2048)), proj(2304))


def _nsa_proj_body(x_ref, g_ref, w_ref, cos_ref, sa_ref, sb_ref, q_ref, rows_ref, win_ref, gate_ref):
    parts = _nsa_proj_parts(x_ref, g_ref, w_ref, cos_ref, sa_ref, sb_ref, q_ref, gate_ref)
    for c in range(4):
        rows_ref[:, c * 256:(c + 1) * 256] = parts[c]
    win_ref[:, 0:256] = parts[4]
    win_ref[:, 256:512] = parts[5]


def _nsa_proj_t_body(x_ref, g_ref, w_ref, cos_ref, sa_ref, sb_ref, q_ref, gate_ref,
                     rows_t_ref, win_t_ref, ks_ref, vs_ref, kw_ref, vw_ref, *, tiles_per_seq, n_sel):
    kc, vc, ks, vs, kw, vw = _nsa_proj_parts(x_ref, g_ref, w_ref, cos_ref, sa_ref, sb_ref, q_ref, gate_ref)
    tm = kc.shape[0]
    for c, part in enumerate((kc, vc, ks, vs)):
        rows_t_ref[0, c * 256:(c + 1) * 256, :] = part.T
    win_t_ref[0, 0:256, :] = kw.T
    win_t_ref[0, 256:512, :] = vw.T
    vs_ref[0] = vs.T.reshape(NSA_KV_HEADS, HEAD_DIM, tm).astype(BF16)
    vw_ref[0] = vw.T.reshape(NSA_KV_HEADS, HEAD_DIM, tm).astype(BF16)
    t = (pl.program_id(0) % tiles_per_seq) * tm + lax.broadcasted_iota(jnp.int32, (tm, HEAD_DIM), 0)
    lane = lax.broadcasted_iota(jnp.int32, (tm, HEAD_DIM), 1)
    onehot = jnp.where(jnp.where(lane < n_sel, t // SEL_BLOCK, -1) == lane, 1.0, 0.0).astype(BF16)
    zeros = jnp.zeros((tm, HEAD_DIM), BF16)
    for g in range(NSA_KV_HEADS):
        sl = slice(g * HEAD_DIM, (g + 1) * HEAD_DIM)
        ks_ref[0, g] = jnp.concatenate([ks[:, sl].astype(BF16), onehot], axis=1)
        kw_ref[0, g] = jnp.concatenate([kw[:, sl].astype(BF16), zeros], axis=1)


def _nsa_proj_t(x, g, w, cos, sa, sb, b, t):
    m, d = x.shape
    nt = t // TM
    n_sel = -(-t // SEL_BLOCK)
    row = lambda i: (i, 0)
    tab = lambda i: (i % nt, 0)
    fm = lambda i: (i // nt, 0, i % nt)
    kmap = lambda i: (i // nt, 0, i % nt, 0)
    vmap = lambda i: (i // nt, 0, 0, i % nt)
    kv = NSA_KV_HEADS
    return pl.pallas_call(
        functools.partial(_nsa_proj_t_body, tiles_per_seq=nt, n_sel=n_sel),
        grid=(m // TM,),
        in_specs=[pl.BlockSpec((TM, d), row), _resident(g.shape), _resident(w.shape),
                  pl.BlockSpec((TM, LANES), tab), pl.BlockSpec((TM, LANES), tab), pl.BlockSpec((TM, LANES), tab)],
        out_specs=[pl.BlockSpec((TM, 1024), row), pl.BlockSpec((TM, LANES), row),
                   pl.BlockSpec((1, 1024, TM), fm), pl.BlockSpec((1, 512, TM), fm),
                   pl.BlockSpec((1, kv, TM, LANES), kmap), pl.BlockSpec((1, kv, HEAD_DIM, TM), vmap),
                   pl.BlockSpec((1, kv, TM, LANES), kmap), pl.BlockSpec((1, kv, HEAD_DIM, TM), vmap)],
        out_shape=[jax.ShapeDtypeStruct((m, 1024), BF16), jax.ShapeDtypeStruct((m, LANES), F32),
                   jax.ShapeDtypeStruct((b, 1024, t), F32), jax.ShapeDtypeStruct((b, 512, t), F32),
                   jax.ShapeDtypeStruct((b, kv, t, LANES), BF16), jax.ShapeDtypeStruct((b, kv, HEAD_DIM, t), BF16),
                   jax.ShapeDtypeStruct((b, kv, t, LANES), BF16), jax.ShapeDtypeStruct((b, kv, HEAD_DIM, t), BF16)],
        compiler_params=_cparams(("arbitrary",)),
        name="nsa_proj_t",
    )(x, g, w, cos, sa, sb)


def _nsa_proj(x, g, w, cos, sa, sb):
    m, d = x.shape
    ntab = cos.shape[0] // TM
    row = lambda i: (i, 0)
    tab = lambda i: (i % ntab, 0)
    return pl.pallas_call(
        _nsa_proj_body,
        grid=(m // TM,),
        in_specs=[pl.BlockSpec((TM, d), row), _resident(g.shape), _resident(w.shape),
                  pl.BlockSpec((TM, LANES), tab), pl.BlockSpec((TM, LANES), tab), pl.BlockSpec((TM, LANES), tab)],
        out_specs=[pl.BlockSpec((TM, 1024), row), pl.BlockSpec((TM, 1024), row),
                   pl.BlockSpec((TM, 512), row), pl.BlockSpec((TM, LANES), row)],
        out_shape=[jax.ShapeDtypeStruct((m, 1024), BF16), jax.ShapeDtypeStruct((m, 1024), F32),
                   jax.ShapeDtypeStruct((m, 512), F32), jax.ShapeDtypeStruct((m, LANES), F32)],
        compiler_params=_cparams(("arbitrary",)),
        name="nsa_proj",
    )(x, g, w, cos, sa, sb)


def _cmp_bias_body(pe_ref, w_ref, o_ref):
    for s in range(2):
        for half in range(2):
            acc = jnp.zeros((SUBLANES, 256), F32)
            for j in range(CMP_STRIDE):
                acc = acc + _dot(pe_ref[s, half, j].astype(BF16), w_ref[s, j, :, half * 256:(half + 1) * 256])
            o_ref[:, s * 512 + half * 256:s * 512 + (half + 1) * 256] = acc


def _cmp_bias(pe_t, wbd):
    return pl.pallas_call(_cmp_bias_body, out_shape=jax.ShapeDtypeStruct((SUBLANES, 1024), F32),
                          compiler_params=_cparams(None), name="cmp_bias")(pe_t, wbd)


CMP_PITCH = CMP_ROWS + SUBLANES


def _compress_pages_body(x_ref, w_ref, b_ref, o_ref, xs_ref):
    per_page = LANES // CMP_STRIDE
    pages_per_block = x_ref.shape[3] // LANES

    def move(p, carry):
        blk = p // pages_per_block
        off = pl.multiple_of((p % pages_per_block) * LANES, LANES)
        for s in range(2):
            xt = x_ref[blk, s, :, pl.ds(off, LANES)].T
            for c in range(per_page):
                for half in range(2):
                    xs_ref[2 * s + half, pl.ds(p * per_page + c, CMP_STRIDE, stride=CMP_PITCH), :] = (
                        xt[c * CMP_STRIDE:(c + 1) * CMP_STRIDE, half * LANES:(half + 1) * LANES])
        return carry

    lax.fori_loop(0, x_ref.shape[0] * pages_per_block, move, 0, unroll=4)
    rows = o_ref.shape[0]
    for s in range(2):
        acc = jnp.zeros((rows, 512), F32)
        for j in range(CMP_STRIDE):
            xj = jnp.concatenate([xs_ref[2 * s, j * CMP_PITCH:j * CMP_PITCH + rows, :],
                                  xs_ref[2 * s + 1, j * CMP_PITCH:j * CMP_PITCH + rows, :]], axis=1)
            acc = acc + _dot(xj.astype(BF16), w_ref[s, j])
        o_ref[:, s * 512:(s + 1) * 512] = acc + b_ref[0:1, s * 512:(s + 1) * 512]


def _compress_pages(pages_t, wbd, bias):
    n, _, _, npos = pages_t.shape
    chunks = npos // CMP_STRIDE
    pp = CMP_ROWS // chunks
    return pl.pallas_call(
        _compress_pages_body,
        grid=(n // pp,),
        in_specs=[pl.BlockSpec((pp, 2, 256, npos), lambda i: (i, 0, 0, 0)),
                  _resident(wbd.shape), _resident(bias.shape)],
        out_specs=pl.BlockSpec((CMP_ROWS, 1024), lambda i: (i, 0)),
        out_shape=jax.ShapeDtypeStruct((n * chunks, 1024), F32),
        scratch_shapes=[pltpu.VMEM((4, CMP_STRIDE * CMP_PITCH, LANES), F32)],
        compiler_params=_cparams(("arbitrary",)),
        name="compress_pages",
    )(pages_t, wbd, bias)


def _finish_compress(fs):
    n = fs.shape[0]
    kc = fs[:, 0:256] + pltpu.roll(fs[:, 256:512], n - 1, axis=0)
    vc = fs[:, 512:768] + pltpu.roll(fs[:, 768:1024], n - 1, axis=0)
    return kc.astype(BF16), vc.astype(BF16)


def _select_mask(sc_raw, jr, pos, n_real):
    sc = jnp.where(jr * SEL_BLOCK > pos, -FORCE, sc_raw)
    sc = jnp.where(jr == pos // SEL_BLOCK, FORCE, jnp.where(jr == 0, FORCE, sc))
    sc = jnp.where(jr >= n_real, -2.0 * FORCE, sc)
    cnt = jnp.zeros(sc.shape, F32)
    for i in range(n_real):
        ri = sc[i:i + 1, :]
        ge = jnp.where(ri >= sc, 1.0, 0.0)
        gt = jnp.where(ri > sc, 1.0, 0.0)
        cnt = cnt + jnp.where(jr > i, ge, gt)
    keep = jnp.where(sc > -0.5 * FORCE, 0.0, NEG)
    return jnp.where(cnt < float(min(TOP_N, n_real)), keep, NEG)


def _cmp_prompt_body(q_ref, fs_ref, ov_ref, oc_ref, qp_ref):
    i = pl.program_id(1)
    tq = q_ref.shape[0]
    nblk = fs_ref.shape[0]
    fs = fs_ref[...]
    kc = (fs[:, 0:256] + pltpu.roll(fs[:, 256:512], nblk - 1, axis=0)).astype(BF16)
    vc_t = (fs[:, 512:768] + pltpu.roll(fs[:, 768:1024], nblk - 1, axis=0)).T.astype(BF16)
    t_row = i * tq + lax.broadcasted_iota(jnp.int32, (nblk, tq), 1)
    n_col = lax.broadcasted_iota(jnp.int32, (nblk, tq), 0)
    vis_t = n_col * CMP_STRIDE + 2 * CMP_STRIDE - 1 <= t_row
    nsel = ov_ref.shape[0]
    jr = lax.broadcasted_iota(jnp.int32, (nsel, tq), 0)
    pos = i * tq + lax.broadcasted_iota(jnp.int32, (nsel, tq), 1)
    eye = jnp.where(lax.broadcasted_iota(jnp.int32, (tq, tq), 0) == lax.broadcasted_iota(jnp.int32, (tq, tq), 1),
                    1.0, 0.0).astype(BF16)
    for g in range(NSA_KV_HEADS):
        kg = kc[:, g * HEAD_DIM:(g + 1) * HEAD_DIM]
        vg_t = vc_t[g * HEAD_DIM:(g + 1) * HEAD_DIM, :]
        ps_t = jnp.zeros((nblk, tq), F32)
        qs = []
        for r in range(NSA_GROUP):
            hd = g * NSA_GROUP + r
            qh = q_ref[:, hd * HEAD_DIM:(hd + 1) * HEAD_DIM]
            qs.append(qh)
            p_t = _softmax_masked(_nt(kg, qh), vis_t, 0)
            oc_ref[0, hd * HEAD_DIM:(hd + 1) * HEAD_DIM, :] = _dot(vg_t, p_t.astype(BF16)).astype(oc_ref.dtype)
            ps_t = ps_t + p_t
        a_t = _select_mask(_split_dot(ov_ref[...], ps_t), jr, pos, nsel)
        a_t = jnp.concatenate([a_t, jnp.zeros((HEAD_DIM - nsel, tq), F32)], axis=0).astype(BF16)
        a = _nt(eye, a_t).astype(BF16)
        for r in range(NSA_GROUP):
            qp_ref[0, g * NSA_GROUP + r] = jnp.concatenate([qs[r], a], axis=1)


def _cmp_prompt(q, fs, ov_t, b, t):
    nq = t // TQ
    nblk = t // CMP_STRIDE
    return pl.pallas_call(
        _cmp_prompt_body,
        grid=(b, nq),
        in_specs=[pl.BlockSpec((TQ, 1024), lambda bi, i: (bi * nq + i, 0)),
                  pl.BlockSpec((nblk, 1024), lambda bi, i: (bi, 0)),
                  _resident(ov_t.shape)],
        out_specs=[pl.BlockSpec((1, 1024, TQ), lambda bi, i: (bi, 0, i)),
                   pl.BlockSpec((1, NSA_HEADS, TQ, LANES), lambda bi, i: (bi, 0, i, 0))],
        out_shape=[jax.ShapeDtypeStruct((b, 1024, t), BF16),
                   jax.ShapeDtypeStruct((b, NSA_HEADS, t, LANES), BF16)],
        compiler_params=_cparams(("arbitrary", "arbitrary")),
        name="cmp_prompt",
    )(q, fs, ov_t)


def _flash_body(qp_ref, kp_ref, vt_ref, o_ref, st_ref, acc_ref, *, window):
    i = pl.program_id(2)
    tq = qp_ref.shape[2]
    dq = (i * tq + lax.broadcasted_iota(jnp.int32, (tq, tq), 1)) - lax.broadcasted_iota(jnp.int32, (tq, tq), 0)

    def logits(c, mx, masked):
        off = pl.multiple_of(c * tq, tq)
        k = kp_ref[0, 0, pl.ds(off, tq), :]
        out = []
        for r in range(NSA_GROUP):
            s = _nt(k, qp_ref[0, r])
            if masked:
                d = dq - c * tq
                s = jnp.where(d >= 0, s, NEG)
                if window is not None:
                    s = jnp.where(d <= window, s, NEG)
            st_ref[r, pl.ds(off, tq), :] = s
            out.append(jnp.maximum(mx[r], jnp.max(s, axis=0, keepdims=True)))
        return tuple(out)

    def by_pairs(lo, hi, body, init):
        n = hi - lo

        def quad(j, cr):
            for u in range(4):
                cr = body(lo + 4 * j + u, cr)
            return cr

        carry = lax.fori_loop(0, n // 4, quad, init)
        c2 = lo + (n // 4) * 4
        carry = lax.cond(n % 4 >= 2, lambda cr: body(c2 + 1, body(c2, cr)), lambda cr: cr, carry)
        return lax.cond(n % 2 == 1, lambda cr: body(hi - 1, cr), lambda cr: cr, carry)

    mx = tuple(jnp.full((1, tq), NEG, F32) for _ in range(NSA_GROUP))
    if window is None:
        lo = 0
        mx = by_pairs(0, i, lambda c, m: logits(c, m, False), mx)
        mx = logits(i, mx, True)
    else:
        lo = jnp.maximum(i - window // tq, 0)
        mx = by_pairs(lo, i + 1, lambda c, m: logits(c, m, True), mx)

    acc_ref[...] = jnp.zeros(acc_ref.shape, F32)

    def accumulate(c, ls):
        off = pl.multiple_of(c * tq, tq)
        vt = vt_ref[0, 0, :, pl.ds(off, tq)]
        out = []
        for r in range(NSA_GROUP):
            p = jnp.exp2(st_ref[r, pl.ds(off, tq), :] - mx[r])
            out.append(ls[r] + jnp.sum(p, axis=0, keepdims=True))
            acc_ref[r] += _dot(vt, p.astype(BF16))
        return tuple(out)

    ls = by_pairs(lo, i + 1, accumulate, tuple(jnp.zeros((1, tq), F32) for _ in range(NSA_GROUP)))
    for r in range(NSA_GROUP):
        o_ref[0, r * HEAD_DIM:(r + 1) * HEAD_DIM, :] = (acc_ref[r] / ls[r]).astype(o_ref.dtype)


def _flash(qp, kp, vt, window):
    b, _, t, _ = qp.shape
    nq = t // TQ
    return pl.pallas_call(
        functools.partial(_flash_body, window=window),
        grid=(b, NSA_KV_HEADS, nq),
        in_specs=[pl.BlockSpec((1, NSA_GROUP, TQ, LANES), lambda bi, g, i: (bi, g, i, 0)),
                  pl.BlockSpec((1, 1, t, LANES), lambda bi, g, i: (bi, g, 0, 0)),
                  pl.BlockSpec((1, 1, HEAD_DIM, t), lambda bi, g, i: (bi, g, 0, 0))],
        out_specs=pl.BlockSpec((1, NSA_GROUP * HEAD_DIM, TQ), lambda bi, g, i: (bi, g, i)),
        out_shape=jax.ShapeDtypeStruct((b, NSA_HEADS * HEAD_DIM, t), BF16),
        scratch_shapes=[pltpu.VMEM((NSA_GROUP, t, TQ), F32), pltpu.VMEM((NSA_GROUP, HEAD_DIM, TQ), F32)],
        compiler_params=_cparams(("arbitrary", "arbitrary", "arbitrary")),
        name="flash_win" if window is not None else "flash_slc",
    )(qp, kp, vt)


def _diag_blocks(o_full, o_ref):
    rows = o_ref.shape[1] // NSA_KV_HEADS
    for g in range(NSA_KV_HEADS):
        o_ref[0, g * rows:(g + 1) * rows, :] = o_full[g * rows:(g + 1) * rows, g * HEAD_DIM:(g + 1) * HEAD_DIM]


def _cmp_sample_body(pt_ref, *refs, n_pages, past_len, dec):
    fs_refs = refs[:n_pages]
    qt_ref, ov_ref, msum_ref, oc_ref, at_ref = refs[n_pages:]
    fs = jnp.concatenate([r[...] for r in fs_refs], axis=0)
    nblk = fs.shape[0]
    kc, vc = _finish_compress(fs)
    qt = qt_ref[0]
    nq = qt.shape[0]
    pos_c = past_len + (lax.broadcasted_iota(jnp.int32, (nq, nblk), 0) // NSA_GROUP) % dec
    n_r = lax.broadcasted_iota(jnp.int32, (nq, nblk), 1)
    vis = n_r * CMP_STRIDE + 2 * CMP_STRIDE - 1 <= pos_c
    vis = jnp.logical_and(vis, n_r < nblk - 1)
    p = _softmax_masked(_nt(qt, kc), vis, -1)
    _diag_blocks(_dot(p.astype(BF16), vc), oc_ref)
    pos_l = past_len + (lax.broadcasted_iota(jnp.int32, (nblk, nq), 1) // NSA_GROUP) % dec
    n_c = lax.broadcasted_iota(jnp.int32, (nblk, nq), 0)
    vis_t = jnp.logical_and(n_c * CMP_STRIDE + 2 * CMP_STRIDE - 1 <= pos_l, n_c < nblk - 1)
    p_t = _softmax_masked(_nt(kc, qt), vis_t, 0)
    a1 = _split_dot(ov_ref[...], p_t)
    hi = a1.astype(BF16)
    lo = (a1 - hi.astype(F32)).astype(BF16)
    sc = _dot(hi, msum_ref[...]) + _dot(lo, msum_ref[...])
    nj = ov_ref.shape[0]
    jr = lax.broadcasted_iota(jnp.int32, (nj, nq), 0)
    pos = past_len + (lax.broadcasted_iota(jnp.int32, (nj, nq), 1) // NSA_GROUP) % dec
    n_sel = -(-(past_len + dec) // SEL_BLOCK)
    a_t = _select_mask(sc, jr, pos, n_sel)
    a_t = jnp.concatenate([a_t, jnp.zeros((LANES - nj, nq), F32)], axis=0)
    at_ref[0] = a_t.T


def _cmp_sample(page_table, fs_phys, qt, ov_t, msum, past_len, dec):
    nb, n_pages = page_table.shape
    nj = ov_t.shape[0]
    fs_specs = [pl.BlockSpec((SUBLANES, 1024), functools.partial(lambda b, pt, k: (pt[b, k], 0), k=k))
                for k in range(n_pages)]
    grid_spec = pltpu.PrefetchScalarGridSpec(
        num_scalar_prefetch=1,
        grid=(nb,),
        in_specs=fs_specs + [pl.BlockSpec((1, LANES, 256), lambda b, pt: (b, 0, 0)),
                             pl.BlockSpec(ov_t.shape, lambda b, pt: (0, 0)),
                             pl.BlockSpec(msum.shape, lambda b, pt: (0, 0))],
        out_specs=[pl.BlockSpec((1, 64, HEAD_DIM), lambda b, pt: (b, 0, 0)),
                   pl.BlockSpec((1, LANES, LANES), lambda b, pt: (b, 0, 0))],
    )
    return pl.pallas_call(
        functools.partial(_cmp_sample_body, n_pages=n_pages, past_len=past_len, dec=dec),
        grid_spec=grid_spec,
        out_shape=[jax.ShapeDtypeStruct((nb, 64, HEAD_DIM), F32),
                   jax.ShapeDtypeStruct((nb, LANES, LANES), F32)],
        compiler_params=_cparams(("arbitrary",)),
        name="cmp_sample",
    )(page_table, *([fs_phys] * n_pages), qt, ov_t, msum)


def _attn_sample_body(*refs, n_blocks, has_pt, has_bm, has_pm):
    refs = list(refs[1:] if has_pt else refs)
    kv_refs = refs[:n_blocks]
    rest = refs[n_blocks:]
    new_ref, qt_ref = rest[0], rest[1]
    rest = rest[2:]
    bm_ref = rest.pop(0) if has_bm else None
    pm_ref = rest.pop(0) if has_pm else None
    nm_ref, o_ref, st_ref = rest
    qt = qt_ref[0]
    nq = qt.shape[0]
    rows = kv_refs[0].shape[3]
    per = rows // SEL_BLOCK
    if has_bm:
        blk_of_lane = lax.broadcasted_iota(jnp.int32, (nq, rows), 1) // SEL_BLOCK
    m = jnp.full((nq, 1), NEG, F32)
    for kb in range(n_blocks):
        s = _dot(qt, kv_refs[kb][0, 0].astype(BF16))
        if has_pm:
            s = s + pm_ref[:, kb * rows:(kb + 1) * rows]
        if has_bm:
            add = bm_ref[0, :, kb * per:kb * per + 1]
            for u in range(1, per):
                add = jnp.where(blk_of_lane == u, bm_ref[0, :, kb * per + u:kb * per + u + 1], add)
            s = s + add
        st_ref[:, kb * rows:(kb + 1) * rows] = s
        m = jnp.maximum(m, jnp.max(s, axis=-1, keepdims=True))
    s_new = _nt(qt, new_ref[0, :, 0:256].astype(BF16)) + nm_ref[...]
    if has_bm:
        s_new = s_new + bm_ref[0, :, n_blocks * per:n_blocks * per + 1]
    m = jnp.maximum(m, jnp.max(s_new, axis=-1, keepdims=True))
    e_new = jnp.exp2(s_new - m)
    l = jnp.sum(e_new, axis=-1, keepdims=True)
    acc = _dot(e_new.astype(BF16), new_ref[0, :, 256:512].astype(BF16))
    for kb in range(n_blocks):
        e = jnp.exp2(st_ref[:, kb * rows:(kb + 1) * rows] - m)
        l = l + jnp.sum(e, axis=-1, keepdims=True)
        acc = acc + _nt(e.astype(BF16), kv_refs[kb][0, 1].astype(BF16))
    _diag_blocks(acc / jnp.maximum(l, 1e-30), o_ref)


def _attn_sample(kv, slot_blk, n_blocks, new, qt, new_mask, page_table=None, blk_mask=None, past_mask=None):
    nb = qt.shape[0]
    kv_rows = kv.shape[3]
    has_pt = page_table is not None
    has_bm = blk_mask is not None
    has_pm = past_mask is not None
    if has_pt:
        kv_specs = [pl.BlockSpec((1, 2, 256, kv_rows),
                                 functools.partial(lambda b, pt, k: (pt[b, k], slot_blk, 0, 0), k=k))
                    for k in range(n_blocks)]
        im = lambda b, pt: (b, 0, 0)
        cm = lambda b, pt: (0, 0)
    else:
        kv_specs = [pl.BlockSpec((1, 2, 256, kv_rows), lambda b: (b, slot_blk, 0, 0))]
        im = lambda b: (b, 0, 0)
        cm = lambda b: (0, 0)
    in_specs = kv_specs + [pl.BlockSpec((1, SUBLANES, 512), im), pl.BlockSpec((1, LANES, 256), im)]
    args = [kv] * n_blocks + [new, qt]
    if has_bm:
        in_specs.append(pl.BlockSpec((1,) + blk_mask.shape[1:], im))
        args.append(blk_mask)
    if has_pm:
        in_specs.append(pl.BlockSpec(past_mask.shape, cm))
        args.append(past_mask)
    in_specs.append(pl.BlockSpec(new_mask.shape, cm))
    args.append(new_mask)
    out_spec = pl.BlockSpec((1, 64, HEAD_DIM), im)
    scratch = [pltpu.VMEM((LANES, n_blocks * kv_rows), F32)]
    body = functools.partial(_attn_sample_body, n_blocks=n_blocks, has_pt=has_pt, has_bm=has_bm, has_pm=has_pm)
    out_shape = jax.ShapeDtypeStruct((nb, 64, HEAD_DIM), F32)
    if has_pt:
        gs = pltpu.PrefetchScalarGridSpec(num_scalar_prefetch=1, grid=(nb,), in_specs=in_specs,
                                          out_specs=out_spec, scratch_shapes=scratch)
        return pl.pallas_call(body, grid_spec=gs, out_shape=out_shape, compiler_params=_cparams(("arbitrary",)),
                              name="slc_sample")(page_table, *args)
    return pl.pallas_call(body, grid=(nb,), in_specs=in_specs, out_specs=out_spec, scratch_shapes=scratch,
                          out_shape=out_shape, compiler_params=_cparams(("arbitrary",)), name="win_sample")(*args)


def _out_proj_body(*refs, n_in, gated, feature_major):
    o_refs = refs[:n_in]
    rest = refs[n_in:]
    if gated:
        gate_ref, rest = rest[0], rest[1:]
    x_ref, w_ref, g_ref, y_ref = rest
    tm = x_ref.shape[0]
    if feature_major:
        gates_t = gate_ref[...].T
        heads = []
        for hd in range(NSA_HEADS):
            acc = jnp.zeros((HEAD_DIM, tm), F32)
            for kbr in range(n_in):
                row = gates_t[hd * 3 + kbr:hd * 3 + kbr + 1, :]
                acc = acc + row * o_refs[kbr][0, hd * HEAD_DIM:(hd + 1) * HEAD_DIM, :].astype(F32)
            heads.append(acc.astype(BF16))
        y = _tn(jnp.concatenate(heads, axis=0), w_ref[...])
        y_ref[...] = x_ref[...] + _rms(y, g_ref[...])
        return
    if gated:
        lane = lax.broadcasted_iota(jnp.int32, (tm, LANES), 1)
        gates = gate_ref[...]
        chunks = []
        for c in range(1024 // LANES):
            acc = jnp.zeros((tm, LANES), F32)
            for kbr in range(n_in):
                c0 = (2 * c) * 3 + kbr
                c1 = (2 * c + 1) * 3 + kbr
                gexp = jnp.where(lane < HEAD_DIM, gates[:, c0:c0 + 1], gates[:, c1:c1 + 1])
                acc = acc + gexp * o_refs[kbr][:, c * LANES:(c + 1) * LANES].astype(F32)
            chunks.append(acc.astype(BF16))
        o = jnp.concatenate(chunks, axis=1)
    else:
        o = o_refs[0][...].astype(BF16)
    y = _dot(o, w_ref[...])
    y_ref[...] = x_ref[...] + _rms(y, g_ref[...])


def _out_proj(os_, gates, x, w, g, seq_len=None):
    m, d = x.shape
    row = lambda i: (i, 0)
    gated = gates is not None
    feature_major = seq_len is not None
    if feature_major:
        nt = seq_len // TM
        in_specs = [pl.BlockSpec((1, 1024, TM), lambda i: (i // nt, 0, i % nt)) for _ in os_]
    else:
        in_specs = [pl.BlockSpec((TM, 1024), row) for _ in os_]
    args = list(os_)
    if gated:
        in_specs.append(pl.BlockSpec((TM, LANES), row))
        args.append(gates)
    in_specs += [pl.BlockSpec((TM, d), row), _resident(w.shape), _resident(g.shape)]
    args += [x, w, g]
    return pl.pallas_call(
        functools.partial(_out_proj_body, n_in=len(os_), gated=gated, feature_major=feature_major),
        grid=(m // TM,), in_specs=in_specs, out_specs=pl.BlockSpec((TM, d), row),
        out_shape=jax.ShapeDtypeStruct((m, d), F32),
        compiler_params=_cparams(("arbitrary",)), name="out_proj",
    )(*args)


def _ffn_body(*refs, sample, tiles_per_seq, nb):
    if sample:
        x_ref, g2_ref, g3_ref, wu_ref, cw_ref, cb_ref, wd_ref, buf_ref, y_ref, tail_ref = refs
    else:
        x_ref, g2_ref, g3_ref, wu_ref, cw_ref, cb_ref, wd_ref, y_ref, tail_ref, carry_ref = refs
    x = x_ref[...]
    tm = x.shape[0]
    f = wd_ref.shape[0]
    h = _rms(x, g2_ref[...]).astype(BF16)
    acc = jnp.zeros(x.shape, F32)
    if not sample:
        first = pl.program_id(0) % tiles_per_seq == 0
        row = lax.broadcasted_iota(jnp.int32, (tm, FFN_FCHUNK), 0)
    for fc in range(f // FFN_FCHUNK):
        sl = slice(fc * FFN_FCHUNK, (fc + 1) * FFN_FCHUNK)
        gate = _dot(h, wu_ref[:, sl])
        val = _dot(h, wu_ref[:, f + fc * FFN_FCHUNK:f + (fc + 1) * FFN_FCHUNK])
        if sample:
            prev1 = jnp.concatenate([buf_ref[nb:2 * nb, sl], gate[0:tm - nb]], axis=0)
            prev2 = jnp.concatenate([buf_ref[0:2 * nb, sl], gate[0:tm - 2 * nb]], axis=0)
            tail_ref[:, sl] = gate[tm - 2 * nb:tm]
        else:
            carry = carry_ref[:, sl]
            c6 = jnp.where(first, 0.0, carry[6:7, :])
            c7 = jnp.where(first, 0.0, carry[7:8, :])
            prev1 = jnp.where(row == 0, c7, pltpu.roll(gate, 1, axis=0))
            prev2 = jnp.where(row == 0, c6, jnp.where(row == 1, c7, pltpu.roll(gate, 2, axis=0)))
            carry_ref[:, sl] = gate[tm - SUBLANES:tm]
            tail_ref[:, sl] = gate[tm - SUBLANES:tm]
        conv = cb_ref[0:1, sl] + cw_ref[0:1, sl] * prev2 + cw_ref[1:2, sl] * prev1 + cw_ref[2:3, sl] * gate
        act = (jax.nn.gelu(conv) * val).astype(BF16)
        acc = acc + _dot(act, wd_ref[sl, :])
    y_ref[...] = x + _rms(acc, g3_ref[...])


def _ffn(x, g2, g3, wu, cw, cb, wd, seq_len=None, buf=None):
    m, d = x.shape
    f = wd.shape[0]
    sample = buf is not None
    row = lambda i: (i, 0)
    if sample:
        tm = m
        nb = buf.shape[0] // 2
        tail_shape, tail_block = (2 * nb, f), (2 * nb, f)
        scratch = []
    else:
        tm = FFN_TM
        nb = 0
        tail_shape, tail_block = (m // tm * SUBLANES, f), (SUBLANES, f)
        scratch = [pltpu.VMEM((SUBLANES, f), F32)]
    in_specs = [pl.BlockSpec((tm, d), row), _resident(g2.shape), _resident(g3.shape), _resident(wu.shape),
                _resident(cw.shape), _resident(cb.shape), _resident(wd.shape)]
    args = [x, g2, g3, wu, cw, cb, wd]
    if sample:
        in_specs.append(_resident(buf.shape))
        args.append(buf)
    return pl.pallas_call(
        functools.partial(_ffn_body, sample=sample, tiles_per_seq=(seq_len // tm if not sample else 1), nb=nb),
        grid=(m // tm,), in_specs=in_specs,
        out_specs=[pl.BlockSpec((tm, d), row), pl.BlockSpec(tail_block, row)],
        out_shape=[jax.ShapeDtypeStruct((m, d), F32), jax.ShapeDtypeStruct(tail_shape, F32)],
        scratch_shapes=scratch,
        compiler_params=_cparams(("arbitrary",)), name="ffn_sample" if sample else "ffn_prompt",
    )(*args)


def _gla_proj_body(x_ref, g_ref, w_ref, wg_ref, bg_ref, q_ref, k_ref, v_ref, sr_ref, la_ref):
    h = _rms(x_ref[...], g_ref[...]).astype(BF16)
    nk = GLA_HEADS * GLA_DK
    nv = GLA_HEADS * GLA_DV
    for c in range(nk // 256):
        q_ref[:, c * 256:(c + 1) * 256] = _dot(h, w_ref[:, c * 256:(c + 1) * 256]) * (GLA_DK ** -0.5)
        k_ref[:, c * 256:(c + 1) * 256] = _dot(h, w_ref[:, nk + c * 256:nk + (c + 1) * 256])
    for c in range(nv // 256):
        v_ref[:, c * 256:(c + 1) * 256] = _dot(h, w_ref[:, 2 * nk + c * 256:2 * nk + (c + 1) * 256]).astype(BF16)
        r = _dot(h, w_ref[:, 2 * nk + nv + c * 256:2 * nk + nv + (c + 1) * 256])
        sr_ref[:, c * 256:(c + 1) * 256] = jax.nn.silu(r).astype(BF16)
    low = _dot(h, w_ref[:, 2 * nk + 2 * nv:2 * nk + 2 * nv + LANES]).astype(BF16)
    gz = _dot(low, wg_ref[...]) + bg_ref[...]
    log_sig = jnp.minimum(gz, 0.0) - jnp.log1p(jnp.exp(-jnp.abs(gz)))
    la_ref[...] = log_sig / GLA_GATE_TEMP


def _gla_proj(x, g, w, wg, bg):
    m, d = x.shape
    nk = GLA_HEADS * GLA_DK
    nv = GLA_HEADS * GLA_DV
    row = lambda i: (i, 0)
    return pl.pallas_call(
        _gla_proj_body, grid=(m // TM,),
        in_specs=[pl.BlockSpec((TM, d), row), _resident(g.shape), _resident(w.shape),
                  _resident(wg.shape), _resident(bg.shape)],
        out_specs=[pl.BlockSpec((TM, nk), row), pl.BlockSpec((TM, nk), row), pl.BlockSpec((TM, nv), row),
                   pl.BlockSpec((TM, nv), row), pl.BlockSpec((TM, nk), row)],
        out_shape=[jax.ShapeDtypeStruct((m, nk), F32), jax.ShapeDtypeStruct((m, nk), F32),
                   jax.ShapeDtypeStruct((m, nv), BF16), jax.ShapeDtypeStruct((m, nv), BF16),
                   jax.ShapeDtypeStruct((m, nk), F32)],
        compiler_params=_cparams(("arbitrary",)), name="gla_proj",
    )(x, g, w, wg, bg)


def _cumsum_rows(x):
    n = x.shape[0]
    row = lax.broadcasted_iota(jnp.int32, x.shape, 0)
    sh = 1
    while sh < n:
        x = x + jnp.where(row >= sh, pltpu.roll(x, sh, axis=0), 0.0)
        sh *= 2
    return x


def _gla_rec_body(*refs, has_s0):
    if has_s0:
        q_ref, k_ref, la_ref, v_ref, sr_ref, gn_ref, s0_ref, o_ref, so_ref, st_ref = refs
    else:
        q_ref, k_ref, la_ref, v_ref, sr_ref, gn_ref, o_ref, so_ref, st_ref = refs
    c = pl.program_id(1)
    cs = q_ref.shape[0]

    @pl.when(c == 0)
    def _():
        st_ref[...] = s0_ref[0] if has_s0 else jnp.zeros(st_ref.shape, F32)

    row = lax.broadcasted_iota(jnp.int32, (cs, 1), 0)
    trow = lax.broadcasted_iota(jnp.int32, (cs, cs), 0)
    scol = lax.broadcasted_iota(jnp.int32, (cs, cs), 1)
    r8 = lax.broadcasted_iota(jnp.int32, (SUBLANES, 1), 0)
    lane8 = lax.broadcasted_iota(jnp.int32, (SUBLANES, cs), 1)
    for hd in range(GLA_HEADS):
        ksl = slice(hd * GLA_DK, (hd + 1) * GLA_DK)
        vsl = slice(hd * GLA_DV, (hd + 1) * GLA_DV)
        q, k, v = q_ref[:, ksl], k_ref[:, ksl], v_ref[:, vsl]
        cum = _cumsum_rows(la_ref[:, ksl])
        last = cum[cs - 1:cs, :]
        s_old = st_ref[hd]
        out = _dot((q * jnp.exp(cum)).astype(BF16), s_old.astype(BF16))
        att = None
        hh = cs // 2
        while hh >= SUBLANES:
            nblk = cs // (2 * hh)
            ref = jnp.concatenate(
                [jnp.broadcast_to(cum[u * 2 * hh + hh - 1:u * 2 * hh + hh, :], (2 * hh, GLA_DK)) for u in range(nblk)],
                axis=0) if nblk > 1 else jnp.broadcast_to(cum[hh - 1:hh, :], (cs, GLA_DK))
            second = (row % (2 * hh)) >= hh
            qh = jnp.where(second, q * jnp.exp(jnp.minimum(cum - ref, 0.0)), 0.0).astype(BF16)
            kh = jnp.where(second, 0.0, k * jnp.exp(jnp.minimum(ref - cum, 0.0))).astype(BF16)
            a = _nt(qh, kh)
            if nblk > 1:
                a = jnp.where(trow // (2 * hh) == scol // (2 * hh), a, 0.0)
            att = a if att is None else att + a
            hh //= 2
        slabs = []
        for g8 in range(cs // SUBLANES):
            r0 = g8 * SUBLANES
            cg, qg, kg = cum[r0:r0 + SUBLANES], q[r0:r0 + SUBLANES], k[r0:r0 + SUBLANES]
            slab = jnp.zeros((SUBLANES, GLA_DV if att is None else cs), F32)
            for s in range(SUBLANES):
                e = jnp.exp(jnp.minimum(cg - cg[s:s + 1, :], 0.0))
                col = jnp.sum(e * qg * kg[s:s + 1, :], axis=1, keepdims=True)
                col = jnp.where(r8 >= s, col, 0.0)
                if att is None:
                    slab = slab + col * v[r0 + s:r0 + s + 1, :].astype(F32)
                else:
                    slab = jnp.where(lane8 == r0 + s, col, slab)
            slabs.append(slab)
        if att is None:
            out = out + slabs[0]
        else:
            att = att + jnp.concatenate(slabs, axis=0)
            out = out + _dot(att.astype(BF16), v)
        kt = (k * jnp.exp(last - cum)).astype(BF16)
        dcol = jnp.broadcast_to(jnp.exp(last), (SUBLANES, GLA_DK)).T[:, 0:1]
        st_ref[hd] = dcol * s_old + _tn(kt, v)
        on = _rms(out, gn_ref[...])
        o_ref[:, vsl] = (on * sr_ref[:, vsl].astype(F32)).astype(o_ref.dtype)

    @pl.when(c == pl.num_programs(1) - 1)
    def _():
        so_ref[0] = st_ref[...]


def _gla_rec(q, k, la, v, sr, gn, nb, chunk, s0=None):
    m = q.shape[0]
    nc = m // nb // chunk
    nk = GLA_HEADS * GLA_DK
    nv = GLA_HEADS * GLA_DV
    row = lambda b, c: (b * nc + c, 0)
    st_spec = pl.BlockSpec((1, GLA_HEADS, GLA_DK, GLA_DV), lambda b, c: (b, 0, 0, 0))
    in_specs = [pl.BlockSpec((chunk, nk), row), pl.BlockSpec((chunk, nk), row), pl.BlockSpec((chunk, nk), row),
                pl.BlockSpec((chunk, nv), row), pl.BlockSpec((chunk, nv), row),
                pl.BlockSpec(gn.shape, lambda b, c: (0, 0))]
    args = [q, k, la, v, sr, gn]
    if s0 is not None:
        in_specs.append(st_spec)
        args.append(s0)
    return pl.pallas_call(
        functools.partial(_gla_rec_body, has_s0=s0 is not None),
        grid=(nb, nc), in_specs=in_specs,
        out_specs=[pl.BlockSpec((chunk, nv), row), st_spec],
        out_shape=[jax.ShapeDtypeStruct((m, nv), BF16),
                   jax.ShapeDtypeStruct((nb, GLA_HEADS, GLA_DK, GLA_DV), F32)],
        scratch_shapes=[pltpu.VMEM((GLA_HEADS, GLA_DK, GLA_DV), F32)],
        compiler_params=_cparams(("arbitrary", "arbitrary")),
        name="gla_rec_sample" if s0 is not None else "gla_rec_prompt",
    )(*args)


def _rope_tables(pos):
    half = HEAD_DIM // 2
    inv = ROPE_THETA ** (-jnp.arange(half, dtype=F32) / half)
    ang = pos.astype(F32)[:, None] * inv[None, :]
    cos, sin = jnp.cos(ang), jnp.sin(ang)
    z = jnp.zeros_like(sin)
    cos_t = jnp.tile(cos, (1, 4))
    sa = jnp.tile(jnp.concatenate([-sin, z], axis=1), (1, 2))
    sb = jnp.tile(jnp.concatenate([z, sin], axis=1), (1, 2))
    return cos_t, sa, sb


def _pad_cols(w, n):
    return jnp.pad(w, ((0, 0), (0, n - w.shape[1])))


def _cmp_weights(cmp_pe, cmp_w):
    g = NSA_KV_HEADS
    w = cmp_w.reshape(2, 2, CMP_STRIDE, HEAD_DIM, HEAD_DIM)
    eye = jnp.eye(g, dtype=F32)
    wbd = jnp.einsum('shjde,gk->sjgdhke', w, eye)
    wbd = wbd.reshape(2, CMP_STRIDE, g * HEAD_DIM, 2 * g * HEAD_DIM).astype(BF16)
    pe = cmp_pe.reshape(2, 2, CMP_STRIDE, 1, 1, HEAD_DIM)
    pe_t = jnp.broadcast_to(pe, (2, 2, CMP_STRIDE, SUBLANES, g, HEAD_DIM)).reshape(2, 2, CMP_STRIDE, SUBLANES, g * HEAD_DIM)
    return wbd, pe_t


def _overlap_t(n_blk, n_sel, rows, cols):
    i = jnp.arange(cols)[None, :]
    j = jnp.arange(rows)[:, None]
    ov = (i * CMP_STRIDE + 2 * CMP_STRIDE > j * SEL_BLOCK) & (i * CMP_STRIDE < (j + 1) * SEL_BLOCK)
    ov = ov & (i < n_blk) & (j < n_sel)
    return ov.astype(BF16)


def _nsa_layer_prompt(x, g, w_in, wbd, cbias, w_o, tabs, b, t):
    q, gates, rows_t, win_t, k_slc, vt_slc, k_win, vt_win = _nsa_proj_t(x, g[0:1], w_in, *tabs, b, t)
    fs = _compress_pages(rows_t.reshape(b, 4, NSA_KV_HEADS * HEAD_DIM, t), wbd, cbias)
    n_blk = t // CMP_STRIDE - 1
    n_sel = -(-t // SEL_BLOCK)
    oc, qp = _cmp_prompt(q, fs, _overlap_t(n_blk, n_sel, n_sel, t // CMP_STRIDE), b, t)
    o_s = _flash(qp, k_slc, vt_slc, None)
    o_w = _flash(qp, k_win, vt_win, WINDOW)
    x = _out_proj([oc, o_s, o_w], gates, x, w_o, g[1:2], seq_len=t)
    return x, rows_t, win_t


def _to_rows(o, nb, dec):
    o = o.reshape(nb, NSA_KV_HEADS, dec, NSA_GROUP, HEAD_DIM).transpose(2, 0, 1, 3, 4)
    return o.reshape(dec * nb, NSA_HEADS * HEAD_DIM).astype(BF16)


def _nsa_layer_sample(x, g, w_in, wbd, cbias, w_o, tabs, cache, page_table, win_state, nb, dec):
    n_phys, page = cache.shape[0], cache.shape[1]
    n_pages = page_table.shape[1]
    past_len = n_pages * page
    q, rows, win, gates = _nsa_proj(x, g[0:1], w_in, *tabs)
    pages_t = cache.transpose(0, 2, 3, 4, 1).reshape(n_phys, 4, NSA_KV_HEADS * HEAD_DIM, page)
    wb = win_state.shape[1]
    win_t = win_state.transpose(0, 2, 3, 4, 1).reshape(nb, 2, NSA_KV_HEADS * HEAD_DIM, wb)
    fs_phys = _compress_pages(pages_t, wbd, cbias)
    q5 = q.reshape(dec, nb, NSA_KV_HEADS, NSA_GROUP, HEAD_DIM).transpose(1, 2, 0, 3, 4)
    eye = jnp.eye(NSA_KV_HEADS, dtype=BF16)
    qt = q5.reshape(nb, NSA_KV_HEADS, dec * NSA_GROUP, 1, HEAD_DIM) * eye[None, :, None, :, None]
    qt = qt.reshape(nb, NSA_KV_HEADS * dec * NSA_GROUP, NSA_KV_HEADS * HEAD_DIM)
    qt = jnp.pad(qt, ((0, 0), (0, LANES - qt.shape[1]), (0, 0)))
    n_blk = (past_len + dec) // CMP_STRIDE - 1
    n_sel = -(-(past_len + dec) // SEL_BLOCK)
    nj = -(-n_sel // SUBLANES) * SUBLANES
    lane = jnp.arange(LANES)
    msum = (lane[:, None] // NSA_GROUP == lane[None, :] // NSA_GROUP).astype(BF16)
    oc, a_t = _cmp_sample(page_table, fs_phys, qt, _overlap_t(n_blk, n_sel, nj, past_len // CMP_STRIDE), msum, past_len, dec)
    t_row = ((lane // NSA_GROUP) % dec)[:, None]
    j8 = jnp.arange(SUBLANES)[None, :]
    new_mask = jnp.where((j8 < dec) & (j8 <= t_row), 0.0, NEG).astype(F32)
    win_mask = jnp.where(jnp.arange(wb)[None, :] >= t_row + wb - WINDOW, 0.0, NEG).astype(F32)

    def new_rows(a):
        a = a.reshape(dec, nb, 512).transpose(1, 0, 2)
        return jnp.pad(a, ((0, 0), (0, SUBLANES - dec), (0, 0)))

    o_s = _attn_sample(pages_t, 1, n_pages, new_rows(rows[:, 512:1024]), qt, new_mask,
                       page_table=page_table, blk_mask=a_t)
    o_w = _attn_sample(win_t, 0, 1, new_rows(win), qt, new_mask, past_mask=win_mask)
    x = _out_proj([_to_rows(oc, nb, dec), _to_rows(o_s, nb, dec), _to_rows(o_w, nb, dec)], gates, x, w_o, g[1:2])
    return x, rows, win


def _gla_layer(x, g, w_in, wg, bg, gn, w_o, nb, seq, s0=None):
    q, k, v, sr, la = _gla_proj(x, g[0:1], w_in, wg, bg)
    if s0 is None:
        o, s = _gla_rec(q, k, la, v, sr, gn, nb, GLA_CHUNK)
    else:
        def seqs(a):
            a = a.reshape(seq, nb, a.shape[1]).transpose(1, 0, 2)
            return jnp.pad(a, ((0, 0), (0, SUBLANES - seq), (0, 0))).reshape(nb * SUBLANES, a.shape[2])
        o, s = _gla_rec(seqs(q), seqs(k), seqs(la), seqs(v), seqs(sr), gn, nb, SUBLANES, s0=s0)
        o = o.reshape(nb, SUBLANES, o.shape[1])[:, :seq].transpose(1, 0, 2).reshape(seq * nb, o.shape[1])
    x = _out_proj([o], None, x, w_o, g[1:2])
    return x, s


def kernel(x_prompt, x_sample, cache_nsa_kv, state_win_kv, state_gla, state_ffn_conv, page_table, norm_gain, nsa_w_in, nsa_cmp_pe, nsa_cmp_w, nsa_w_o, gla_w_in, gla_w_gate_up, gla_b_gate, gla_norm_gain, gla_w_o, ffn_w_up, ffn_conv_w, ffn_conv_b, ffn_w_down):
    b, t, d = x_prompt.shape
    nb, dec, _ = x_sample.shape
    depth = norm_gain.shape[0]
    f = ffn_w_down.shape[1]
    past_len = page_table.shape[1] * cache_nsa_kv.shape[2]

    xp = x_prompt.reshape(b * t, d)
    xs = x_sample.transpose(1, 0, 2).reshape(dec * nb, d)
    tabs_p = _rope_tables(jnp.arange(t, dtype=jnp.int32))
    tabs_s = _rope_tables(past_len + jnp.repeat(jnp.arange(dec, dtype=jnp.int32), nb))

    nsa_p, nsa_s, win_p, win_s, gla_p, gla_s, ffn_p, ffn_s = [], [], [], [], [], [], [], []
    for i in range(depth):
        g = norm_gain[i]
        a = i // 2
        if i % 2 == 0:
            w_in = _pad_cols(nsa_w_in[a], 2688).astype(BF16)
            w_o = nsa_w_o[a].astype(BF16)
            wbd, pe_t = _cmp_weights(nsa_cmp_pe[a], nsa_cmp_w[a])
            cbias = _cmp_bias(pe_t, wbd)
            xp, rows_t, win_t = _nsa_layer_prompt(xp, g, w_in, wbd, cbias, w_o, tabs_p, b, t)
            nsa_p.append(rows_t.reshape(b, 4, NSA_KV_HEADS, HEAD_DIM, t).transpose(0, 4, 1, 2, 3))
            nw = min(WINDOW, t)
            win_p.append(win_t.reshape(b, 2, NSA_KV_HEADS, HEAD_DIM, t)[..., t - nw:].transpose(0, 4, 1, 2, 3))
            xs, rows, win = _nsa_layer_sample(xs, g, w_in, wbd, cbias, w_o, tabs_s, cache_nsa_kv[a], page_table,
                                              state_win_kv[a], nb, dec)
            nsa_s.append(rows.reshape(dec, nb, 4, NSA_KV_HEADS, HEAD_DIM).transpose(1, 0, 2, 3, 4))
            new_w = win.reshape(dec, nb, 2, NSA_KV_HEADS, HEAD_DIM).transpose(1, 0, 2, 3, 4)
            wb = state_win_kv.shape[2]
            win_s.append(jnp.concatenate([state_win_kv[a], new_w], axis=1)[:, -wb:])
        else:
            w_in = _pad_cols(gla_w_in[a], 3200).astype(BF16)
            wg = jnp.pad(gla_w_gate_up[a], ((0, LANES - GLA_GATE_RANK), (0, 0))).astype(BF16)
            bg = gla_b_gate[a][None, :]
            gn = gla_norm_gain[a][None, :]
            w_o = gla_w_o[a].astype(BF16)
            xp, s = _gla_layer(xp, g, w_in, wg, bg, gn, w_o, b, t)
            gla_p.append(s)
            xs, s = _gla_layer(xs, g, w_in, wg, bg, gn, w_o, nb, dec, s0=state_gla[a])
            gla_s.append(s)
        wu = ffn_w_up[i].astype(BF16)
        wd = ffn_w_down[i].astype(BF16)
        cw = jnp.pad(ffn_conv_w[i], ((0, SUBLANES - ffn_conv_w.shape[1]), (0, 0)))
        cb = ffn_conv_b[i][None, :]
        xp, tail = _ffn(xp, g[2:3], g[3:4], wu, cw, cb, wd, seq_len=t)
        tail = tail.reshape(b, t // FFN_TM, SUBLANES, f)[:, -1, SUBLANES - 2:]
        ffn_p.append(tail)
        buf = state_ffn_conv[i].transpose(1, 0, 2).reshape(2 * nb, f)
        xs, tail = _ffn(xs, g[2:3], g[3:4], wu, cw, cb, wd, buf=buf)
        ffn_s.append(tail.reshape(2, nb, f).transpose(1, 0, 2))

    y_prompt = xp.reshape(b, t, d)
    y_sample = xs.reshape(dec, nb, d).transpose(1, 0, 2)
    return (y_prompt, y_sample, jnp.stack(nsa_p), jnp.stack(nsa_s), jnp.stack(win_p), jnp.stack(win_s),
            jnp.stack(gla_p), jnp.stack(gla_s), jnp.stack(ffn_p), jnp.stack(ffn_s))
```

```python
import functools
import math

import jax
import jax.numpy as jnp
from jax import lax
from jax.experimental import pallas as pl
from jax.experimental.pallas import tpu as pltpu

F32, BF16 = jnp.float32, jnp.bfloat16

HEAD_DIM = 64
NSA_HEADS = 16
NSA_KV_HEADS = 4
NSA_GROUP = 4
CMP_STRIDE = 16
SEL_BLOCK = 64
TOP_N = 16
WINDOW = 512
ROPE_THETA = 10000.0
GLA_HEADS = 4
GLA_DK = 128
GLA_DV = 256
GLA_GATE_RANK = 16
GLA_GATE_TEMP = 16.0
NORM_EPS = 1e-6
FORCE = 1e6
NEG = -1e30

LANES = 128
SUBLANES = 8
VMEM_LIMIT = 48 * 1024 * 1024

TM = 256
TQ = 256
CMP_ROWS = 256
GLA_CHUNK = 128
SLC_SEQS = 2
WIN_SEQS = 4
CMP_SEQS = 4
GLA_SEQS = 4
FFN_FCHUNK = 1408
FFN_TM = 512


def _cparams(sem):
    return pltpu.CompilerParams(dimension_semantics=sem, vmem_limit_bytes=VMEM_LIMIT)


def _resident(shape):
    nd = len(shape)
    return pl.BlockSpec(shape, lambda *_: (0,) * nd, pipeline_mode=pl.Buffered(1))


def _rms(x, g):
    return x * lax.rsqrt(jnp.mean(x * x, axis=-1, keepdims=True) + NORM_EPS) * g


def _nt(a, b):
    return lax.dot_general(a, b, (((1,), (1,)), ((), ())), preferred_element_type=F32)


def _tn(a, b):
    return lax.dot_general(a, b, (((0,), (0,)), ((), ())), preferred_element_type=F32)


def _dot(a, b):
    return jnp.dot(a, b, preferred_element_type=F32)


def _split_dot(w, x):
    hi = x.astype(BF16)
    lo = (x - hi.astype(F32)).astype(BF16)
    return _dot(w, hi) + _dot(w, lo)


def _softmax_masked(s, mask, axis):
    s = jnp.where(mask, s, NEG)
    m = jnp.max(s, axis=axis, keepdims=True)
    e = jnp.where(mask, jnp.exp2(s - m), 0.0)
    return e / jnp.maximum(jnp.sum(e, axis=axis, keepdims=True), 1e-30)


Q_SCALE = HEAD_DIM ** -0.5 * math.log2(math.e)


def _nsa_proj_parts(x_ref, g_ref, w_ref, cos_ref, sa_ref, sb_ref, q_ref, gate_ref):
    h = _rms(x_ref[...], g_ref[...]).astype(BF16)
    cos, sa, sb = cos_ref[...], sa_ref[...], sb_ref[...]

    def rope(z):
        return z * cos + pltpu.roll(z, 96, axis=1) * sa + pltpu.roll(z, 32, axis=1) * sb

    def proj(c0):
        return _dot(h, w_ref[:, c0:c0 + 256])

    def rope256(z):
        return jnp.concatenate([rope(z[:, :LANES]), rope(z[:, LANES:])], axis=1)

    for c in range(4):
        q_ref[:, c * 256:(c + 1) * 256] = (rope256(proj(c * 256)) * Q_SCALE).astype(BF16)
    gate_ref[...] = jax.nn.sigmoid(_dot(h, w_ref[:, 2560:2688]))
    return (rope256(proj(1024)), proj(1280), rope256(proj(1536)), proj(1792), rope256(proj(2048)), proj(2304))


def _nsa_proj_body(x_ref, g_ref, w_ref, cos_ref, sa_ref, sb_ref, q_ref, rows_ref, win_ref, gate_ref):
    parts = _nsa_proj_parts(x_ref, g_ref, w_ref, cos_ref, sa_ref, sb_ref, q_ref, gate_ref)
    for c in range(4):
        rows_ref[:, c * 256:(c + 1) * 256] = parts[c]
    win_ref[:, 0:256] = parts[4]
    win_ref[:, 256:512] = parts[5]


def _nsa_proj_t_body(x_ref, g_ref, w_ref, cos_ref, sa_ref, sb_ref, q_ref, gate_ref,
                     rows_t_ref, win_t_ref, ks_ref, vs_ref, kw_ref, vw_ref, *, tiles_per_seq, n_sel):
    kc, vc, ks, vs, kw, vw = _nsa_proj_parts(x_ref, g_ref, w_ref, cos_ref, sa_ref, sb_ref, q_ref, gate_ref)
    tm = kc.shape[0]
    for c, part in enumerate((kc, vc, ks, vs)):
        rows_t_ref[0, c * 256:(c + 1) * 256, :] = part.T
    win_t_ref[0, 0:256, :] = kw.T
    win_t_ref[0, 256:512, :] = vw.T
    vs_ref[0] = vs.T.reshape(NSA_KV_HEADS, HEAD_DIM, tm).astype(BF16)
    vw_ref[0] = vw.T.reshape(NSA_KV_HEADS, HEAD_DIM, tm).astype(BF16)
    t = (pl.program_id(0) % tiles_per_seq) * tm + lax.broadcasted_iota(jnp.int32, (tm, HEAD_DIM), 0)
    lane = lax.broadcasted_iota(jnp.int32, (tm, HEAD_DIM), 1)
    onehot = jnp.where(jnp.where(lane < n_sel, t // SEL_BLOCK, -1) == lane, 1.0, 0.0).astype(BF16)
    zeros = jnp.zeros((tm, HEAD_DIM), BF16)
    for g in range(NSA_KV_HEADS):
        sl = slice(g * HEAD_DIM, (g + 1) * HEAD_DIM)
        ks_ref[0, g] = jnp.concatenate([ks[:, sl].astype(BF16), onehot], axis=1)
        kw_ref[0, g] = jnp.concatenate([kw[:, sl].astype(BF16), zeros], axis=1)


def _nsa_proj_t(x, g, w, cos, sa, sb, b, t):
    m, d = x.shape
    nt = t // TM
    n_sel = -(-t // SEL_BLOCK)
    row = lambda i: (i, 0)
    tab = lambda i: (i % nt, 0)
    fm = lambda i: (i // nt, 0, i % nt)
    kmap = lambda i: (i // nt, 0, i % nt, 0)
    vmap = lambda i: (i // nt, 0, 0, i % nt)
    kv = NSA_KV_HEADS
    return pl.pallas_call(
        functools.partial(_nsa_proj_t_body, tiles_per_seq=nt, n_sel=n_sel),
        grid=(m // TM,),
        in_specs=[pl.BlockSpec((TM, d), row), _resident(g.shape), _resident(w.shape),
                  pl.BlockSpec((TM, LANES), tab), pl.BlockSpec((TM, LANES), tab), pl.BlockSpec((TM, LANES), tab)],
        out_specs=[pl.BlockSpec((TM, 1024), row), pl.BlockSpec((TM, LANES), row),
                   pl.BlockSpec((1, 1024, TM), fm), pl.BlockSpec((1, 512, TM), fm),
                   pl.BlockSpec((1, kv, TM, LANES), kmap), pl.BlockSpec((1, kv, HEAD_DIM, TM), vmap),
                   pl.BlockSpec((1, kv, TM, LANES), kmap), pl.BlockSpec((1, kv, HEAD_DIM, TM), vmap)],
        out_shape=[jax.ShapeDtypeStruct((m, 1024), BF16), jax.ShapeDtypeStruct((m, LANES), F32),
                   jax.ShapeDtypeStruct((b, 1024, t), F32), jax.ShapeDtypeStruct((b, 512, t), F32),
                   jax.ShapeDtypeStruct((b, kv, t, LANES), BF16), jax.ShapeDtypeStruct((b, kv, HEAD_DIM, t), BF16),
                   jax.ShapeDtypeStruct((b, kv, t, LANES), BF16), jax.ShapeDtypeStruct((b, kv, HEAD_DIM, t), BF16)],
        compiler_params=_cparams(("arbitrary",)),
        name="nsa_proj_t",
    )(x, g, w, cos, sa, sb)


def _nsa_proj(x, g, w, cos, sa, sb):
    m, d = x.shape
    ntab = cos.shape[0] // TM
    row = lambda i: (i, 0)
    tab = lambda i: (i % ntab, 0)
    return pl.pallas_call(
        _nsa_proj_body,
        grid=(m // TM,),
        in_specs=[pl.BlockSpec((TM, d), row), _resident(g.shape), _resident(w.shape),
                  pl.BlockSpec((TM, LANES), tab), pl.BlockSpec((TM, LANES), tab), pl.BlockSpec((TM, LANES), tab)],
        out_specs=[pl.BlockSpec((TM, 1024), row), pl.BlockSpec((TM, 1024), row),
                   pl.BlockSpec((TM, 512), row), pl.BlockSpec((TM, LANES), row)],
        out_shape=[jax.ShapeDtypeStruct((m, 1024), BF16), jax.ShapeDtypeStruct((m, 1024), F32),
                   jax.ShapeDtypeStruct((m, 512), F32), jax.ShapeDtypeStruct((m, LANES), F32)],
        compiler_params=_cparams(("arbitrary",)),
        name="nsa_proj",
    )(x, g, w, cos, sa, sb)


def _cmp_bias_body(pe_ref, w_ref, o_ref):
    for s in range(2):
        for half in range(2):
            acc = jnp.zeros((SUBLANES, 256), F32)
            for j in range(CMP_STRIDE):
                acc = acc + _dot(pe_ref[s, half, j].astype(BF16), w_ref[s, j, :, half * 256:(half + 1) * 256])
            o_ref[:, s * 512 + half * 256:s * 512 + (half + 1) * 256] = acc


def _cmp_bias(pe_t, wbd):
    return pl.pallas_call(_cmp_bias_body, out_shape=jax.ShapeDtypeStruct((SUBLANES, 1024), F32),
                          compiler_params=_cparams(None), name="cmp_bias")(pe_t, wbd)


CMP_HALF = CMP_ROWS // 2
CMP_PITCH = CMP_HALF + SUBLANES


def _compress_pages_body(x_ref, w_ref, b_ref, o_ref, xa_ref, xb_ref):
    per_page = LANES // CMP_STRIDE
    pages_per_block = x_ref.shape[3] // LANES
    pages_half = CMP_HALF // per_page

    def move(xs_ref, p0):
        for q in range(pages_half):
            blk, off = divmod(p0 + q, pages_per_block)
            for s in range(2):
                xt = x_ref[blk, s, :, off * LANES:(off + 1) * LANES].T
                for c in range(per_page):
                    for half in range(2):
                        xs_ref[2 * s + half, pl.ds(q * per_page + c, CMP_STRIDE, stride=CMP_PITCH), :] = (
                            xt[c * CMP_STRIDE:(c + 1) * CMP_STRIDE, half * LANES:(half + 1) * LANES])

    def project(xs_ref, r0):
        for s in range(2):
            acc = jnp.zeros((CMP_HALF, 512), F32)
            for j in range(CMP_STRIDE):
                xj = jnp.concatenate([xs_ref[2 * s, j * CMP_PITCH:j * CMP_PITCH + CMP_HALF, :],
                                      xs_ref[2 * s + 1, j * CMP_PITCH:j * CMP_PITCH + CMP_HALF, :]], axis=1)
                acc = acc + _dot(xj.astype(BF16), w_ref[s, j])
            o_ref[r0:r0 + CMP_HALF, s * 512:(s + 1) * 512] = acc + b_ref[0:1, s * 512:(s + 1) * 512]

    move(xa_ref, 0)
    move(xb_ref, pages_half)
    project(xa_ref, 0)
    project(xb_ref, CMP_HALF)


def _compress_pages(pages_t, wbd, bias):
    n, _, _, npos = pages_t.shape
    chunks = npos // CMP_STRIDE
    pp = CMP_ROWS // chunks
    return pl.pallas_call(
        _compress_pages_body,
        grid=(n // pp,),
        in_specs=[pl.BlockSpec((pp, 2, 256, npos), lambda i: (i, 0, 0, 0)),
                  _resident(wbd.shape), _resident(bias.shape)],
        out_specs=pl.BlockSpec((CMP_ROWS, 1024), lambda i: (i, 0)),
        out_shape=jax.ShapeDtypeStruct((n * chunks, 1024), F32),
        scratch_shapes=[pltpu.VMEM((4, CMP_STRIDE * CMP_PITCH, LANES), F32),
                        pltpu.VMEM((4, CMP_STRIDE * CMP_PITCH, LANES), F32)],
        compiler_params=_cparams(("arbitrary",)),
        name="compress_pages",
    )(pages_t, wbd, bias)


def _finish_compress(fs):
    n = fs.shape[0]
    kc = fs[:, 0:256] + pltpu.roll(fs[:, 256:512], n - 1, axis=0)
    vc = fs[:, 512:768] + pltpu.roll(fs[:, 768:1024], n - 1, axis=0)
    return kc.astype(BF16), vc.astype(BF16)


def _select_mask(sc_raw, jr, pos, n_real):
    sc = jnp.where(jr * SEL_BLOCK > pos, -FORCE, sc_raw)
    sc = jnp.where(jr == pos // SEL_BLOCK, FORCE, jnp.where(jr == 0, FORCE, sc))
    sc = jnp.where(jr >= n_real, -2.0 * FORCE, sc)
    cnt = jnp.zeros(sc.shape, F32)
    for i in range(n_real):
        ri = sc[i:i + 1, :]
        ge = jnp.where(ri >= sc, 1.0, 0.0)
        gt = jnp.where(ri > sc, 1.0, 0.0)
        cnt = cnt + jnp.where(jr > i, ge, gt)
    keep = jnp.where(sc > -0.5 * FORCE, 0.0, NEG)
    return jnp.where(cnt < float(min(TOP_N, n_real)), keep, NEG)


def _cmp_prompt_body(q_ref, fs_ref, ov_ref, oc_ref, qp_ref):
    i = pl.program_id(1)
    tq = q_ref.shape[0]
    nblk = fs_ref.shape[0]
    fs = fs_ref[...]
    kc = (fs[:, 0:256] + pltpu.roll(fs[:, 256:512], nblk - 1, axis=0)).astype(BF16)
    vc_t = (fs[:, 512:768] + pltpu.roll(fs[:, 768:1024], nblk - 1, axis=0)).T.astype(BF16)
    t_row = i * tq + lax.broadcasted_iota(jnp.int32, (nblk, tq), 1)
    n_col = lax.broadcasted_iota(jnp.int32, (nblk, tq), 0)
    vis_t = n_col * CMP_STRIDE + 2 * CMP_STRIDE - 1 <= t_row
    nsel = ov_ref.shape[0]
    jr = lax.broadcasted_iota(jnp.int32, (nsel, tq), 0)
    pos = i * tq + lax.broadcasted_iota(jnp.int32, (nsel, tq), 1)
    eye = jnp.where(lax.broadcasted_iota(jnp.int32, (tq, tq), 0) == lax.broadcasted_iota(jnp.int32, (tq, tq), 1),
                    1.0, 0.0).astype(BF16)
    for g in range(NSA_KV_HEADS):
        kg = kc[:, g * HEAD_DIM:(g + 1) * HEAD_DIM]
        vg_t = vc_t[g * HEAD_DIM:(g + 1) * HEAD_DIM, :]
        ps_t = jnp.zeros((nblk, tq), F32)
        qs = []
        for r in range(NSA_GROUP):
            hd = g * NSA_GROUP + r
            qh = q_ref[:, hd * HEAD_DIM:(hd + 1) * HEAD_DIM]
            qs.append(qh)
            p_t = _softmax_masked(_nt(kg, qh), vis_t, 0)
            oc_ref[0, hd * HEAD_DIM:(hd + 1) * HEAD_DIM, :] = _dot(vg_t, p_t.astype(BF16)).astype(oc_ref.dtype)
            ps_t = ps_t + p_t
        a_t = _select_mask(_split_dot(ov_ref[...], ps_t), jr, pos, nsel)
        a_t = jnp.concatenate([a_t, jnp.zeros((HEAD_DIM - nsel, tq), F32)], axis=0).astype(BF16)
        a = _nt(eye, a_t).astype(BF16)
        for r in range(NSA_GROUP):
            qp_ref[0, g * NSA_GROUP + r] = jnp.concatenate([qs[r], a], axis=1)


def _cmp_prompt(q, fs, ov_t, b, t):
    nq = t // TQ
    nblk = t // CMP_STRIDE
    return pl.pallas_call(
        _cmp_prompt_body,
        grid=(b, nq),
        in_specs=[pl.BlockSpec((TQ, 1024), lambda bi, i: (bi * nq + i, 0)),
                  pl.BlockSpec((nblk, 1024), lambda bi, i: (bi, 0)),
                  _resident(ov_t.shape)],
        out_specs=[pl.BlockSpec((1, 1024, TQ), lambda bi, i: (bi, 0, i)),
                   pl.BlockSpec((1, NSA_HEADS, TQ, LANES), lambda bi, i: (bi, 0, i, 0))],
        out_shape=[jax.ShapeDtypeStruct((b, 1024, t), BF16),
                   jax.ShapeDtypeStruct((b, NSA_HEADS, t, LANES), BF16)],
        compiler_params=_cparams(("arbitrary", "arbitrary")),
        name="cmp_prompt",
    )(q, fs, ov_t)


def _flash_body(qp_ref, kp_ref, vt_ref, o_ref, st_ref, acc_ref, *, window):
    i = pl.program_id(2)
    tq = qp_ref.shape[2]
    dq = (i * tq + lax.broadcasted_iota(jnp.int32, (tq, tq), 1)) - lax.broadcasted_iota(jnp.int32, (tq, tq), 0)

    def logits(c, mx, masked):
        off = pl.multiple_of(c * tq, tq)
        k = kp_ref[0, 0, pl.ds(off, tq), :]
        out = []
        for r in range(NSA_GROUP):
            s = _nt(k, qp_ref[0, r])
            if masked:
                d = dq - c * tq
                s = jnp.where(d >= 0, s, NEG)
                if window is not None:
                    s = jnp.where(d <= window, s, NEG)
            st_ref[r, pl.ds(off, tq), :] = s
            out.append(jnp.maximum(mx[r], jnp.max(s, axis=0, keepdims=True)))
        return tuple(out)

    def by_pairs(lo, hi, body, init):
        n = hi - lo

        def quad(j, cr):
            for u in range(4):
                cr = body(lo + 4 * j + u, cr)
            return cr

        carry = lax.fori_loop(0, n // 4, quad, init)
        c2 = lo + (n // 4) * 4
        carry = lax.cond(n % 4 >= 2, lambda cr: body(c2 + 1, body(c2, cr)), lambda cr: cr, carry)
        return lax.cond(n % 2 == 1, lambda cr: body(hi - 1, cr), lambda cr: cr, carry)

    mx = tuple(jnp.full((1, tq), NEG, F32) for _ in range(NSA_GROUP))
    if window is None:
        lo = 0
        mx = by_pairs(0, i, lambda c, m: logits(c, m, False), mx)
        mx = logits(i, mx, True)
    else:
        lo = jnp.maximum(i - window // tq, 0)
        mx = by_pairs(lo, i + 1, lambda c, m: logits(c, m, True), mx)

    acc_ref[...] = jnp.zeros(acc_ref.shape, F32)

    def accumulate(c, ls):
        off = pl.multiple_of(c * tq, tq)
        vt = vt_ref[0, 0, :, pl.ds(off, tq)]
        out = []
        for r in range(NSA_GROUP):
            p = jnp.exp2(st_ref[r, pl.ds(off, tq), :] - mx[r])
            out.append(ls[r] + jnp.sum(p, axis=0, keepdims=True))
            acc_ref[r] += _dot(vt, p.astype(BF16))
        return tuple(out)

    ls = by_pairs(lo, i + 1, accumulate, tuple(jnp.zeros((1, tq), F32) for _ in range(NSA_GROUP)))
    for r in range(NSA_GROUP):
        o_ref[0, r * HEAD_DIM:(r + 1) * HEAD_DIM, :] = (acc_ref[r] / ls[r]).astype(o_ref.dtype)


def _flash(qp, kp, vt, window):
    b, _, t, _ = qp.shape
    nq = t // TQ
    return pl.pallas_call(
        functools.partial(_flash_body, window=window),
        grid=(b, NSA_KV_HEADS, nq),
        in_specs=[pl.BlockSpec((1, NSA_GROUP, TQ, LANES), lambda bi, g, i: (bi, g, i, 0)),
                  pl.BlockSpec((1, 1, t, LANES), lambda bi, g, i: (bi, g, 0, 0)),
                  pl.BlockSpec((1, 1, HEAD_DIM, t), lambda bi, g, i: (bi, g, 0, 0))],
        out_specs=pl.BlockSpec((1, NSA_GROUP * HEAD_DIM, TQ), lambda bi, g, i: (bi, g, i)),
        out_shape=jax.ShapeDtypeStruct((b, NSA_HEADS * HEAD_DIM, t), BF16),
        scratch_shapes=[pltpu.VMEM((NSA_GROUP, t, TQ), F32), pltpu.VMEM((NSA_GROUP, HEAD_DIM, TQ), F32)],
        compiler_params=_cparams(("arbitrary", "arbitrary", "arbitrary")),
        name="flash_win" if window is not None else "flash_slc",
    )(qp, kp, vt)


def _diag_blocks(o_full, o_ref, s=0):
    rows = o_ref.shape[1] // NSA_KV_HEADS
    for g in range(NSA_KV_HEADS):
        o_ref[s, g * rows:(g + 1) * rows, :] = o_full[g * rows:(g + 1) * rows, g * HEAD_DIM:(g + 1) * HEAD_DIM]


def _cmp_sample_body(pt_ref, *refs, n_pages, past_len, dec, sb):
    fs_refs = refs[:sb * n_pages]
    qt_ref, ov_ref, msum_ref, oc_ref, at_ref = refs[sb * n_pages:]
    nq = qt_ref.shape[1]
    nblk = n_pages * fs_refs[0].shape[0]
    nj = ov_ref.shape[0]
    pos_l = past_len + (lax.broadcasted_iota(jnp.int32, (nblk, nq), 1) // NSA_GROUP) % dec
    n_c = lax.broadcasted_iota(jnp.int32, (nblk, nq), 0)
    vis_t = jnp.logical_and(n_c * CMP_STRIDE + 2 * CMP_STRIDE - 1 <= pos_l, n_c < nblk - 1)
    jr = lax.broadcasted_iota(jnp.int32, (nj, nq), 0)
    pos = past_len + (lax.broadcasted_iota(jnp.int32, (nj, nq), 1) // NSA_GROUP) % dec
    n_sel = -(-(past_len + dec) // SEL_BLOCK)
    for s in range(sb):
        fs = jnp.concatenate([r[...] for r in fs_refs[s * n_pages:(s + 1) * n_pages]], axis=0)
        kc, vc = _finish_compress(fs)
        qt = qt_ref[s]
        p_t = _softmax_masked(_nt(kc, qt), vis_t, 0)
        _diag_blocks(_tn(p_t.astype(BF16), vc), oc_ref, s)
        a1 = _split_dot(ov_ref[...], p_t)
        hi = a1.astype(BF16)
        lo = (a1 - hi.astype(F32)).astype(BF16)
        sc = _dot(hi, msum_ref[...]) + _dot(lo, msum_ref[...])
        a_t = _select_mask(sc, jr, pos, n_sel)
        a_t = jnp.concatenate([a_t, jnp.zeros((LANES - nj, nq), F32)], axis=0)
        at_ref[s] = a_t.T


def _cmp_sample(page_table, fs_phys, qt, ov_t, msum, past_len, dec):
    nb, n_pages = page_table.shape
    sb = CMP_SEQS
    fs_specs = [pl.BlockSpec((SUBLANES, 1024), functools.partial(lambda b, pt, s, k: (pt[b * sb + s, k], 0), s=s, k=k))
                for s in range(sb) for k in range(n_pages)]
    grid_spec = pltpu.PrefetchScalarGridSpec(
        num_scalar_prefetch=1,
        grid=(nb // sb,),
        in_specs=fs_specs + [pl.BlockSpec((sb, LANES, 256), lambda b, pt: (b, 0, 0)),
                             pl.BlockSpec(ov_t.shape, lambda b, pt: (0, 0)),
                             pl.BlockSpec(msum.shape, lambda b, pt: (0, 0))],
        out_specs=[pl.BlockSpec((sb, 64, HEAD_DIM), lambda b, pt: (b, 0, 0)),
                   pl.BlockSpec((sb, LANES, LANES), lambda b, pt: (b, 0, 0))],
    )
    return pl.pallas_call(
        functools.partial(_cmp_sample_body, n_pages=n_pages, past_len=past_len, dec=dec, sb=sb),
        grid_spec=grid_spec,
        out_shape=[jax.ShapeDtypeStruct((nb, 64, HEAD_DIM), F32),
                   jax.ShapeDtypeStruct((nb, LANES, LANES), F32)],
        compiler_params=_cparams(("arbitrary",)),
        name="cmp_sample",
    )(page_table, *([fs_phys] * (sb * n_pages)), qt, ov_t, msum)


def _attn_sample_body(*refs, n_blocks, sb, dec, has_pt, has_bm, has_pm, emit_state):
    refs = list(refs[1:] if has_pt else refs)
    n_kv = sb * n_blocks if has_pt else 1
    kv_refs = refs[:n_kv]
    rest = refs[n_kv:]
    new_ref, qt_ref = rest[0], rest[1]
    rest = rest[2:]
    bm_ref = rest.pop(0) if has_bm else None
    pm_ref = rest.pop(0) if has_pm else None
    nm_ref, o_ref = rest[0], rest[1]
    so_ref = rest[2] if emit_state else None
    st_ref = rest[-1]
    nq = qt_ref.shape[1]
    rows = kv_refs[0].shape[3]
    per = rows // SEL_BLOCK

    def kv(s, kb, which):
        return kv_refs[s * n_blocks + kb][0, which] if has_pt else kv_refs[0][s, which]

    if has_bm:
        blk_of_lane = lax.broadcasted_iota(jnp.int32, (nq, rows), 1) // SEL_BLOCK
    for s in range(sb):
        qt = qt_ref[s]
        m = jnp.full((nq, 1), NEG, F32)
        for kb in range(n_blocks):
            sc = _dot(qt, kv(s, kb, 0).astype(BF16))
            if has_pm:
                sc = sc + pm_ref[:, kb * rows:(kb + 1) * rows]
            if has_bm:
                add = bm_ref[s, :, kb * per:kb * per + 1]
                for u in range(1, per):
                    add = jnp.where(blk_of_lane == u, bm_ref[s, :, kb * per + u:kb * per + u + 1], add)
                sc = sc + add
            st_ref[s, :, kb * rows:(kb + 1) * rows] = sc
            m = jnp.maximum(m, jnp.max(sc, axis=-1, keepdims=True))
        s_new = _nt(qt, new_ref[s, :, 0:256].astype(BF16)) + nm_ref[...]
        if has_bm:
            s_new = s_new + bm_ref[s, :, n_blocks * per:n_blocks * per + 1]
        m = jnp.maximum(m, jnp.max(s_new, axis=-1, keepdims=True))
        e_new = jnp.exp2(s_new - m)
        l = jnp.sum(e_new, axis=-1, keepdims=True)
        acc = _dot(e_new.astype(BF16), new_ref[s, :, 256:512].astype(BF16))
        for kb in range(n_blocks):
            e = jnp.exp2(st_ref[s, :, kb * rows:(kb + 1) * rows] - m)
            l = l + jnp.sum(e, axis=-1, keepdims=True)
            acc = acc + _nt(e.astype(BF16), kv(s, kb, 1).astype(BF16))
        _diag_blocks(acc / jnp.maximum(l, 1e-30), o_ref, s)
        if emit_state:
            lane = lax.broadcasted_iota(jnp.int32, (256, LANES), 1)
            tail = pltpu.roll(new_ref[s], SUBLANES - dec, axis=0)
            for which in range(2):
                shifted = pltpu.roll(kv(s, 0, which), rows - dec, axis=1)
                fill = jnp.concatenate([jnp.zeros((LANES - SUBLANES, 256), F32),
                                        tail[:, which * 256:(which + 1) * 256]], axis=0).T
                last = jnp.where(lane < LANES - dec, shifted[:, rows - LANES:], fill)
                so_ref[s, which, :, 0:rows - LANES] = shifted[:, 0:rows - LANES]
                so_ref[s, which, :, rows - LANES:rows] = last


def _attn_sample(kv, slot_blk, n_blocks, new, qt, new_mask, sb, dec,
                 page_table=None, blk_mask=None, past_mask=None, emit_state=False):
    nb = qt.shape[0]
    kv_rows = kv.shape[3]
    has_pt = page_table is not None
    has_bm = blk_mask is not None
    has_pm = past_mask is not None
    if has_pt:
        kv_specs = [pl.BlockSpec((1, 2, 256, kv_rows),
                                 functools.partial(lambda b, pt, s, k: (pt[b * sb + s, k], slot_blk, 0, 0), s=s, k=k))
                    for s in range(sb) for k in range(n_blocks)]
        im = lambda b, pt: (b, 0, 0)
        im4 = lambda b, pt: (b, 0, 0, 0)
        cm = lambda b, pt: (0, 0)
    else:
        kv_specs = [pl.BlockSpec((sb, 2, 256, kv_rows), lambda b: (b, slot_blk, 0, 0))]
        im = lambda b: (b, 0, 0)
        im4 = lambda b: (b, 0, 0, 0)
        cm = lambda b: (0, 0)
    in_specs = kv_specs + [pl.BlockSpec((sb, SUBLANES, 512), im), pl.BlockSpec((sb, LANES, 256), im)]
    args = [kv] * len(kv_specs) + [new, qt]
    if has_bm:
        in_specs.append(pl.BlockSpec((sb,) + blk_mask.shape[1:], im))
        args.append(blk_mask)
    if has_pm:
        in_specs.append(pl.BlockSpec(past_mask.shape, cm))
        args.append(past_mask)
    in_specs.append(pl.BlockSpec(new_mask.shape, cm))
    args.append(new_mask)
    out_specs = [pl.BlockSpec((sb, 64, HEAD_DIM), im)]
    out_shape = [jax.ShapeDtypeStruct((nb, 64, HEAD_DIM), F32)]
    if emit_state:
        out_specs.append(pl.BlockSpec((sb, 2, 256, kv_rows), im4))
        out_shape.append(jax.ShapeDtypeStruct((nb, 2, 256, kv_rows), F32))
    scratch = [pltpu.VMEM((sb, LANES, n_blocks * kv_rows), F32)]
    body = functools.partial(_attn_sample_body, n_blocks=n_blocks, sb=sb, dec=dec, has_pt=has_pt, has_bm=has_bm,
                             has_pm=has_pm, emit_state=emit_state)
    if has_pt:
        gs = pltpu.PrefetchScalarGridSpec(num_scalar_prefetch=1, grid=(nb // sb,), in_specs=in_specs,
                                          out_specs=out_specs, scratch_shapes=scratch)
        return pl.pallas_call(body, grid_spec=gs, out_shape=out_shape, compiler_params=_cparams(("arbitrary",)),
                              name="slc_sample")(page_table, *args)
    return pl.pallas_call(body, grid=(nb // sb,), in_specs=in_specs, out_specs=out_specs, scratch_shapes=scratch,
                          out_shape=out_shape, compiler_params=_cparams(("arbitrary",)), name="win_sample")(*args)


def _out_proj_body(*refs, n_in, gated, feature_major):
    o_refs = refs[:n_in]
    rest = refs[n_in:]
    if gated:
        gate_ref, rest = rest[0], rest[1:]
    x_ref, w_ref, g_ref, y_ref = rest
    tm = x_ref.shape[0]
    if feature_major:
        gates_t = gate_ref[...].T
        heads = []
        for hd in range(NSA_HEADS):
            acc = jnp.zeros((HEAD_DIM, tm), F32)
            for kbr in range(n_in):
                row = gates_t[hd * 3 + kbr:hd * 3 + kbr + 1, :]
                acc = acc + row * o_refs[kbr][0, hd * HEAD_DIM:(hd + 1) * HEAD_DIM, :].astype(F32)
            heads.append(acc.astype(BF16))
        y = _tn(jnp.concatenate(heads, axis=0), w_ref[...])
        y_ref[...] = x_ref[...] + _rms(y, g_ref[...])
        return
    if gated:
        lane = lax.broadcasted_iota(jnp.int32, (tm, LANES), 1)
        gates = gate_ref[...]
        chunks = []
        for c in range(1024 // LANES):
            acc = jnp.zeros((tm, LANES), F32)
            for kbr in range(n_in):
                c0 = (2 * c) * 3 + kbr
                c1 = (2 * c + 1) * 3 + kbr
                gexp = jnp.where(lane < HEAD_DIM, gates[:, c0:c0 + 1], gates[:, c1:c1 + 1])
                acc = acc + gexp * o_refs[kbr][:, c * LANES:(c + 1) * LANES].astype(F32)
            chunks.append(acc.astype(BF16))
        o = jnp.concatenate(chunks, axis=1)
    else:
        o = o_refs[0][...].astype(BF16)
    y = _dot(o, w_ref[...])
    y_ref[...] = x_ref[...] + _rms(y, g_ref[...])


def _out_proj(os_, gates, x, w, g, seq_len=None):
    m, d = x.shape
    row = lambda i: (i, 0)
    gated = gates is not None
    feature_major = seq_len is not None
    if feature_major:
        nt = seq_len // TM
        in_specs = [pl.BlockSpec((1, 1024, TM), lambda i: (i // nt, 0, i % nt)) for _ in os_]
    else:
        in_specs = [pl.BlockSpec((TM, 1024), row) for _ in os_]
    args = list(os_)
    if gated:
        in_specs.append(pl.BlockSpec((TM, LANES), row))
        args.append(gates)
    in_specs += [pl.BlockSpec((TM, d), row), _resident(w.shape), _resident(g.shape)]
    args += [x, w, g]
    return pl.pallas_call(
        functools.partial(_out_proj_body, n_in=len(os_), gated=gated, feature_major=feature_major),
        grid=(m // TM,), in_specs=in_specs, out_specs=pl.BlockSpec((TM, d), row),
        out_shape=jax.ShapeDtypeStruct((m, d), F32),
        compiler_params=_cparams(("arbitrary",)), name="out_proj",
    )(*args)


def _ffn_body(*refs, sample, tiles_per_seq, nb):
    if sample:
        x_ref, g2_ref, g3_ref, wu_ref, cw_ref, cb_ref, wd_ref, buf_ref, y_ref, tail_ref = refs
    else:
        x_ref, g2_ref, g3_ref, wu_ref, cw_ref, cb_ref, wd_ref, y_ref, tail_ref, carry_ref = refs
    x = x_ref[...]
    tm = x.shape[0]
    f = wd_ref.shape[0]
    h = _rms(x, g2_ref[...]).astype(BF16)
    acc = jnp.zeros(x.shape, F32)
    if not sample:
        first = pl.program_id(0) % tiles_per_seq == 0
        row = lax.broadcasted_iota(jnp.int32, (tm, FFN_FCHUNK), 0)
    for fc in range(f // FFN_FCHUNK):
        sl = slice(fc * FFN_FCHUNK, (fc + 1) * FFN_FCHUNK)
        gate = _dot(h, wu_ref[:, sl])
        val = _dot(h, wu_ref[:, f + fc * FFN_FCHUNK:f + (fc + 1) * FFN_FCHUNK])
        if sample:
            prev1 = jnp.concatenate([buf_ref[nb:2 * nb, sl], gate[0:tm - nb]], axis=0)
            prev2 = jnp.concatenate([buf_ref[0:2 * nb, sl], gate[0:tm - 2 * nb]], axis=0)
            tail_ref[:, sl] = gate[tm - 2 * nb:tm]
        else:
            carry = carry_ref[:, sl]
            c6 = jnp.where(first, 0.0, carry[6:7, :])
            c7 = jnp.where(first, 0.0, carry[7:8, :])
            prev1 = jnp.where(row == 0, c7, pltpu.roll(gate, 1, axis=0))
            prev2 = jnp.where(row == 0, c6, jnp.where(row == 1, c7, pltpu.roll(gate, 2, axis=0)))
            carry_ref[:, sl] = gate[tm - SUBLANES:tm]
            tail_ref[:, sl] = gate[tm - SUBLANES:tm]
        conv = cb_ref[0:1, sl] + cw_ref[0:1, sl] * prev2 + cw_ref[1:2, sl] * prev1 + cw_ref[2:3, sl] * gate
        act = (jax.nn.gelu(conv) * val).astype(BF16)
        acc = acc + _dot(act, wd_ref[sl, :])
    y_ref[...] = x + _rms(acc, g3_ref[...])


def _ffn(x, g2, g3, wu, cw, cb, wd, seq_len=None, buf=None):
    m, d = x.shape
    f = wd.shape[0]
    sample = buf is not None
    row = lambda i: (i, 0)
    if sample:
        tm = m
        nb = buf.shape[0] // 2
        tail_shape, tail_block = (2 * nb, f), (2 * nb, f)
        scratch = []
    else:
        tm = FFN_TM
        nb = 0
        tail_shape, tail_block = (m // tm * SUBLANES, f), (SUBLANES, f)
        scratch = [pltpu.VMEM((SUBLANES, f), F32)]
    in_specs = [pl.BlockSpec((tm, d), row), _resident(g2.shape), _resident(g3.shape), _resident(wu.shape),
                _resident(cw.shape), _resident(cb.shape), _resident(wd.shape)]
    args = [x, g2, g3, wu, cw, cb, wd]
    if sample:
        in_specs.append(_resident(buf.shape))
        args.append(buf)
    return pl.pallas_call(
        functools.partial(_ffn_body, sample=sample, tiles_per_seq=(seq_len // tm if not sample else 1), nb=nb),
        grid=(m // tm,), in_specs=in_specs,
        out_specs=[pl.BlockSpec((tm, d), row), pl.BlockSpec(tail_block, row)],
        out_shape=[jax.ShapeDtypeStruct((m, d), F32), jax.ShapeDtypeStruct(tail_shape, F32)],
        scratch_shapes=scratch,
        compiler_params=_cparams(("arbitrary",)), name="ffn_sample" if sample else "ffn_prompt",
    )(*args)


def _gla_proj_body(x_ref, g_ref, w_ref, wg_ref, bg_ref, q_ref, k_ref, v_ref, sr_ref, la_ref):
    h = _rms(x_ref[...], g_ref[...]).astype(BF16)
    nk = GLA_HEADS * GLA_DK
    nv = GLA_HEADS * GLA_DV
    for c in range(nk // 256):
        q_ref[:, c * 256:(c + 1) * 256] = _dot(h, w_ref[:, c * 256:(c + 1) * 256]) * (GLA_DK ** -0.5)
        k_ref[:, c * 256:(c + 1) * 256] = _dot(h, w_ref[:, nk + c * 256:nk + (c + 1) * 256])
    for c in range(nv // 256):
        v_ref[:, c * 256:(c + 1) * 256] = _dot(h, w_ref[:, 2 * nk + c * 256:2 * nk + (c + 1) * 256]).astype(BF16)
        r = _dot(h, w_ref[:, 2 * nk + nv + c * 256:2 * nk + nv + (c + 1) * 256])
        sr_ref[:, c * 256:(c + 1) * 256] = jax.nn.silu(r).astype(BF16)
    low = _dot(h, w_ref[:, 2 * nk + 2 * nv:2 * nk + 2 * nv + LANES]).astype(BF16)
    gz = _dot(low, wg_ref[...]) + bg_ref[...]
    log_sig = jnp.minimum(gz, 0.0) - jnp.log1p(jnp.exp(-jnp.abs(gz)))
    la_ref[...] = log_sig / GLA_GATE_TEMP


def _gla_proj(x, g, w, wg, bg):
    m, d = x.shape
    nk = GLA_HEADS * GLA_DK
    nv = GLA_HEADS * GLA_DV
    row = lambda i: (i, 0)
    return pl.pallas_call(
        _gla_proj_body, grid=(m // TM,),
        in_specs=[pl.BlockSpec((TM, d), row), _resident(g.shape), _resident(w.shape),
                  _resident(wg.shape), _resident(bg.shape)],
        out_specs=[pl.BlockSpec((TM, nk), row), pl.BlockSpec((TM, nk), row), pl.BlockSpec((TM, nv), row),
                   pl.BlockSpec((TM, nv), row), pl.BlockSpec((TM, nk), row)],
        out_shape=[jax.ShapeDtypeStruct((m, nk), F32), jax.ShapeDtypeStruct((m, nk), F32),
                   jax.ShapeDtypeStruct((m, nv), BF16), jax.ShapeDtypeStruct((m, nv), BF16),
                   jax.ShapeDtypeStruct((m, nk), F32)],
        compiler_params=_cparams(("arbitrary",)), name="gla_proj",
    )(x, g, w, wg, bg)


def _cumsum_rows(x):
    n = x.shape[0]
    row = lax.broadcasted_iota(jnp.int32, x.shape, 0)
    sh = 1
    while sh < n:
        x = x + jnp.where(row >= sh, pltpu.roll(x, sh, axis=0), 0.0)
        sh *= 2
    return x


def _gla_chunk(q, k, la, v, s_old):
    cs = q.shape[0]
    row = lax.broadcasted_iota(jnp.int32, (cs, 1), 0)
    trow = lax.broadcasted_iota(jnp.int32, (cs, cs), 0)
    scol = lax.broadcasted_iota(jnp.int32, (cs, cs), 1)
    r8 = lax.broadcasted_iota(jnp.int32, (SUBLANES, 1), 0)
    lane8 = lax.broadcasted_iota(jnp.int32, (SUBLANES, cs), 1)
    if True:
        cum = _cumsum_rows(la)
        last = cum[cs - 1:cs, :]
        out = _dot((q * jnp.exp(cum)).astype(BF16), s_old.astype(BF16))
        att = None
        hh = cs // 2
        while hh >= SUBLANES:
            nblk = cs // (2 * hh)
            ref = jnp.concatenate(
                [jnp.broadcast_to(cum[u * 2 * hh + hh - 1:u * 2 * hh + hh, :], (2 * hh, GLA_DK)) for u in range(nblk)],
                axis=0) if nblk > 1 else jnp.broadcast_to(cum[hh - 1:hh, :], (cs, GLA_DK))
            second = (row % (2 * hh)) >= hh
            qh = jnp.where(second, q * jnp.exp(jnp.minimum(cum - ref, 0.0)), 0.0).astype(BF16)
            kh = jnp.where(second, 0.0, k * jnp.exp(jnp.minimum(ref - cum, 0.0))).astype(BF16)
            a = _nt(qh, kh)
            if nblk > 1:
                a = jnp.where(trow // (2 * hh) == scol // (2 * hh), a, 0.0)
            att = a if att is None else att + a
            hh //= 2
        slabs = []
        for g8 in range(cs // SUBLANES):
            r0 = g8 * SUBLANES
            cg, qg, kg = cum[r0:r0 + SUBLANES], q[r0:r0 + SUBLANES], k[r0:r0 + SUBLANES]
            slab = jnp.zeros((SUBLANES, GLA_DV if att is None else cs), F32)
            for s in range(SUBLANES):
                e = jnp.exp(jnp.minimum(cg - cg[s:s + 1, :], 0.0))
                col = jnp.sum(e * qg * kg[s:s + 1, :], axis=1, keepdims=True)
                col = jnp.where(r8 >= s, col, 0.0)
                if att is None:
                    slab = slab + col * v[r0 + s:r0 + s + 1, :].astype(F32)
                else:
                    slab = jnp.where(lane8 == r0 + s, col, slab)
            slabs.append(slab)
        if att is None:
            out = out + slabs[0]
        else:
            att = att + jnp.concatenate(slabs, axis=0)
            out = out + _dot(att.astype(BF16), v)
        kt = (k * jnp.exp(last - cum)).astype(BF16)
        dcol = jnp.broadcast_to(jnp.exp(last), (SUBLANES, GLA_DK)).T[:, 0:1]
        return out, dcol * s_old + _tn(kt, v)


def _gla_rec_prompt_body(q_ref, k_ref, la_ref, v_ref, sr_ref, gn_ref, o_ref, so_ref, st_ref):
    c = pl.program_id(1)

    @pl.when(c == 0)
    def _():
        st_ref[...] = jnp.zeros(st_ref.shape, F32)

    for hd in range(GLA_HEADS):
        ksl = slice(hd * GLA_DK, (hd + 1) * GLA_DK)
        vsl = slice(hd * GLA_DV, (hd + 1) * GLA_DV)
        out, s_new = _gla_chunk(q_ref[:, ksl], k_ref[:, ksl], la_ref[:, ksl], v_ref[:, vsl], st_ref[hd])
        st_ref[hd] = s_new
        o_ref[:, vsl] = (_rms(out, gn_ref[...]) * sr_ref[:, vsl].astype(F32)).astype(o_ref.dtype)

    @pl.when(c == pl.num_programs(1) - 1)
    def _():
        so_ref[0] = st_ref[...]


def _gla_rec_sample_body(q_ref, k_ref, la_ref, v_ref, sr_ref, gn_ref, s0_ref, o_ref, so_ref, *, sb):
    for s in range(sb):
        rsl = slice(s * SUBLANES, (s + 1) * SUBLANES)
        for hd in range(GLA_HEADS):
            ksl = slice(hd * GLA_DK, (hd + 1) * GLA_DK)
            vsl = slice(hd * GLA_DV, (hd + 1) * GLA_DV)
            out, s_new = _gla_chunk(q_ref[rsl, ksl], k_ref[rsl, ksl], la_ref[rsl, ksl], v_ref[rsl, vsl], s0_ref[s, hd])
            so_ref[s, hd] = s_new
            o_ref[rsl, vsl] = (_rms(out, gn_ref[...]) * sr_ref[rsl, vsl].astype(F32)).astype(o_ref.dtype)


def _gla_rec(q, k, la, v, sr, gn, nb, chunk, s0=None):
    m = q.shape[0]
    nk = GLA_HEADS * GLA_DK
    nv = GLA_HEADS * GLA_DV
    out_shape = [jax.ShapeDtypeStruct((m, nv), BF16), jax.ShapeDtypeStruct((nb, GLA_HEADS, GLA_DK, GLA_DV), F32)]
    if s0 is None:
        nc = m // nb // chunk
        row = lambda b, c: (b * nc + c, 0)
        st_spec = pl.BlockSpec((1, GLA_HEADS, GLA_DK, GLA_DV), lambda b, c: (b, 0, 0, 0))
        return pl.pallas_call(
            _gla_rec_prompt_body, grid=(nb, nc),
            in_specs=[pl.BlockSpec((chunk, nk), row), pl.BlockSpec((chunk, nk), row), pl.BlockSpec((chunk, nk), row),
                      pl.BlockSpec((chunk, nv), row), pl.BlockSpec((chunk, nv), row),
                      pl.BlockSpec(gn.shape, lambda b, c: (0, 0))],
            out_specs=[pl.BlockSpec((chunk, nv), row), st_spec], out_shape=out_shape,
            scratch_shapes=[pltpu.VMEM((GLA_HEADS, GLA_DK, GLA_DV), F32)],
            compiler_params=_cparams(("arbitrary", "arbitrary")), name="gla_rec_prompt",
        )(q, k, la, v, sr, gn)
    sb = GLA_SEQS
    rows = sb * chunk
    row = lambda b: (b, 0)
    st_spec = pl.BlockSpec((sb, GLA_HEADS, GLA_DK, GLA_DV), lambda b: (b, 0, 0, 0))
    return pl.pallas_call(
        functools.partial(_gla_rec_sample_body, sb=sb), grid=(nb // sb,),
        in_specs=[pl.BlockSpec((rows, nk), row), pl.BlockSpec((rows, nk), row), pl.BlockSpec((rows, nk), row),
                  pl.BlockSpec((rows, nv), row), pl.BlockSpec((rows, nv), row),
                  pl.BlockSpec(gn.shape, lambda b: (0, 0)), st_spec],
        out_specs=[pl.BlockSpec((rows, nv), row), st_spec], out_shape=out_shape,
        compiler_params=_cparams(("arbitrary",)), name="gla_rec_sample",
    )(q, k, la, v, sr, gn, s0)


def _rope_tables(pos):
    half = HEAD_DIM // 2
    inv = ROPE_THETA ** (-jnp.arange(half, dtype=F32) / half)
    ang = pos.astype(F32)[:, None] * inv[None, :]
    cos, sin = jnp.cos(ang), jnp.sin(ang)
    z = jnp.zeros_like(sin)
    cos_t = jnp.tile(cos, (1, 4))
    sa = jnp.tile(jnp.concatenate([-sin, z], axis=1), (1, 2))
    sb = jnp.tile(jnp.concatenate([z, sin], axis=1), (1, 2))
    return cos_t, sa, sb


def _pad_cols(w, n):
    return jnp.pad(w, ((0, 0), (0, n - w.shape[1])))


def _cmp_weights(cmp_pe, cmp_w):
    g = NSA_KV_HEADS
    w = cmp_w.reshape(2, 2, CMP_STRIDE, HEAD_DIM, HEAD_DIM)
    eye = jnp.eye(g, dtype=F32)
    wbd = jnp.einsum('shjde,gk->sjgdhke', w, eye)
    wbd = wbd.reshape(2, CMP_STRIDE, g * HEAD_DIM, 2 * g * HEAD_DIM).astype(BF16)
    pe = cmp_pe.reshape(2, 2, CMP_STRIDE, 1, 1, HEAD_DIM)
    pe_t = jnp.broadcast_to(pe, (2, 2, CMP_STRIDE, SUBLANES, g, HEAD_DIM)).reshape(2, 2, CMP_STRIDE, SUBLANES, g * HEAD_DIM)
    return wbd, pe_t


def _overlap_t(n_blk, n_sel, rows, cols):
    i = jnp.arange(cols)[None, :]
    j = jnp.arange(rows)[:, None]
    ov = (i * CMP_STRIDE + 2 * CMP_STRIDE > j * SEL_BLOCK) & (i * CMP_STRIDE < (j + 1) * SEL_BLOCK)
    ov = ov & (i < n_blk) & (j < n_sel)
    return ov.astype(BF16)


def _nsa_layer_prompt(x, g, w_in, wbd, cbias, w_o, tabs, b, t):
    q, gates, rows_t, win_t, k_slc, vt_slc, k_win, vt_win = _nsa_proj_t(x, g[0:1], w_in, *tabs, b, t)
    fs = _compress_pages(rows_t.reshape(b, 4, NSA_KV_HEADS * HEAD_DIM, t), wbd, cbias)
    n_blk = t // CMP_STRIDE - 1
    n_sel = -(-t // SEL_BLOCK)
    oc, qp = _cmp_prompt(q, fs, _overlap_t(n_blk, n_sel, n_sel, t // CMP_STRIDE), b, t)
    o_s = _flash(qp, k_slc, vt_slc, None)
    o_w = _flash(qp, k_win, vt_win, WINDOW)
    x = _out_proj([oc, o_s, o_w], gates, x, w_o, g[1:2], seq_len=t)
    return x, rows_t, win_t


def _to_rows(o, nb, dec):
    o = o.reshape(nb, NSA_KV_HEADS, dec, NSA_GROUP, HEAD_DIM).transpose(2, 0, 1, 3, 4)
    return o.reshape(dec * nb, NSA_HEADS * HEAD_DIM).astype(BF16)


def _nsa_layer_sample(x, g, w_in, wbd, cbias, w_o, tabs, cache, page_table, win_state, nb, dec):
    n_phys, page = cache.shape[0], cache.shape[1]
    n_pages = page_table.shape[1]
    past_len = n_pages * page
    q, rows, win, gates = _nsa_proj(x, g[0:1], w_in, *tabs)
    pages_t = cache.transpose(0, 2, 3, 4, 1).reshape(n_phys, 4, NSA_KV_HEADS * HEAD_DIM, page)
    wb = win_state.shape[1]
    win_t = win_state.transpose(0, 2, 3, 4, 1).reshape(nb, 2, NSA_KV_HEADS * HEAD_DIM, wb)
    fs_phys = _compress_pages(pages_t, wbd, cbias)
    q5 = q.reshape(dec, nb, NSA_KV_HEADS, NSA_GROUP, HEAD_DIM).transpose(1, 2, 0, 3, 4)
    eye = jnp.eye(NSA_KV_HEADS, dtype=BF16)
    qt = q5.reshape(nb, NSA_KV_HEADS, dec * NSA_GROUP, 1, HEAD_DIM) * eye[None, :, None, :, None]
    qt = qt.reshape(nb, NSA_KV_HEADS * dec * NSA_GROUP, NSA_KV_HEADS * HEAD_DIM)
    qt = jnp.pad(qt, ((0, 0), (0, LANES - qt.shape[1]), (0, 0)))
    n_blk = (past_len + dec) // CMP_STRIDE - 1
    n_sel = -(-(past_len + dec) // SEL_BLOCK)
    nj = -(-n_sel // SUBLANES) * SUBLANES
    lane = jnp.arange(LANES)
    msum = (lane[:, None] // NSA_GROUP == lane[None, :] // NSA_GROUP).astype(BF16)
    oc, a_t = _cmp_sample(page_table, fs_phys, qt, _overlap_t(n_blk, n_sel, nj, past_len // CMP_STRIDE), msum, past_len, dec)
    t_row = ((lane // NSA_GROUP) % dec)[:, None]
    j8 = jnp.arange(SUBLANES)[None, :]
    new_mask = jnp.where((j8 < dec) & (j8 <= t_row), 0.0, NEG).astype(F32)
    win_mask = jnp.where(jnp.arange(wb)[None, :] >= t_row + wb - WINDOW, 0.0, NEG).astype(F32)

    def new_rows(a):
        a = a.reshape(dec, nb, 512).transpose(1, 0, 2)
        return jnp.pad(a, ((0, 0), (0, SUBLANES - dec), (0, 0)))

    (o_s,) = _attn_sample(pages_t, 1, n_pages, new_rows(rows[:, 512:1024]), qt, new_mask, SLC_SEQS, dec,
                          page_table=page_table, blk_mask=a_t)
    o_w, win_next = _attn_sample(win_t, 0, 1, new_rows(win), qt, new_mask, WIN_SEQS, dec,
                                 past_mask=win_mask, emit_state=True)
    x = _out_proj([_to_rows(oc, nb, dec), _to_rows(o_s, nb, dec), _to_rows(o_w, nb, dec)], gates, x, w_o, g[1:2])
    win_next = win_next.reshape(nb, 2, NSA_KV_HEADS, HEAD_DIM, wb).transpose(0, 4, 1, 2, 3)
    return x, rows, win_next


def _gla_layer(x, g, w_in, wg, bg, gn, w_o, nb, seq, s0=None):
    q, k, v, sr, la = _gla_proj(x, g[0:1], w_in, wg, bg)
    if s0 is None:
        o, s = _gla_rec(q, k, la, v, sr, gn, nb, GLA_CHUNK)
    else:
        def seqs(a):
            a = a.reshape(seq, nb, a.shape[1]).transpose(1, 0, 2)
            return jnp.pad(a, ((0, 0), (0, SUBLANES - seq), (0, 0))).reshape(nb * SUBLANES, a.shape[2])
        o, s = _gla_rec(seqs(q), seqs(k), seqs(la), seqs(v), seqs(sr), gn, nb, SUBLANES, s0=s0)
        o = o.reshape(nb, SUBLANES, o.shape[1])[:, :seq].transpose(1, 0, 2).reshape(seq * nb, o.shape[1])
    x = _out_proj([o], None, x, w_o, g[1:2])
    return x, s


def kernel(x_prompt, x_sample, cache_nsa_kv, state_win_kv, state_gla, state_ffn_conv, page_table, norm_gain, nsa_w_in, nsa_cmp_pe, nsa_cmp_w, nsa_w_o, gla_w_in, gla_w_gate_up, gla_b_gate, gla_norm_gain, gla_w_o, ffn_w_up, ffn_conv_w, ffn_conv_b, ffn_w_down):
    b, t, d = x_prompt.shape
    nb, dec, _ = x_sample.shape
    depth = norm_gain.shape[0]
    f = ffn_w_down.shape[1]
    past_len = page_table.shape[1] * cache_nsa_kv.shape[2]

    xp = x_prompt.reshape(b * t, d)
    xs = x_sample.transpose(1, 0, 2).reshape(dec * nb, d)
    tabs_p = _rope_tables(jnp.arange(t, dtype=jnp.int32))
    tabs_s = _rope_tables(past_len + jnp.repeat(jnp.arange(dec, dtype=jnp.int32), nb))

    nsa_p, nsa_s, win_p, win_s, gla_p, gla_s, ffn_p, ffn_s = [], [], [], [], [], [], [], []
    for i in range(depth):
        g = norm_gain[i]
        a = i // 2
        if i % 2 == 0:
            w_in = _pad_cols(nsa_w_in[a], 2688).astype(BF16)
            w_o = nsa_w_o[a].astype(BF16)
            wbd, pe_t = _cmp_weights(nsa_cmp_pe[a], nsa_cmp_w[a])
            cbias = _cmp_bias(pe_t, wbd)
            xp, rows_t, win_t = _nsa_layer_prompt(xp, g, w_in, wbd, cbias, w_o, tabs_p, b, t)
            nsa_p.append(rows_t.reshape(b, 4, NSA_KV_HEADS, HEAD_DIM, t).transpose(0, 4, 1, 2, 3))
            nw = min(WINDOW, t)
            win_p.append(win_t.reshape(b, 2, NSA_KV_HEADS, HEAD_DIM, t)[..., t - nw:].transpose(0, 4, 1, 2, 3))
            xs, rows, win_next = _nsa_layer_sample(xs, g, w_in, wbd, cbias, w_o, tabs_s, cache_nsa_kv[a], page_table,
                                                   state_win_kv[a], nb, dec)
            nsa_s.append(rows.reshape(dec, nb, 4, NSA_KV_HEADS, HEAD_DIM).transpose(1, 0, 2, 3, 4))
            win_s.append(win_next)
        else:
            w_in = _pad_cols(gla_w_in[a], 3200).astype(BF16)
            wg = jnp.pad(gla_w_gate_up[a], ((0, LANES - GLA_GATE_RANK), (0, 0))).astype(BF16)
            bg = gla_b_gate[a][None, :]
            gn = gla_norm_gain[a][None, :]
            w_o = gla_w_o[a].astype(BF16)
            xp, s = _gla_layer(xp, g, w_in, wg, bg, gn, w_o, b, t)
            gla_p.append(s)
            xs, s = _gla_layer(xs, g, w_in, wg, bg, gn, w_o, nb, dec, s0=state_gla[a])
            gla_s.append(s)
        wu = ffn_w_up[i].astype(BF16)
        wd = ffn_w_down[i].astype(BF16)
        cw = jnp.pad(ffn_conv_w[i], ((0, SUBLANES - ffn_conv_w.shape[1]), (0, 0)))
        cb = ffn_conv_b[i][None, :]
        xp, tail = _ffn(xp, g[2:3], g[3:4], wu, cw, cb, wd, seq_len=t)
        tail = tail.reshape(b, t // FFN_TM, SUBLANES, f)[:, -1, SUBLANES - 2:]
        ffn_p.append(tail)
        buf = state_ffn_conv[i].transpose(1, 0, 2).reshape(2 * nb, f)
        xs, tail = _ffn(xs, g[2:3], g[3:4], wu, cw, cb, wd, buf=buf)
        ffn_s.append(tail.reshape(2, nb, f).transpose(1, 0, 2))

    y_prompt = xp.reshape(b, t, d)
    y_sample = xs.reshape(dec, nb, d).transpose(1, 0, 2)
    return (y_prompt, y_sample, jnp.stack(nsa_p), jnp.stack(nsa_s), jnp.stack(win_p), jnp.stack(win_s),
            jnp.stack(gla_p), jnp.stack(gla_s), jnp.stack(ffn_p), jnp.stack(ffn_s))
```

```python
import functools
import math

import jax
import jax.numpy as jnp
from jax import lax
from jax.experimental import pallas as pl
from jax.experimental.pallas import tpu as pltpu

F32, BF16 = jnp.float32, jnp.bfloat16

HEAD_DIM = 64
NSA_HEADS = 16
NSA_KV_HEADS = 4
NSA_GROUP = 4
CMP_STRIDE = 16
SEL_BLOCK = 64
TOP_N = 16
WINDOW = 512
ROPE_THETA = 10000.0
GLA_HEADS = 4
GLA_DK = 128
GLA_DV = 256
GLA_GATE_RANK = 16
GLA_GATE_TEMP = 16.0
NORM_EPS = 1e-6
FORCE = 1e6
NEG = -1e30

LANES = 128
SUBLANES = 8
VMEM_LIMIT = 48 * 1024 * 1024

TM = 256
TQ = 256
CMP_ROWS = 256
GLA_CHUNK = 128
SLC_SEQS = 2
WIN_SEQS = 4
CMP_SEQS = 4
GLA_SEQS = 4
FFN_FCHUNK = 1408
FFN_TM = 512


def _cparams(sem):
    return pltpu.CompilerParams(dimension_semantics=sem, vmem_limit_bytes=VMEM_LIMIT)


def _resident(shape):
    nd = len(shape)
    return pl.BlockSpec(shape, lambda *_: (0,) * nd, pipeline_mode=pl.Buffered(1))


def _rms(x, g):
    return x * lax.rsqrt(jnp.mean(x * x, axis=-1, keepdims=True) + NORM_EPS) * g


def _nt(a, b):
    return lax.dot_general(a, b, (((1,), (1,)), ((), ())), preferred_element_type=F32)


def _tn(a, b):
    return lax.dot_general(a, b, (((0,), (0,)), ((), ())), preferred_element_type=F32)


def _dot(a, b):
    return jnp.dot(a, b, preferred_element_type=F32)


def _split_dot(w, x):
    hi = x.astype(BF16)
    lo = (x - hi.astype(F32)).astype(BF16)
    return _dot(w, hi) + _dot(w, lo)


def _softmax_masked(s, mask, axis):
    s = jnp.where(mask, s, NEG)
    m = jnp.max(s, axis=axis, keepdims=True)
    e = jnp.where(mask, jnp.exp2(s - m), 0.0)
    return e / jnp.maximum(jnp.sum(e, axis=axis, keepdims=True), 1e-30)


Q_SCALE = HEAD_DIM ** -0.5 * math.log2(math.e)


def _nsa_proj_parts(x_ref, g_ref, w_ref, cos_ref, sa_ref, sb_ref, q_ref, gate_ref):
    h = _rms(x_ref[...], g_ref[...]).astype(BF16)
    cos, sa, sb = cos_ref[...], sa_ref[...], sb_ref[...]

    def rope(z):
        return z * cos + pltpu.roll(z, 96, axis=1) * sa + pltpu.roll(z, 32, axis=1) * sb

    def proj(c0):
        return _dot(h, w_ref[:, c0:c0 + 256])

    def rope256(z):
        return jnp.concatenate([rope(z[:, :LANES]), rope(z[:, LANES:])], axis=1)

    for c in range(4):
        q_ref[:, c * 256:(c + 1) * 256] = (rope256(proj(c * 256)) * Q_SCALE).astype(BF16)
    gate_ref[...] = jax.nn.sigmoid(_dot(h, w_ref[:, 2560:2688]))
    return (rope256(proj(1024)), proj(1280), rope256(proj(1536)), proj(1792), rope256(proj(2048)), proj(2304))


def _nsa_proj_body(x_ref, g_ref, w_ref, cos_ref, sa_ref, sb_ref, q_ref, rows_ref, win_ref, gate_ref):
    parts = _nsa_proj_parts(x_ref, g_ref, w_ref, cos_ref, sa_ref, sb_ref, q_ref, gate_ref)
    for c in range(4):
        rows_ref[:, c * 256:(c + 1) * 256] = parts[c]
    win_ref[:, 0:256] = parts[4]
    win_ref[:, 256:512] = parts[5]


def _nsa_proj_t_body(x_ref, g_ref, w_ref, cos_ref, sa_ref, sb_ref, q_ref, gate_ref,
                     rows_t_ref, win_t_ref, ks_ref, vs_ref, kw_ref, vw_ref, *, tiles_per_seq, n_sel):
    kc, vc, ks, vs, kw, vw = _nsa_proj_parts(x_ref, g_ref, w_ref, cos_ref, sa_ref, sb_ref, q_ref, gate_ref)
    tm = kc.shape[0]
    for c, part in enumerate((kc, vc, ks, vs)):
        rows_t_ref[0, c * 256:(c + 1) * 256, :] = part.T
    win_t_ref[0, 0:256, :] = kw.T
    win_t_ref[0, 256:512, :] = vw.T
    vs_ref[0] = vs.T.reshape(NSA_KV_HEADS, HEAD_DIM, tm).astype(BF16)
    vw_ref[0] = vw.T.reshape(NSA_KV_HEADS, HEAD_DIM, tm).astype(BF16)
    t = (pl.program_id(0) % tiles_per_seq) * tm + lax.broadcasted_iota(jnp.int32, (tm, HEAD_DIM), 0)
    lane = lax.broadcasted_iota(jnp.int32, (tm, HEAD_DIM), 1)
    onehot = jnp.where(jnp.where(lane < n_sel, t // SEL_BLOCK, -1) == lane, 1.0, 0.0).astype(BF16)
    zeros = jnp.zeros((tm, HEAD_DIM), BF16)
    for g in range(NSA_KV_HEADS):
        sl = slice(g * HEAD_DIM, (g + 1) * HEAD_DIM)
        ks_ref[0, g] = jnp.concatenate([ks[:, sl].astype(BF16), onehot], axis=1)
        kw_ref[0, g] = jnp.concatenate([kw[:, sl].astype(BF16), zeros], axis=1)


def _nsa_proj_t(x, g, w, cos, sa, sb, b, t):
    m, d = x.shape
    nt = t // TM
    n_sel = -(-t // SEL_BLOCK)
    row = lambda i: (i, 0)
    tab = lambda i: (i % nt, 0)
    fm = lambda i: (i // nt, 0, i % nt)
    kmap = lambda i: (i // nt, 0, i % nt, 0)
    vmap = lambda i: (i // nt, 0, 0, i % nt)
    kv = NSA_KV_HEADS
    return pl.pallas_call(
        functools.partial(_nsa_proj_t_body, tiles_per_seq=nt, n_sel=n_sel),
        grid=(m // TM,),
        in_specs=[pl.BlockSpec((TM, d), row), _resident(g.shape), _resident(w.shape),
                  pl.BlockSpec((TM, LANES), tab), pl.BlockSpec((TM, LANES), tab), pl.BlockSpec((TM, LANES), tab)],
        out_specs=[pl.BlockSpec((TM, 1024), row), pl.BlockSpec((TM, LANES), row),
                   pl.BlockSpec((1, 1024, TM), fm), pl.BlockSpec((1, 512, TM), fm),
                   pl.BlockSpec((1, kv, TM, LANES), kmap), pl.BlockSpec((1, kv, HEAD_DIM, TM), vmap),
                   pl.BlockSpec((1, kv, TM, LANES), kmap), pl.BlockSpec((1, kv, HEAD_DIM, TM), vmap)],
        out_shape=[jax.ShapeDtypeStruct((m, 1024), BF16), jax.ShapeDtypeStruct((m, LANES), F32),
                   jax.ShapeDtypeStruct((b, 1024, t), F32), jax.ShapeDtypeStruct((b, 512, t), F32),
                   jax.ShapeDtypeStruct((b, kv, t, LANES), BF16), jax.ShapeDtypeStruct((b, kv, HEAD_DIM, t), BF16),
                   jax.ShapeDtypeStruct((b, kv, t, LANES), BF16), jax.ShapeDtypeStruct((b, kv, HEAD_DIM, t), BF16)],
        compiler_params=_cparams(("arbitrary",)),
        name="nsa_proj_t",
    )(x, g, w, cos, sa, sb)


def _nsa_proj(x, g, w, cos, sa, sb):
    m, d = x.shape
    ntab = cos.shape[0] // TM
    row = lambda i: (i, 0)
    tab = lambda i: (i % ntab, 0)
    return pl.pallas_call(
        _nsa_proj_body,
        grid=(m // TM,),
        in_specs=[pl.BlockSpec((TM, d), row), _resident(g.shape), _resident(w.shape),
                  pl.BlockSpec((TM, LANES), tab), pl.BlockSpec((TM, LANES), tab), pl.BlockSpec((TM, LANES), tab)],
        out_specs=[pl.BlockSpec((TM, 1024), row), pl.BlockSpec((TM, 1024), row),
                   pl.BlockSpec((TM, 512), row), pl.BlockSpec((TM, LANES), row)],
        out_shape=[jax.ShapeDtypeStruct((m, 1024), BF16), jax.ShapeDtypeStruct((m, 1024), F32),
                   jax.ShapeDtypeStruct((m, 512), F32), jax.ShapeDtypeStruct((m, LANES), F32)],
        compiler_params=_cparams(("arbitrary",)),
        name="nsa_proj",
    )(x, g, w, cos, sa, sb)


def _cmp_bias_body(pe_ref, w_ref, o_ref):
    for s in range(2):
        for half in range(2):
            acc = jnp.zeros((SUBLANES, 256), F32)
            for j in range(CMP_STRIDE):
                acc = acc + _dot(pe_ref[s, half, j].astype(BF16), w_ref[s, j, :, half * 256:(half + 1) * 256])
            o_ref[:, s * 512 + half * 256:s * 512 + (half + 1) * 256] = acc


def _cmp_bias(pe_t, wbd):
    return pl.pallas_call(_cmp_bias_body, out_shape=jax.ShapeDtypeStruct((SUBLANES, 1024), F32),
                          compiler_params=_cparams(None), name="cmp_bias")(pe_t, wbd)


CMP_HALF = CMP_ROWS // 2
CMP_PITCH = CMP_HALF + SUBLANES


def _compress_pages_body(x_ref, w_ref, b_ref, o_ref, xa_ref, xb_ref):
    per_page = LANES // CMP_STRIDE
    pages_per_block = x_ref.shape[3] // LANES
    pages_half = CMP_HALF // per_page

    def move(xs_ref, p0):
        for q in range(pages_half):
            blk, off = divmod(p0 + q, pages_per_block)
            for s in range(2):
                xt = x_ref[blk, s, :, off * LANES:(off + 1) * LANES].T
                for c in range(per_page):
                    for half in range(2):
                        xs_ref[2 * s + half, pl.ds(q * per_page + c, CMP_STRIDE, stride=CMP_PITCH), :] = (
                            xt[c * CMP_STRIDE:(c + 1) * CMP_STRIDE, half * LANES:(half + 1) * LANES])

    def project(xs_ref, r0):
        for s in range(2):
            acc = jnp.zeros((CMP_HALF, 512), F32)
            for j in range(CMP_STRIDE):
                xj = jnp.concatenate([xs_ref[2 * s, j * CMP_PITCH:j * CMP_PITCH + CMP_HALF, :],
                                      xs_ref[2 * s + 1, j * CMP_PITCH:j * CMP_PITCH + CMP_HALF, :]], axis=1)
                acc = acc + _dot(xj.astype(BF16), w_ref[s, j])
            o_ref[r0:r0 + CMP_HALF, s * 512:(s + 1) * 512] = acc + b_ref[0:1, s * 512:(s + 1) * 512]

    move(xa_ref, 0)
    move(xb_ref, pages_half)
    project(xa_ref, 0)
    project(xb_ref, CMP_HALF)


def _compress_pages(pages_t, wbd, bias):
    n, _, _, npos = pages_t.shape
    chunks = npos // CMP_STRIDE
    pp = CMP_ROWS // chunks
    return pl.pallas_call(
        _compress_pages_body,
        grid=(n // pp,),
        in_specs=[pl.BlockSpec((pp, 2, 256, npos), lambda i: (i, 0, 0, 0)),
                  _resident(wbd.shape), _resident(bias.shape)],
        out_specs=pl.BlockSpec((CMP_ROWS, 1024), lambda i: (i, 0)),
        out_shape=jax.ShapeDtypeStruct((n * chunks, 1024), F32),
        scratch_shapes=[pltpu.VMEM((4, CMP_STRIDE * CMP_PITCH, LANES), F32),
                        pltpu.VMEM((4, CMP_STRIDE * CMP_PITCH, LANES), F32)],
        compiler_params=_cparams(("arbitrary",)),
        name="compress_pages",
    )(pages_t, wbd, bias)


def _finish_compress(fs):
    n = fs.shape[0]
    kc = fs[:, 0:256] + pltpu.roll(fs[:, 256:512], n - 1, axis=0)
    vc = fs[:, 512:768] + pltpu.roll(fs[:, 768:1024], n - 1, axis=0)
    return kc.astype(BF16), vc.astype(BF16)


def _select_mask(sc_raw, jr, pos, n_real):
    sc = jnp.where(jr * SEL_BLOCK > pos, -FORCE, sc_raw)
    sc = jnp.where(jr == pos // SEL_BLOCK, FORCE, jnp.where(jr == 0, FORCE, sc))
    sc = jnp.where(jr >= n_real, -2.0 * FORCE, sc)
    cnt = jnp.zeros(sc.shape, F32)
    for i in range(n_real):
        ri = sc[i:i + 1, :]
        ge = jnp.where(ri >= sc, 1.0, 0.0)
        gt = jnp.where(ri > sc, 1.0, 0.0)
        cnt = cnt + jnp.where(jr > i, ge, gt)
    keep = jnp.where(sc > -0.5 * FORCE, 0.0, NEG)
    return jnp.where(cnt < float(min(TOP_N, n_real)), keep, NEG)


def _cmp_prompt_body(q_ref, fs_ref, ov_ref, oc_ref, qp_ref):
    i = pl.program_id(1)
    tq = q_ref.shape[0]
    nblk = fs_ref.shape[0]
    fs = fs_ref[...]
    kc = (fs[:, 0:256] + pltpu.roll(fs[:, 256:512], nblk - 1, axis=0)).astype(BF16)
    vc_t = (fs[:, 512:768] + pltpu.roll(fs[:, 768:1024], nblk - 1, axis=0)).T.astype(BF16)
    t_row = i * tq + lax.broadcasted_iota(jnp.int32, (nblk, tq), 1)
    n_col = lax.broadcasted_iota(jnp.int32, (nblk, tq), 0)
    vis_t = n_col * CMP_STRIDE + 2 * CMP_STRIDE - 1 <= t_row
    nsel = ov_ref.shape[0]
    jr = lax.broadcasted_iota(jnp.int32, (nsel, tq), 0)
    pos = i * tq + lax.broadcasted_iota(jnp.int32, (nsel, tq), 1)
    eye = jnp.where(lax.broadcasted_iota(jnp.int32, (tq, tq), 0) == lax.broadcasted_iota(jnp.int32, (tq, tq), 1),
                    1.0, 0.0).astype(BF16)
    for g in range(NSA_KV_HEADS):
        kg = kc[:, g * HEAD_DIM:(g + 1) * HEAD_DIM]
        vg_t = vc_t[g * HEAD_DIM:(g + 1) * HEAD_DIM, :]
        ps_t = jnp.zeros((nblk, tq), F32)
        qs = []
        for r in range(NSA_GROUP):
            hd = g * NSA_GROUP + r
            qh = q_ref[:, hd * HEAD_DIM:(hd + 1) * HEAD_DIM]
            qs.append(qh)
            p_t = _softmax_masked(_nt(kg, qh), vis_t, 0)
            oc_ref[0, hd * HEAD_DIM:(hd + 1) * HEAD_DIM, :] = _dot(vg_t, p_t.astype(BF16)).astype(oc_ref.dtype)
            ps_t = ps_t + p_t
        a_t = _select_mask(_split_dot(ov_ref[...], ps_t), jr, pos, nsel)
        a_t = jnp.concatenate([a_t, jnp.zeros((HEAD_DIM - nsel, tq), F32)], axis=0).astype(BF16)
        a = _nt(eye, a_t).astype(BF16)
        for r in range(NSA_GROUP):
            qp_ref[0, g * NSA_GROUP + r] = jnp.concatenate([qs[r], a], axis=1)


def _cmp_prompt(q, fs, ov_t, b, t):
    nq = t // TQ
    nblk = t // CMP_STRIDE
    return pl.pallas_call(
        _cmp_prompt_body,
        grid=(b, nq),
        in_specs=[pl.BlockSpec((TQ, 1024), lambda bi, i: (bi * nq + i, 0)),
                  pl.BlockSpec((nblk, 1024), lambda bi, i: (bi, 0)),
                  _resident(ov_t.shape)],
        out_specs=[pl.BlockSpec((1, 1024, TQ), lambda bi, i: (bi, 0, i)),
                   pl.BlockSpec((1, NSA_HEADS, TQ, LANES), lambda bi, i: (bi, 0, i, 0))],
        out_shape=[jax.ShapeDtypeStruct((b, 1024, t), BF16),
                   jax.ShapeDtypeStruct((b, NSA_HEADS, t, LANES), BF16)],
        compiler_params=_cparams(("arbitrary", "arbitrary")),
        name="cmp_prompt",
    )(q, fs, ov_t)


def _flash_body(qp_ref, kp_ref, vt_ref, o_ref, st_ref, acc_ref, *, window):
    i = pl.program_id(2)
    tq = qp_ref.shape[2]
    dq = (i * tq + lax.broadcasted_iota(jnp.int32, (tq, tq), 1)) - lax.broadcasted_iota(jnp.int32, (tq, tq), 0)

    def logits(c, mx, causal, far):
        off = pl.multiple_of(c * tq, tq)
        k = kp_ref[0, 0, pl.ds(off, tq), :]
        d = dq - c * tq
        out = []
        for r in range(NSA_GROUP):
            s = _nt(k, qp_ref[0, r])
            if causal:
                s = jnp.where(d >= 0, s, NEG)
            if far:
                s = jnp.where(d <= window, s, NEG)
            st_ref[r, pl.ds(off, tq), :] = s
            out.append(jnp.maximum(mx[r], jnp.max(s, axis=0, keepdims=True)))
        return tuple(out)

    def by_pairs(lo, hi, body, init):
        n = hi - lo

        def quad(j, cr):
            for u in range(4):
                cr = body(lo + 4 * j + u, cr)
            return cr

        carry = lax.fori_loop(0, n // 4, quad, init)
        c2 = lo + (n // 4) * 4
        carry = lax.cond(n % 4 >= 2, lambda cr: body(c2 + 1, body(c2, cr)), lambda cr: cr, carry)
        return lax.cond(n % 2 == 1, lambda cr: body(hi - 1, cr), lambda cr: cr, carry)

    mx = tuple(jnp.full((1, tq), NEG, F32) for _ in range(NSA_GROUP))
    back = 0 if window is None else window // tq
    if window is None:
        lo = 0
        mx = by_pairs(0, i + 1, lambda c, m: logits(c, m, True, False), mx)
    else:
        lo = jnp.maximum(i - back, 0)

        def steady(m):
            m = logits(i - back, m, False, True)
            for u in range(back - 1, 0, -1):
                m = logits(i - u, m, False, False)
            return logits(i, m, True, False)

        mx = lax.cond(i >= back, steady, lambda m: by_pairs(lo, i + 1, lambda c, mm: logits(c, mm, True, True), m), mx)

    acc_ref[...] = jnp.zeros(acc_ref.shape, F32)

    def accumulate(c, ls):
        off = pl.multiple_of(c * tq, tq)
        vt = vt_ref[0, 0, :, pl.ds(off, tq)]
        out = []
        for r in range(NSA_GROUP):
            p = jnp.exp2(st_ref[r, pl.ds(off, tq), :] - mx[r])
            out.append(ls[r] + jnp.sum(p, axis=0, keepdims=True))
            acc_ref[r] += _dot(vt, p.astype(BF16))
        return tuple(out)

    ls = tuple(jnp.zeros((1, tq), F32) for _ in range(NSA_GROUP))
    if window is None:
        ls = by_pairs(lo, i + 1, accumulate, ls)
    else:
        def steady_acc(l):
            for u in range(back, -1, -1):
                l = accumulate(i - u, l)
            return l

        ls = lax.cond(i >= back, steady_acc, lambda l: by_pairs(lo, i + 1, accumulate, l), ls)
    for r in range(NSA_GROUP):
        o_ref[0, r * HEAD_DIM:(r + 1) * HEAD_DIM, :] = (acc_ref[r] / ls[r]).astype(o_ref.dtype)


def _flash(qp, kp, vt, window):
    b, _, t, _ = qp.shape
    nq = t // TQ
    return pl.pallas_call(
        functools.partial(_flash_body, window=window),
        grid=(b, NSA_KV_HEADS, nq),
        in_specs=[pl.BlockSpec((1, NSA_GROUP, TQ, LANES), lambda bi, g, i: (bi, g, i, 0)),
                  pl.BlockSpec((1, 1, t, LANES), lambda bi, g, i: (bi, g, 0, 0)),
                  pl.BlockSpec((1, 1, HEAD_DIM, t), lambda bi, g, i: (bi, g, 0, 0))],
        out_specs=pl.BlockSpec((1, NSA_GROUP * HEAD_DIM, TQ), lambda bi, g, i: (bi, g, i)),
        out_shape=jax.ShapeDtypeStruct((b, NSA_HEADS * HEAD_DIM, t), BF16),
        scratch_shapes=[pltpu.VMEM((NSA_GROUP, t, TQ), F32), pltpu.VMEM((NSA_GROUP, HEAD_DIM, TQ), F32)],
        compiler_params=_cparams(("arbitrary", "arbitrary", "arbitrary")),
        name="flash_win" if window is not None else "flash_slc",
    )(qp, kp, vt)


def _diag_blocks(o_full, o_ref, s=0):
    rows = o_ref.shape[1] // NSA_KV_HEADS
    for g in range(NSA_KV_HEADS):
        o_ref[s, g * rows:(g + 1) * rows, :] = o_full[g * rows:(g + 1) * rows, g * HEAD_DIM:(g + 1) * HEAD_DIM]


def _cmp_sample_body(pt_ref, *refs, n_pages, past_len, dec, sb):
    fs_refs = refs[:sb * n_pages]
    qt_ref, ov_ref, msum_ref, oc_ref, at_ref = refs[sb * n_pages:]
    nq = qt_ref.shape[1]
    nblk = n_pages * fs_refs[0].shape[0]
    nj = ov_ref.shape[0]
    pos_l = past_len + (lax.broadcasted_iota(jnp.int32, (nblk, nq), 1) // NSA_GROUP) % dec
    n_c = lax.broadcasted_iota(jnp.int32, (nblk, nq), 0)
    vis_t = jnp.logical_and(n_c * CMP_STRIDE + 2 * CMP_STRIDE - 1 <= pos_l, n_c < nblk - 1)
    jr = lax.broadcasted_iota(jnp.int32, (nj, nq), 0)
    pos = past_len + (lax.broadcasted_iota(jnp.int32, (nj, nq), 1) // NSA_GROUP) % dec
    n_sel = -(-(past_len + dec) // SEL_BLOCK)
    for s in range(sb):
        fs = jnp.concatenate([r[...] for r in fs_refs[s * n_pages:(s + 1) * n_pages]], axis=0)
        kc, vc = _finish_compress(fs)
        qt = qt_ref[s]
        p_t = _softmax_masked(_nt(kc, qt), vis_t, 0)
        _diag_blocks(_tn(p_t.astype(BF16), vc), oc_ref, s)
        a1 = _split_dot(ov_ref[...], p_t)
        hi = a1.astype(BF16)
        lo = (a1 - hi.astype(F32)).astype(BF16)
        sc = _dot(hi, msum_ref[...]) + _dot(lo, msum_ref[...])
        at_ref[s] = _select_mask(sc, jr, pos, n_sel)


def _cmp_sample(page_table, fs_phys, qt, ov_t, msum, past_len, dec):
    nb, n_pages = page_table.shape
    sb = CMP_SEQS
    nj = ov_t.shape[0]
    fs_specs =[pl.BlockSpec((SUBLANES, 1024), functools.partial(lambda b, pt, s, k: (pt[b * sb + s, k], 0), s=s, k=k))
                for s in range(sb) for k in range(n_pages)]
    grid_spec = pltpu.PrefetchScalarGridSpec(
        num_scalar_prefetch=1,
        grid=(nb // sb,),
        in_specs=fs_specs + [pl.BlockSpec((sb, LANES, 256), lambda b, pt: (b, 0, 0)),
                             pl.BlockSpec(ov_t.shape, lambda b, pt: (0, 0)),
                             pl.BlockSpec(msum.shape, lambda b, pt: (0, 0))],
        out_specs=[pl.BlockSpec((sb, 64, HEAD_DIM), lambda b, pt: (b, 0, 0)),
                   pl.BlockSpec((sb, nj, LANES), lambda b, pt: (b, 0, 0))],
    )
    return pl.pallas_call(
        functools.partial(_cmp_sample_body, n_pages=n_pages, past_len=past_len, dec=dec, sb=sb),
        grid_spec=grid_spec,
        out_shape=[jax.ShapeDtypeStruct((nb, 64, HEAD_DIM), F32),
                   jax.ShapeDtypeStruct((nb, nj, LANES), F32)],
        compiler_params=_cparams(("arbitrary",)),
        name="cmp_sample",
    )(page_table, *([fs_phys] * (sb * n_pages)), qt, ov_t, msum)


def _attn_sample_body(*refs, n_blocks, sb, dec, has_pt, has_bm, has_pm, emit_state):
    refs = list(refs[1:] if has_pt else refs)
    n_kv = sb * n_blocks if has_pt else 1
    kv_refs = refs[:n_kv]
    rest = refs[n_kv:]
    new_ref, qt_ref = rest[0], rest[1]
    rest = rest[2:]
    bm_ref = rest.pop(0) if has_bm else None
    pm_ref = rest.pop(0) if has_pm else None
    nm_ref, o_ref = rest[0], rest[1]
    so_ref = rest[2] if emit_state else None
    st_ref = rest[-1]
    nq = qt_ref.shape[2]
    rows = kv_refs[0].shape[3]
    per = rows // SEL_BLOCK

    def kv(s, kb, which):
        return kv_refs[s * n_blocks + kb][0, which] if has_pt else kv_refs[0][s, which]

    for s in range(sb):
        qbd = qt_ref[s]
        m = jnp.full((1, nq), NEG, F32)
        for kb in range(n_blocks):
            sc = _tn(kv(s, kb, 0).astype(BF16), qbd)
            if has_pm:
                sc = sc + pm_ref[kb * rows:(kb + 1) * rows, :]
            if has_bm:
                sc = sc + jnp.concatenate(
                    [jnp.broadcast_to(bm_ref[s, kb * per + u:kb * per + u + 1, :], (SEL_BLOCK, nq)) for u in range(per)],
                    axis=0)
            st_ref[s, kb * rows:(kb + 1) * rows, :] = sc
            m = jnp.maximum(m, jnp.max(sc, axis=0, keepdims=True))
        s_new = _dot(new_ref[s, :, 0:256].astype(BF16), qbd) + nm_ref[...]
        if has_bm:
            s_new = s_new + bm_ref[s, n_blocks * per:n_blocks * per + 1, :]
        m = jnp.maximum(m, jnp.max(s_new, axis=0, keepdims=True))
        e_new = jnp.exp2(s_new - m)
        l = jnp.sum(e_new, axis=0, keepdims=True)
        acc = _tn(new_ref[s, :, 256:512].astype(BF16), e_new.astype(BF16))
        for kb in range(n_blocks):
            e = jnp.exp2(st_ref[s, kb * rows:(kb + 1) * rows, :] - m)
            l = l + jnp.sum(e, axis=0, keepdims=True)
            acc = acc + _dot(kv(s, kb, 1).astype(BF16), e.astype(BF16))
        o_ref[s] = acc / jnp.maximum(l, 1e-30)
        if emit_state:
            lane = lax.broadcasted_iota(jnp.int32, (256, LANES), 1)
            tail = pltpu.roll(new_ref[s], SUBLANES - dec, axis=0)
            for which in range(2):
                shifted = pltpu.roll(kv(s, 0, which), rows - dec, axis=1)
                fill = jnp.concatenate([jnp.zeros((LANES - SUBLANES, 256), F32),
                                        tail[:, which * 256:(which + 1) * 256]], axis=0).T
                last = jnp.where(lane < LANES - dec, shifted[:, rows - LANES:], fill)
                so_ref[s, which, :, 0:rows - LANES] = shifted[:, 0:rows - LANES]
                so_ref[s, which, :, rows - LANES:rows] = last


def _attn_sample(kv, slot_blk, n_blocks, new, qt, new_mask, sb, dec,
                 page_table=None, blk_mask=None, past_mask=None, emit_state=False):
    nb = qt.shape[0]
    kv_rows = kv.shape[3]
    has_pt = page_table is not None
    has_bm = blk_mask is not None
    has_pm = past_mask is not None
    if has_pt:
        kv_specs = [pl.BlockSpec((1, 2, 256, kv_rows),
                                 functools.partial(lambda b, pt, s, k: (pt[b * sb + s, k], slot_blk, 0, 0), s=s, k=k))
                    for s in range(sb) for k in range(n_blocks)]
        im = lambda b, pt: (b, 0, 0)
        im4 = lambda b, pt: (b, 0, 0, 0)
        cm = lambda b, pt: (0, 0)
    else:
        kv_specs = [pl.BlockSpec((sb, 2, 256, kv_rows), lambda b: (b, slot_blk, 0, 0))]
        im = lambda b: (b, 0, 0)
        im4 = lambda b: (b, 0, 0, 0)
        cm = lambda b: (0, 0)
    in_specs = kv_specs + [pl.BlockSpec((sb, SUBLANES, 512), im), pl.BlockSpec((sb, 256, LANES), im)]
    args = [kv] * len(kv_specs) + [new, qt]
    if has_bm:
        in_specs.append(pl.BlockSpec((sb,) + blk_mask.shape[1:], im))
        args.append(blk_mask)
    if has_pm:
        in_specs.append(pl.BlockSpec(past_mask.shape, cm))
        args.append(past_mask)
    in_specs.append(pl.BlockSpec(new_mask.shape, cm))
    args.append(new_mask)
    out_specs = [pl.BlockSpec((sb, 256, LANES), im)]
    out_shape = [jax.ShapeDtypeStruct((nb, 256, LANES), F32)]
    if emit_state:
        out_specs.append(pl.BlockSpec((sb, 2, 256, kv_rows), im4))
        out_shape.append(jax.ShapeDtypeStruct((nb, 2, 256, kv_rows), F32))
    scratch = [pltpu.VMEM((sb, n_blocks * kv_rows, LANES), F32)]
    body = functools.partial(_attn_sample_body, n_blocks=n_blocks, sb=sb, dec=dec, has_pt=has_pt, has_bm=has_bm,
                             has_pm=has_pm, emit_state=emit_state)
    if has_pt:
        gs = pltpu.PrefetchScalarGridSpec(num_scalar_prefetch=1, grid=(nb // sb,), in_specs=in_specs,
                                          out_specs=out_specs, scratch_shapes=scratch)
        return pl.pallas_call(body, grid_spec=gs, out_shape=out_shape, compiler_params=_cparams(("arbitrary",)),
                              name="slc_sample")(page_table, *args)
    return pl.pallas_call(body, grid=(nb // sb,), in_specs=in_specs, out_specs=out_specs, scratch_shapes=scratch,
                          out_shape=out_shape, compiler_params=_cparams(("arbitrary",)), name="win_sample")(*args)


def _out_proj_body(*refs, n_in, gated, feature_major):
    o_refs = refs[:n_in]
    rest = refs[n_in:]
    if gated:
        gate_ref, rest = rest[0], rest[1:]
    x_ref, w_ref, g_ref, y_ref = rest
    tm = x_ref.shape[0]
    if feature_major:
        gates_t = gate_ref[...].T
        heads = []
        for hd in range(NSA_HEADS):
            acc = jnp.zeros((HEAD_DIM, tm), F32)
            for kbr in range(n_in):
                row = gates_t[hd * 3 + kbr:hd * 3 + kbr + 1, :]
                acc = acc + row * o_refs[kbr][0, hd * HEAD_DIM:(hd + 1) * HEAD_DIM, :].astype(F32)
            heads.append(acc.astype(BF16))
        y = _tn(jnp.concatenate(heads, axis=0), w_ref[...])
        y_ref[...] = x_ref[...] + _rms(y, g_ref[...])
        return
    if gated:
        lane = lax.broadcasted_iota(jnp.int32, (tm, LANES), 1)
        gates = gate_ref[...]
        chunks = []
        for c in range(1024 // LANES):
            acc = jnp.zeros((tm, LANES), F32)
            for kbr in range(n_in):
                c0 = (2 * c) * 3 + kbr
                c1 = (2 * c + 1) * 3 + kbr
                gexp = jnp.where(lane < HEAD_DIM, gates[:, c0:c0 + 1], gates[:, c1:c1 + 1])
                acc = acc + gexp * o_refs[kbr][:, c * LANES:(c + 1) * LANES].astype(F32)
            chunks.append(acc.astype(BF16))
        o = jnp.concatenate(chunks, axis=1)
    else:
        o = o_refs[0][...].astype(BF16)
    y = _dot(o, w_ref[...])
    y_ref[...] = x_ref[...] + _rms(y, g_ref[...])


def _out_proj(os_, gates, x, w, g, seq_len=None):
    m, d = x.shape
    row = lambda i: (i, 0)
    gated = gates is not None
    feature_major = seq_len is not None
    if feature_major:
        nt = seq_len // TM
        in_specs = [pl.BlockSpec((1, 1024, TM), lambda i: (i // nt, 0, i % nt)) for _ in os_]
    else:
        in_specs = [pl.BlockSpec((TM, 1024), row) for _ in os_]
    args = list(os_)
    if gated:
        in_specs.append(pl.BlockSpec((TM, LANES), row))
        args.append(gates)
    in_specs += [pl.BlockSpec((TM, d), row), _resident(w.shape), _resident(g.shape)]
    args += [x, w, g]
    return pl.pallas_call(
        functools.partial(_out_proj_body, n_in=len(os_), gated=gated, feature_major=feature_major),
        grid=(m // TM,), in_specs=in_specs, out_specs=pl.BlockSpec((TM, d), row),
        out_shape=jax.ShapeDtypeStruct((m, d), F32),
        compiler_params=_cparams(("arbitrary",)), name="out_proj",
    )(*args)


def _ffn_body(*refs, sample, tiles_per_seq, nb):
    if sample:
        x_ref, g2_ref, g3_ref, wu_ref, cw_ref, cb_ref, wd_ref, buf_ref, y_ref, tail_ref = refs
    else:
        x_ref, g2_ref, g3_ref, wu_ref, cw_ref, cb_ref, wd_ref, y_ref, tail_ref, carry_ref = refs
    x = x_ref[...]
    tm = x.shape[0]
    f = wd_ref.shape[0]
    h = _rms(x, g2_ref[...]).astype(BF16)
    acc = jnp.zeros(x.shape, F32)
    if not sample:
        first = pl.program_id(0) % tiles_per_seq == 0
        row = lax.broadcasted_iota(jnp.int32, (tm, FFN_FCHUNK), 0)
    for fc in range(f // FFN_FCHUNK):
        sl = slice(fc * FFN_FCHUNK, (fc + 1) * FFN_FCHUNK)
        gate = _dot(h, wu_ref[:, sl])
        val = _dot(h, wu_ref[:, f + fc * FFN_FCHUNK:f + (fc + 1) * FFN_FCHUNK])
        if sample:
            prev1 = jnp.concatenate([buf_ref[nb:2 * nb, sl], gate[0:tm - nb]], axis=0)
            prev2 = jnp.concatenate([buf_ref[0:2 * nb, sl], gate[0:tm - 2 * nb]], axis=0)
            tail_ref[:, sl] = gate[tm - 2 * nb:tm]
        else:
            carry = carry_ref[:, sl]
            c6 = jnp.where(first, 0.0, carry[6:7, :])
            c7 = jnp.where(first, 0.0, carry[7:8, :])
            prev1 = jnp.where(row == 0, c7, pltpu.roll(gate, 1, axis=0))
            prev2 = jnp.where(row == 0, c6, jnp.where(row == 1, c7, pltpu.roll(gate, 2, axis=0)))
            carry_ref[:, sl] = gate[tm - SUBLANES:tm]
            tail_ref[:, sl] = gate[tm - SUBLANES:tm]
        conv = cb_ref[0:1, sl] + cw_ref[0:1, sl] * prev2 + cw_ref[1:2, sl] * prev1 + cw_ref[2:3, sl] * gate
        act = (jax.nn.gelu(conv) * val).astype(BF16)
        acc = acc + _dot(act, wd_ref[sl, :])
    y_ref[...] = x + _rms(acc, g3_ref[...])


def _ffn(x, g2, g3, wu, cw, cb, wd, seq_len=None, buf=None):
    m, d = x.shape
    f = wd.shape[0]
    sample = buf is not None
    row = lambda i: (i, 0)
    if sample:
        tm = m
        nb = buf.shape[0] // 2
        tail_shape, tail_block = (2 * nb, f), (2 * nb, f)
        scratch = []
    else:
        tm = FFN_TM
        nb = 0
        tail_shape, tail_block = (m // tm * SUBLANES, f), (SUBLANES, f)
        scratch = [pltpu.VMEM((SUBLANES, f), F32)]
    in_specs = [pl.BlockSpec((tm, d), row), _resident(g2.shape), _resident(g3.shape), _resident(wu.shape),
                _resident(cw.shape), _resident(cb.shape), _resident(wd.shape)]
    args = [x, g2, g3, wu, cw, cb, wd]
    if sample:
        in_specs.append(_resident(buf.shape))
        args.append(buf)
    return pl.pallas_call(
        functools.partial(_ffn_body, sample=sample, tiles_per_seq=(seq_len // tm if not sample else 1), nb=nb),
        grid=(m // tm,), in_specs=in_specs,
        out_specs=[pl.BlockSpec((tm, d), row), pl.BlockSpec(tail_block, row)],
        out_shape=[jax.ShapeDtypeStruct((m, d), F32), jax.ShapeDtypeStruct(tail_shape, F32)],
        scratch_shapes=scratch,
        compiler_params=_cparams(("arbitrary",)), name="ffn_sample" if sample else "ffn_prompt",
    )(*args)


def _gla_proj_body(x_ref, g_ref, w_ref, wg_ref, bg_ref, q_ref, k_ref, v_ref, sr_ref, la_ref):
    h = _rms(x_ref[...], g_ref[...]).astype(BF16)
    nk = GLA_HEADS * GLA_DK
    nv = GLA_HEADS * GLA_DV
    for c in range(nk // 256):
        q_ref[:, c * 256:(c + 1) * 256] = _dot(h, w_ref[:, c * 256:(c + 1) * 256]) * (GLA_DK ** -0.5)
        k_ref[:, c * 256:(c + 1) * 256] = _dot(h, w_ref[:, nk + c * 256:nk + (c + 1) * 256])
    for c in range(nv // 256):
        v_ref[:, c * 256:(c + 1) * 256] = _dot(h, w_ref[:, 2 * nk + c * 256:2 * nk + (c + 1) * 256]).astype(BF16)
        r = _dot(h, w_ref[:, 2 * nk + nv + c * 256:2 * nk + nv + (c + 1) * 256])
        sr_ref[:, c * 256:(c + 1) * 256] = jax.nn.silu(r).astype(BF16)
    low = _dot(h, w_ref[:, 2 * nk + 2 * nv:2 * nk + 2 * nv + LANES]).astype(BF16)
    gz = _dot(low, wg_ref[...]) + bg_ref[...]
    log_sig = jnp.minimum(gz, 0.0) - jnp.log1p(jnp.exp(-jnp.abs(gz)))
    la_ref[...] = log_sig / GLA_GATE_TEMP


def _gla_proj(x, g, w, wg, bg):
    m, d = x.shape
    nk = GLA_HEADS * GLA_DK
    nv = GLA_HEADS * GLA_DV
    row = lambda i: (i, 0)
    return pl.pallas_call(
        _gla_proj_body, grid=(m // TM,),
        in_specs=[pl.BlockSpec((TM, d), row), _resident(g.shape), _resident(w.shape),
                  _resident(wg.shape), _resident(bg.shape)],
        out_specs=[pl.BlockSpec((TM, nk), row), pl.BlockSpec((TM, nk), row), pl.BlockSpec((TM, nv), row),
                   pl.BlockSpec((TM, nv), row), pl.BlockSpec((TM, nk), row)],
        out_shape=[jax.ShapeDtypeStruct((m, nk), F32), jax.ShapeDtypeStruct((m, nk), F32),
                   jax.ShapeDtypeStruct((m, nv), BF16), jax.ShapeDtypeStruct((m, nv), BF16),
                   jax.ShapeDtypeStruct((m, nk), F32)],
        compiler_params=_cparams(("arbitrary",)), name="gla_proj",
    )(x, g, w, wg, bg)


def _cumsum_rows(x):
    n = x.shape[0]
    row = lax.broadcasted_iota(jnp.int32, x.shape, 0)
    sh = 1
    while sh < n:
        x = x + jnp.where(row >= sh, pltpu.roll(x, sh, axis=0), 0.0)
        sh *= 2
    return x


def _gla_chunk(q, k, la, v, s_old):
    cs = q.shape[0]
    row = lax.broadcasted_iota(jnp.int32, (cs, 1), 0)
    trow = lax.broadcasted_iota(jnp.int32, (cs, cs), 0)
    scol = lax.broadcasted_iota(jnp.int32, (cs, cs), 1)
    r8 = lax.broadcasted_iota(jnp.int32, (SUBLANES, 1), 0)
    lane8 = lax.broadcasted_iota(jnp.int32, (SUBLANES, cs), 1)
    if True:
        cum = _cumsum_rows(la)
        last = cum[cs - 1:cs, :]
        out = _dot((q * jnp.exp(cum)).astype(BF16), s_old.astype(BF16))
        att = None
        hh = cs // 2
        while hh >= SUBLANES:
            nblk = cs // (2 * hh)
            ref = jnp.concatenate(
                [jnp.broadcast_to(cum[u * 2 * hh + hh - 1:u * 2 * hh + hh, :], (2 * hh, GLA_DK)) for u in range(nblk)],
                axis=0) if nblk > 1 else jnp.broadcast_to(cum[hh - 1:hh, :], (cs, GLA_DK))
            second = (row % (2 * hh)) >= hh
            qh = jnp.where(second, q * jnp.exp(jnp.minimum(cum - ref, 0.0)), 0.0).astype(BF16)
            kh = jnp.where(second, 0.0, k * jnp.exp(jnp.minimum(ref - cum, 0.0))).astype(BF16)
            a = _nt(qh, kh)
            if nblk > 1:
                a = jnp.where(trow // (2 * hh) == scol // (2 * hh), a, 0.0)
            att = a if att is None else att + a
            hh //= 2
        slabs = []
        for g8 in range(cs // SUBLANES):
            r0 = g8 * SUBLANES
            cg, qg, kg = cum[r0:r0 + SUBLANES], q[r0:r0 + SUBLANES], k[r0:r0 + SUBLANES]
            slab = jnp.zeros((SUBLANES, GLA_DV if att is None else cs), F32)
            for s in range(SUBLANES):
                e = jnp.exp(jnp.minimum(cg - cg[s:s + 1, :], 0.0))
                col = jnp.sum(e * qg * kg[s:s + 1, :], axis=1, keepdims=True)
                col = jnp.where(r8 >= s, col, 0.0)
                if att is None:
                    slab = slab + col * v[r0 + s:r0 + s + 1, :].astype(F32)
                else:
                    slab = jnp.where(lane8 == r0 + s, col, slab)
            slabs.append(slab)
        if att is None:
            out = out + slabs[0]
        else:
            att = att + jnp.concatenate(slabs, axis=0)
            out = out + _dot(att.astype(BF16), v)
        kt = (k * jnp.exp(last - cum)).astype(BF16)
        dcol = jnp.broadcast_to(jnp.exp(last), (SUBLANES, GLA_DK)).T[:, 0:1]
        return out, dcol * s_old + _tn(kt, v)


def _gla_rec_prompt_body(q_ref, k_ref, la_ref, v_ref, sr_ref, gn_ref, o_ref, so_ref, st_ref):
    c = pl.program_id(1)

    @pl.when(c == 0)
    def _():
        st_ref[...] = jnp.zeros(st_ref.shape, F32)

    for hd in range(GLA_HEADS):
        ksl = slice(hd * GLA_DK, (hd + 1) * GLA_DK)
        vsl = slice(hd * GLA_DV, (hd + 1) * GLA_DV)
        out, s_new = _gla_chunk(q_ref[:, ksl], k_ref[:, ksl], la_ref[:, ksl], v_ref[:, vsl], st_ref[hd])
        st_ref[hd] = s_new
        o_ref[:, vsl] = (_rms(out, gn_ref[...]) * sr_ref[:, vsl].astype(F32)).astype(o_ref.dtype)

    @pl.when(c == pl.num_programs(1) - 1)
    def _():
        so_ref[0] = st_ref[...]


def _gla_rec_sample_body(q_ref, k_ref, la_ref, v_ref, sr_ref, gn_ref, s0_ref, o_ref, so_ref, *, sb):
    for s in range(sb):
        rsl = slice(s * SUBLANES, (s + 1) * SUBLANES)
        for hd in range(GLA_HEADS):
            ksl = slice(hd * GLA_DK, (hd + 1) * GLA_DK)
            vsl = slice(hd * GLA_DV, (hd + 1) * GLA_DV)
            out, s_new = _gla_chunk(q_ref[rsl, ksl], k_ref[rsl, ksl], la_ref[rsl, ksl], v_ref[rsl, vsl], s0_ref[s, hd])
            so_ref[s, hd] = s_new
            o_ref[rsl, vsl] = (_rms(out, gn_ref[...]) * sr_ref[rsl, vsl].astype(F32)).astype(o_ref.dtype)


def _gla_rec(q, k, la, v, sr, gn, nb, chunk, s0=None):
    m = q.shape[0]
    nk = GLA_HEADS * GLA_DK
    nv = GLA_HEADS * GLA_DV
    out_shape = [jax.ShapeDtypeStruct((m, nv), BF16), jax.ShapeDtypeStruct((nb, GLA_HEADS, GLA_DK, GLA_DV), F32)]
    if s0 is None:
        nc = m // nb // chunk
        row = lambda b, c: (b * nc + c, 0)
        st_spec = pl.BlockSpec((1, GLA_HEADS, GLA_DK, GLA_DV), lambda b, c: (b, 0, 0, 0))
        return pl.pallas_call(
            _gla_rec_prompt_body, grid=(nb, nc),
            in_specs=[pl.BlockSpec((chunk, nk), row), pl.BlockSpec((chunk, nk), row), pl.BlockSpec((chunk, nk), row),
                      pl.BlockSpec((chunk, nv), row), pl.BlockSpec((chunk, nv), row),
                      pl.BlockSpec(gn.shape, lambda b, c: (0, 0))],
            out_specs=[pl.BlockSpec((chunk, nv), row), st_spec], out_shape=out_shape,
            scratch_shapes=[pltpu.VMEM((GLA_HEADS, GLA_DK, GLA_DV), F32)],
            compiler_params=_cparams(("arbitrary", "arbitrary")), name="gla_rec_prompt",
        )(q, k, la, v, sr, gn)
    sb = GLA_SEQS
    rows = sb * chunk
    row = lambda b: (b, 0)
    st_spec = pl.BlockSpec((sb, GLA_HEADS, GLA_DK, GLA_DV), lambda b: (b, 0, 0, 0))
    return pl.pallas_call(
        functools.partial(_gla_rec_sample_body, sb=sb), grid=(nb // sb,),
        in_specs=[pl.BlockSpec((rows, nk), row), pl.BlockSpec((rows, nk), row), pl.BlockSpec((rows, nk), row),
                  pl.BlockSpec((rows, nv), row), pl.BlockSpec((rows, nv), row),
                  pl.BlockSpec(gn.shape, lambda b: (0, 0)), st_spec],
        out_specs=[pl.BlockSpec((rows, nv), row), st_spec], out_shape=out_shape,
        compiler_params=_cparams(("arbitrary",)), name="gla_rec_sample",
    )(q, k, la, v, sr, gn, s0)


def _rope_tables(pos):
    half = HEAD_DIM // 2
    inv = ROPE_THETA ** (-jnp.arange(half, dtype=F32) / half)
    ang = pos.astype(F32)[:, None] * inv[None, :]
    cos, sin = jnp.cos(ang), jnp.sin(ang)
    z = jnp.zeros_like(sin)
    cos_t = jnp.tile(cos, (1, 4))
    sa = jnp.tile(jnp.concatenate([-sin, z], axis=1), (1, 2))
    sb = jnp.tile(jnp.concatenate([z, sin], axis=1), (1, 2))
    return cos_t, sa, sb


def _pad_cols(w, n):
    return jnp.pad(w, ((0, 0), (0, n - w.shape[1])))


def _cmp_weights(cmp_pe, cmp_w):
    g = NSA_KV_HEADS
    w = cmp_w.reshape(2, 2, CMP_STRIDE, HEAD_DIM, HEAD_DIM)
    eye = jnp.eye(g, dtype=F32)
    wbd = jnp.einsum('shjde,gk->sjgdhke', w, eye)
    wbd = wbd.reshape(2, CMP_STRIDE, g * HEAD_DIM, 2 * g * HEAD_DIM).astype(BF16)
    pe = cmp_pe.reshape(2, 2, CMP_STRIDE, 1, 1, HEAD_DIM)
    pe_t = jnp.broadcast_to(pe, (2, 2, CMP_STRIDE, SUBLANES, g, HEAD_DIM)).reshape(2, 2, CMP_STRIDE, SUBLANES, g * HEAD_DIM)
    return wbd, pe_t


def _overlap_t(n_blk, n_sel, rows, cols):
    i = jnp.arange(cols)[None, :]
    j = jnp.arange(rows)[:, None]
    ov = (i * CMP_STRIDE + 2 * CMP_STRIDE > j * SEL_BLOCK) & (i * CMP_STRIDE < (j + 1) * SEL_BLOCK)
    ov = ov & (i < n_blk) & (j < n_sel)
    return ov.astype(BF16)


def _nsa_layer_prompt(x, g, w_in, wbd, cbias, w_o, tabs, b, t):
    q, gates, rows_t, win_t, k_slc, vt_slc, k_win, vt_win = _nsa_proj_t(x, g[0:1], w_in, *tabs, b, t)
    fs = _compress_pages(rows_t.reshape(b, 4, NSA_KV_HEADS * HEAD_DIM, t), wbd, cbias)
    n_blk = t // CMP_STRIDE - 1
    n_sel = -(-t // SEL_BLOCK)
    oc, qp = _cmp_prompt(q, fs, _overlap_t(n_blk, n_sel, n_sel, t // CMP_STRIDE), b, t)
    o_s = _flash(qp, k_slc, vt_slc, None)
    o_w = _flash(qp, k_win, vt_win, WINDOW)
    x = _out_proj([oc, o_s, o_w], gates, x, w_o, g[1:2], seq_len=t)
    return x, rows_t, win_t


def _to_rows(o, nb, dec):
    o = o.reshape(nb, NSA_KV_HEADS, dec, NSA_GROUP, HEAD_DIM).transpose(2, 0, 1, 3, 4)
    return o.reshape(dec * nb, NSA_HEADS * HEAD_DIM).astype(BF16)


def _nsa_layer_sample(x, g, w_in, wbd, cbias, w_o, tabs, cache, page_table, win_state, nb, dec):
    n_phys, page = cache.shape[0], cache.shape[1]
    n_pages = page_table.shape[1]
    past_len = n_pages * page
    q, rows, win, gates = _nsa_proj(x, g[0:1], w_in, *tabs)
    pages_t = cache.transpose(0, 2, 3, 4, 1).reshape(n_phys, 4, NSA_KV_HEADS * HEAD_DIM, page)
    wb = win_state.shape[1]
    win_t = win_state.transpose(0, 2, 3, 4, 1).reshape(nb, 2, NSA_KV_HEADS * HEAD_DIM, wb)
    fs_phys = _compress_pages(pages_t, wbd, cbias)
    q5 = q.reshape(dec, nb, NSA_KV_HEADS, NSA_GROUP, HEAD_DIM).transpose(1, 2, 0, 3, 4)
    eye = jnp.eye(NSA_KV_HEADS, dtype=BF16)
    qt = q5.reshape(nb, NSA_KV_HEADS, dec * NSA_GROUP, 1, HEAD_DIM) * eye[None, :, None, :, None]
    qt = qt.reshape(nb, NSA_KV_HEADS * dec * NSA_GROUP, NSA_KV_HEADS * HEAD_DIM)
    qt = jnp.pad(qt, ((0, 0), (0, LANES - qt.shape[1]), (0, 0)))
    n_blk = (past_len + dec) // CMP_STRIDE - 1
    n_sel = -(-(past_len + dec) // SEL_BLOCK)
    nj = -(-n_sel // SUBLANES) * SUBLANES
    lane = jnp.arange(LANES)
    msum = (lane[:, None] // NSA_GROUP == lane[None, :] // NSA_GROUP).astype(BF16)
    oc, a_t = _cmp_sample(page_table, fs_phys, qt, _overlap_t(n_blk, n_sel, nj, past_len // CMP_STRIDE), msum, past_len, dec)
    t_q = ((lane // NSA_GROUP) % dec)[None, :]
    j8 = jnp.arange(SUBLANES)[:, None]
    new_mask = jnp.where((j8 < dec) & (j8 <= t_q), 0.0, NEG).astype(F32)
    win_mask = jnp.where(jnp.arange(wb)[:, None] >= t_q + wb - WINDOW, 0.0, NEG).astype(F32)
    qbd = qt.transpose(0, 2, 1)

    def new_rows(a):
        a = a.reshape(dec, nb, 512).transpose(1, 0, 2)
        return jnp.pad(a, ((0, 0), (0, SUBLANES - dec), (0, 0)))

    def from_t(o):
        o = o[:, :, :NSA_KV_HEADS * dec * NSA_GROUP].reshape(nb, NSA_KV_HEADS, HEAD_DIM, NSA_KV_HEADS, dec, NSA_GROUP)
        o = jnp.diagonal(o, axis1=1, axis2=3)
        return o.transpose(2, 0, 4, 3, 1).reshape(dec * nb, NSA_HEADS * HEAD_DIM).astype(BF16)

    (o_s,) = _attn_sample(pages_t, 1, n_pages, new_rows(rows[:, 512:1024]), qbd, new_mask, SLC_SEQS, dec,
                          page_table=page_table, blk_mask=a_t)
    o_w, win_next = _attn_sample(win_t, 0, 1, new_rows(win), qbd, new_mask, WIN_SEQS, dec,
                                 past_mask=win_mask, emit_state=True)
    x = _out_proj([_to_rows(oc, nb, dec), from_t(o_s), from_t(o_w)], gates, x, w_o, g[1:2])
    win_next = win_next.reshape(nb, 2, NSA_KV_HEADS, HEAD_DIM, wb).transpose(0, 4, 1, 2, 3)
    return x, rows, win_next


def _gla_layer(x, g, w_in, wg, bg, gn, w_o, nb, seq, s0=None):
    q, k, v, sr, la = _gla_proj(x, g[0:1], w_in, wg, bg)
    if s0 is None:
        o, s = _gla_rec(q, k, la, v, sr, gn, nb, GLA_CHUNK)
    else:
        def seqs(a):
            a = a.reshape(seq, nb, a.shape[1]).transpose(1, 0, 2)
            return jnp.pad(a, ((0, 0), (0, SUBLANES - seq), (0, 0))).reshape(nb * SUBLANES, a.shape[2])
        o, s = _gla_rec(seqs(q), seqs(k), seqs(la), seqs(v), seqs(sr), gn, nb, SUBLANES, s0=s0)
        o = o.reshape(nb, SUBLANES, o.shape[1])[:, :seq].transpose(1, 0, 2).reshape(seq * nb, o.shape[1])
    x = _out_proj([o], None, x, w_o, g[1:2])
    return x, s


def kernel(x_prompt, x_sample, cache_nsa_kv, state_win_kv, state_gla, state_ffn_conv, page_table, norm_gain, nsa_w_in, nsa_cmp_pe, nsa_cmp_w, nsa_w_o, gla_w_in, gla_w_gate_up, gla_b_gate, gla_norm_gain, gla_w_o, ffn_w_up, ffn_conv_w, ffn_conv_b, ffn_w_down):
    b, t, d = x_prompt.shape
    nb, dec, _ = x_sample.shape
    depth = norm_gain.shape[0]
    f = ffn_w_down.shape[1]
    past_len = page_table.shape[1] * cache_nsa_kv.shape[2]

    xp = x_prompt.reshape(b * t, d)
    xs = x_sample.transpose(1, 0, 2).reshape(dec * nb, d)
    tabs_p = _rope_tables(jnp.arange(t, dtype=jnp.int32))
    tabs_s = _rope_tables(past_len + jnp.repeat(jnp.arange(dec, dtype=jnp.int32), nb))

    nsa_p, nsa_s, win_p, win_s, gla_p, gla_s, ffn_p, ffn_s = [], [], [], [], [], [], [], []
    for i in range(depth):
        g = norm_gain[i]
        a = i // 2
        if i % 2 == 0:
            w_in = _pad_cols(nsa_w_in[a], 2688).astype(BF16)
            w_o = nsa_w_o[a].astype(BF16)
            wbd, pe_t = _cmp_weights(nsa_cmp_pe[a], nsa_cmp_w[a])
            cbias = _cmp_bias(pe_t, wbd)
            xp, rows_t, win_t = _nsa_layer_prompt(xp, g, w_in, wbd, cbias, w_o, tabs_p, b, t)
            nsa_p.append(rows_t.reshape(b, 4, NSA_KV_HEADS, HEAD_DIM, t).transpose(0, 4, 1, 2, 3))
            nw = min(WINDOW, t)
            win_p.append(win_t.reshape(b, 2, NSA_KV_HEADS, HEAD_DIM, t)[..., t - nw:].transpose(0, 4, 1, 2, 3))
            xs, rows, win_next = _nsa_layer_sample(xs, g, w_in, wbd, cbias, w_o, tabs_s, cache_nsa_kv[a], page_table,
                                                   state_win_kv[a], nb, dec)
            nsa_s.append(rows.reshape(dec, nb, 4, NSA_KV_HEADS, HEAD_DIM).transpose(1, 0, 2, 3, 4))
            win_s.append(win_next)
        else:
            w_in = _pad_cols(gla_w_in[a], 3200).astype(BF16)
            wg = jnp.pad(gla_w_gate_up[a], ((0, LANES - GLA_GATE_RANK), (0, 0))).astype(BF16)
            bg = gla_b_gate[a][None, :]
            gn = gla_norm_gain[a][None, :]
            w_o = gla_w_o[a].astype(BF16)
            xp, s = _gla_layer(xp, g, w_in, wg, bg, gn, w_o, b, t)
            gla_p.append(s)
            xs, s = _gla_layer(xs, g, w_in, wg, bg, gn, w_o, nb, dec, s0=state_gla[a])
            gla_s.append(s)
        wu = ffn_w_up[i].astype(BF16)
        wd = ffn_w_down[i].astype(BF16)
        cw = jnp.pad(ffn_conv_w[i], ((0, SUBLANES - ffn_conv_w.shape[1]), (0, 0)))
        cb = ffn_conv_b[i][None, :]
        xp, tail = _ffn(xp, g[2:3], g[3:4], wu, cw, cb, wd, seq_len=t)
        tail = tail.reshape(b, t // FFN_TM, SUBLANES, f)[:, -1, SUBLANES - 2:]
        ffn_p.append(tail)
        buf = state_ffn_conv[i].transpose(1, 0, 2).reshape(2 * nb, f)
        xs, tail = _ffn(xs, g[2:3], g[3:4], wu, cw, cb, wd, buf=buf)
        ffn_s.append(tail.reshape(2, nb, f).transpose(1, 0, 2))

    y_prompt = xp.reshape(b, t, d)
    y_sample = xs.reshape(dec, nb, d).transpose(1, 0, 2)
    return (y_prompt, y_sample, jnp.stack(nsa_p), jnp.stack(nsa_s), jnp.stack(win_p), jnp.stack(win_s),
            jnp.stack(gla_p), jnp.stack(gla_s), jnp.stack(ffn_p), jnp.stack(ffn_s))
```

```python
import functools
import math

import jax
import jax.numpy as jnp
from jax import lax
from jax.experimental import pallas as pl
from jax.experimental.pallas import tpu as pltpu

F32, BF16 = jnp.float32, jnp.bfloat16

HEAD_DIM = 64
NSA_HEADS = 16
NSA_KV_HEADS = 4
NSA_GROUP = 4
CMP_STRIDE = 16
SEL_BLOCK = 64
TOP_N = 16
WINDOW = 512
ROPE_THETA = 10000.0
GLA_HEADS = 4
GLA_DK = 128
GLA_DV = 256
GLA_GATE_RANK = 16
GLA_GATE_TEMP = 16.0
NORM_EPS = 1e-6
FORCE = 1e6
NEG = -1e30

LANES = 128
SUBLANES = 8
VMEM_LIMIT = 48 * 1024 * 1024

TM = 256
TQ = 256
VT_ROWS = 80
CMP_ROWS = 256
GLA_CHUNK = 128
SLC_SEQS = 2
WIN_SEQS = 4
CMP_SEQS = 4
GLA_SEQS = 4
FFN_FCHUNK = 2816
FFN_TM = 256


def _cparams(sem):
    return pltpu.CompilerParams(dimension_semantics=sem, vmem_limit_bytes=VMEM_LIMIT)


def _resident(shape):
    nd = len(shape)
    return pl.BlockSpec(shape, lambda *_: (0,) * nd, pipeline_mode=pl.Buffered(1))


def _rms(x, g):
    return x * lax.rsqrt(jnp.mean(x * x, axis=-1, keepdims=True) + NORM_EPS) * g


def _nt(a, b):
    return lax.dot_general(a, b, (((1,), (1,)), ((), ())), preferred_element_type=F32)


def _tn(a, b):
    return lax.dot_general(a, b, (((0,), (0,)), ((), ())), preferred_element_type=F32)


def _dot(a, b):
    return jnp.dot(a, b, preferred_element_type=F32)


def _split_dot(w, x):
    hi = x.astype(BF16)
    lo = (x - hi.astype(F32)).astype(BF16)
    return _dot(w, hi) + _dot(w, lo)


def _softmax_masked(s, mask, axis):
    s = jnp.where(mask, s, NEG)
    m = jnp.max(s, axis=axis, keepdims=True)
    e = jnp.where(mask, jnp.exp2(s - m), 0.0)
    return e / jnp.maximum(jnp.sum(e, axis=axis, keepdims=True), 1e-30)


Q_SCALE = HEAD_DIM ** -0.5 * math.log2(math.e)


def _nsa_proj_parts(x_ref, g_ref, w_ref, cos_ref, sa_ref, sb_ref, q_ref, gate_ref):
    h = _rms(x_ref[...], g_ref[...]).astype(BF16)
    cos, sa, sb = cos_ref[...], sa_ref[...], sb_ref[...]

    def rope(z):
        return z * cos + pltpu.roll(z, 96, axis=1) * sa + pltpu.roll(z, 32, axis=1) * sb

    def proj(c0):
        return _dot(h, w_ref[:, c0:c0 + 256])

    def rope256(z):
        return jnp.concatenate([rope(z[:, :LANES]), rope(z[:, LANES:])], axis=1)

    for c in range(4):
        q_ref[:, c * 256:(c + 1) * 256] = (rope256(proj(c * 256)) * Q_SCALE).astype(BF16)
    gate_ref[...] = jax.nn.sigmoid(_dot(h, w_ref[:, 2560:2688]))
    return (rope256(proj(1024)), proj(1280), rope256(proj(1536)), proj(1792), rope256(proj(2048)), proj(2304))


def _nsa_proj_body(x_ref, g_ref, w_ref, cos_ref, sa_ref, sb_ref, q_ref, rows_ref, win_ref, gate_ref):
    parts = _nsa_proj_parts(x_ref, g_ref, w_ref, cos_ref, sa_ref, sb_ref, q_ref, gate_ref)
    for c in range(4):
        rows_ref[:, c * 256:(c + 1) * 256] = parts[c]
    win_ref[:, 0:256] = parts[4]
    win_ref[:, 256:512] = parts[5]


def _nsa_proj_t_body(x_ref, g_ref, w_ref, cos_ref, sa_ref, sb_ref, q_ref, gate_ref,
                     rows_t_ref, win_t_ref, ks_ref, vs_ref, kw_ref, vw_ref, *, tiles_per_seq, n_sel):
    kc, vc, ks, vs, kw, vw = _nsa_proj_parts(x_ref, g_ref, w_ref, cos_ref, sa_ref, sb_ref, q_ref, gate_ref)
    tm = kc.shape[0]
    for c, part in enumerate((kc, vc, ks, vs)):
        rows_t_ref[0, c * 256:(c + 1) * 256, :] = part.T
    win_t_ref[0, 0:256, :] = kw.T
    win_t_ref[0, 256:512, :] = vw.T
    ones = jnp.ones((NSA_KV_HEADS, VT_ROWS - HEAD_DIM, tm), BF16)
    vs_ref[0] = jnp.concatenate([vs.T.reshape(NSA_KV_HEADS, HEAD_DIM, tm).astype(BF16), ones], axis=1)
    vw_ref[0] = jnp.concatenate([vw.T.reshape(NSA_KV_HEADS, HEAD_DIM, tm).astype(BF16), ones], axis=1)
    t = (pl.program_id(0) % tiles_per_seq) * tm + lax.broadcasted_iota(jnp.int32, (tm, HEAD_DIM), 0)
    lane = lax.broadcasted_iota(jnp.int32, (tm, HEAD_DIM), 1)
    onehot = jnp.where(jnp.where(lane < n_sel, t // SEL_BLOCK, -1) == lane, 1.0, 0.0).astype(BF16)
    zeros = jnp.zeros((tm, HEAD_DIM), BF16)
    for g in range(NSA_KV_HEADS):
        sl = slice(g * HEAD_DIM, (g + 1) * HEAD_DIM)
        ks_ref[0, g] = jnp.concatenate([ks[:, sl].astype(BF16), onehot], axis=1)
        kw_ref[0, g] = jnp.concatenate([kw[:, sl].astype(BF16), zeros], axis=1)


def _nsa_proj_t(x, g, w, cos, sa, sb, b, t):
    m, d = x.shape
    nt = t // TM
    n_sel = -(-t // SEL_BLOCK)
    row = lambda i: (i, 0)
    tab = lambda i: (i % nt, 0)
    fm = lambda i: (i // nt, 0, i % nt)
    kmap = lambda i: (i // nt, 0, i % nt, 0)
    vmap = lambda i: (i // nt, 0, 0, i % nt)
    kv = NSA_KV_HEADS
    return pl.pallas_call(
        functools.partial(_nsa_proj_t_body, tiles_per_seq=nt, n_sel=n_sel),
        grid=(m // TM,),
        in_specs=[pl.BlockSpec((TM, d), row), _resident(g.shape), _resident(w.shape),
                  pl.BlockSpec((TM, LANES), tab), pl.BlockSpec((TM, LANES), tab), pl.BlockSpec((TM, LANES), tab)],
        out_specs=[pl.BlockSpec((TM, 1024), row), pl.BlockSpec((TM, LANES), row),
                   pl.BlockSpec((1, 1024, TM), fm), pl.BlockSpec((1, 512, TM), fm),
                   pl.BlockSpec((1, kv, TM, LANES), kmap), pl.BlockSpec((1, kv, VT_ROWS, TM), vmap),
                   pl.BlockSpec((1, kv, TM, LANES), kmap), pl.BlockSpec((1, kv, VT_ROWS, TM), vmap)],
        out_shape=[jax.ShapeDtypeStruct((m, 1024), BF16), jax.ShapeDtypeStruct((m, LANES), F32),
                   jax.ShapeDtypeStruct((b, 1024, t), F32), jax.ShapeDtypeStruct((b, 512, t), F32),
                   jax.ShapeDtypeStruct((b, kv, t, LANES), BF16), jax.ShapeDtypeStruct((b, kv, VT_ROWS, t), BF16),
                   jax.ShapeDtypeStruct((b, kv, t, LANES), BF16), jax.ShapeDtypeStruct((b, kv, VT_ROWS, t), BF16)],
        compiler_params=_cparams(("arbitrary",)),
        name="nsa_proj_t",
    )(x, g, w, cos, sa, sb)


def _nsa_proj(x, g, w, cos, sa, sb):
    m, d = x.shape
    ntab = cos.shape[0] // TM
    row = lambda i: (i, 0)
    tab = lambda i: (i % ntab, 0)
    return pl.pallas_call(
        _nsa_proj_body,
        grid=(m // TM,),
        in_specs=[pl.BlockSpec((TM, d), row), _resident(g.shape), _resident(w.shape),
                  pl.BlockSpec((TM, LANES), tab), pl.BlockSpec((TM, LANES), tab), pl.BlockSpec((TM, LANES), tab)],
        out_specs=[pl.BlockSpec((TM, 1024), row), pl.BlockSpec((TM, 1024), row),
                   pl.BlockSpec((TM, 512), row), pl.BlockSpec((TM, LANES), row)],
        out_shape=[jax.ShapeDtypeStruct((m, 1024), BF16), jax.ShapeDtypeStruct((m, 1024), F32),
                   jax.ShapeDtypeStruct((m, 512), F32), jax.ShapeDtypeStruct((m, LANES), F32)],
        compiler_params=_cparams(("arbitrary",)),
        name="nsa_proj",
    )(x, g, w, cos, sa, sb)


def _cmp_bias_body(pe_ref, w_ref, o_ref):
    for s in range(2):
        for half in range(2):
            acc = jnp.zeros((SUBLANES, 256), F32)
            for j in range(CMP_STRIDE):
                acc = acc + _dot(pe_ref[s, half, j].astype(BF16), w_ref[s, j, :, half * 256:(half + 1) * 256])
            o_ref[:, s * 512 + half * 256:s * 512 + (half + 1) * 256] = acc


def _cmp_bias(pe_t, wbd):
    return pl.pallas_call(_cmp_bias_body, out_shape=jax.ShapeDtypeStruct((SUBLANES, 1024), F32),
                          compiler_params=_cparams(None), name="cmp_bias")(pe_t, wbd)


CMP_HALF = CMP_ROWS // 2
CMP_PITCH = CMP_HALF + SUBLANES


def _compress_pages_body(x_ref, w_ref, b_ref, o_ref, xa_ref, xb_ref):
    per_page = LANES // CMP_STRIDE
    pages_per_block = x_ref.shape[3] // LANES
    pages_half = CMP_HALF // per_page

    def move(xs_ref, p0):
        for q in range(pages_half):
            blk, off = divmod(p0 + q, pages_per_block)
            for s in range(2):
                xt = x_ref[blk, s, :, off * LANES:(off + 1) * LANES].T
                for c in range(per_page):
                    for half in range(2):
                        xs_ref[2 * s + half, pl.ds(q * per_page + c, CMP_STRIDE, stride=CMP_PITCH), :] = (
                            xt[c * CMP_STRIDE:(c + 1) * CMP_STRIDE, half * LANES:(half + 1) * LANES])

    def project(xs_ref, r0):
        for s in range(2):
            acc = jnp.zeros((CMP_HALF, 512), F32)
            for j in range(CMP_STRIDE):
                xj = jnp.concatenate([xs_ref[2 * s, j * CMP_PITCH:j * CMP_PITCH + CMP_HALF, :],
                                      xs_ref[2 * s + 1, j * CMP_PITCH:j * CMP_PITCH + CMP_HALF, :]], axis=1)
                acc = acc + _dot(xj.astype(BF16), w_ref[s, j])
            o_ref[r0:r0 + CMP_HALF, s * 512:(s + 1) * 512] = acc + b_ref[0:1, s * 512:(s + 1) * 512]

    move(xa_ref, 0)
    move(xb_ref, pages_half)
    project(xa_ref, 0)
    project(xb_ref, CMP_HALF)


def _compress_pages(pages_t, wbd, bias):
    n, _, _, npos = pages_t.shape
    chunks = npos // CMP_STRIDE
    pp = CMP_ROWS // chunks
    return pl.pallas_call(
        _compress_pages_body,
        grid=(n // pp,),
        in_specs=[pl.BlockSpec((pp, 2, 256, npos), lambda i: (i, 0, 0, 0)),
                  _resident(wbd.shape), _resident(bias.shape)],
        out_specs=pl.BlockSpec((CMP_ROWS, 1024), lambda i: (i, 0)),
        out_shape=jax.ShapeDtypeStruct((n * chunks, 1024), F32),
        scratch_shapes=[pltpu.VMEM((4, CMP_STRIDE * CMP_PITCH, LANES), F32),
                        pltpu.VMEM((4, CMP_STRIDE * CMP_PITCH, LANES), F32)],
        compiler_params=_cparams(("arbitrary",)),
        name="compress_pages",
    )(pages_t, wbd, bias)


def _finish_compress(fs):
    n = fs.shape[0]
    kc = fs[:, 0:256] + pltpu.roll(fs[:, 256:512], n - 1, axis=0)
    vc = fs[:, 512:768] + pltpu.roll(fs[:, 768:1024], n - 1, axis=0)
    return kc.astype(BF16), vc.astype(BF16)


def _select_mask(sc_raw, jr, pos, n_real):
    sc = jnp.where(jr * SEL_BLOCK > pos, -FORCE, sc_raw)
    sc = jnp.where(jr == pos // SEL_BLOCK, FORCE, jnp.where(jr == 0, FORCE, sc))
    sc = jnp.where(jr >= n_real, -2.0 * FORCE, sc)
    cnt = jnp.zeros(sc.shape, F32)
    for i in range(n_real):
        ri = sc[i:i + 1, :]
        ge = jnp.where(ri >= sc, 1.0, 0.0)
        gt = jnp.where(ri > sc, 1.0, 0.0)
        cnt = cnt + jnp.where(jr > i, ge, gt)
    keep = jnp.where(sc > -0.5 * FORCE, 0.0, NEG)
    return jnp.where(cnt < float(min(TOP_N, n_real)), keep, NEG)


def _cmp_prompt_body(q_ref, fs_ref, ov_ref, oc_ref, qp_ref):
    i = pl.program_id(1)
    tq = q_ref.shape[0]
    nblk = fs_ref.shape[0]
    fs = fs_ref[...]
    kc = (fs[:, 0:256] + pltpu.roll(fs[:, 256:512], nblk - 1, axis=0)).astype(BF16)
    vc_t = (fs[:, 512:768] + pltpu.roll(fs[:, 768:1024], nblk - 1, axis=0)).T.astype(BF16)
    t_row = i * tq + lax.broadcasted_iota(jnp.int32, (nblk, tq), 1)
    n_col = lax.broadcasted_iota(jnp.int32, (nblk, tq), 0)
    vis_t = n_col * CMP_STRIDE + 2 * CMP_STRIDE - 1 <= t_row
    nsel = ov_ref.shape[0]
    jr = lax.broadcasted_iota(jnp.int32, (nsel, tq), 0)
    pos = i * tq + lax.broadcasted_iota(jnp.int32, (nsel, tq), 1)
    eye = jnp.where(lax.broadcasted_iota(jnp.int32, (tq, tq), 0) == lax.broadcasted_iota(jnp.int32, (tq, tq), 1),
                    1.0, 0.0).astype(BF16)
    for g in range(NSA_KV_HEADS):
        kg = kc[:, g * HEAD_DIM:(g + 1) * HEAD_DIM]
        vg_t = vc_t[g * HEAD_DIM:(g + 1) * HEAD_DIM, :]
        ps_t = jnp.zeros((nblk, tq), F32)
        qs = []
        for r in range(NSA_GROUP):
            hd = g * NSA_GROUP + r
            qh = q_ref[:, hd * HEAD_DIM:(hd + 1) * HEAD_DIM]
            qs.append(qh)
            p_t = _softmax_masked(_nt(kg, qh), vis_t, 0)
            oc_ref[0, hd * HEAD_DIM:(hd + 1) * HEAD_DIM, :] = _dot(vg_t, p_t.astype(BF16)).astype(oc_ref.dtype)
            ps_t = ps_t + p_t
        a_t = _select_mask(_split_dot(ov_ref[...], ps_t), jr, pos, nsel)
        a_t = jnp.concatenate([a_t, jnp.zeros((HEAD_DIM - nsel, tq), F32)], axis=0).astype(BF16)
        a = _nt(eye, a_t).astype(BF16)
        for r in range(NSA_GROUP):
            qp_ref[0, g * NSA_GROUP + r] = jnp.concatenate([qs[r], a], axis=1)


def _cmp_prompt(q, fs, ov_t, b, t):
    nq = t // TQ
    nblk = t // CMP_STRIDE
    return pl.pallas_call(
        _cmp_prompt_body,
        grid=(b, nq),
        in_specs=[pl.BlockSpec((TQ, 1024), lambda bi, i: (bi * nq + i, 0)),
                  pl.BlockSpec((nblk, 1024), lambda bi, i: (bi, 0)),
                  _resident(ov_t.shape)],
        out_specs=[pl.BlockSpec((1, 1024, TQ), lambda bi, i: (bi, 0, i)),
                   pl.BlockSpec((1, NSA_HEADS, TQ, LANES), lambda bi, i: (bi, 0, i, 0))],
        out_shape=[jax.ShapeDtypeStruct((b, 1024, t), BF16),
                   jax.ShapeDtypeStruct((b, NSA_HEADS, t, LANES), BF16)],
        compiler_params=_cparams(("arbitrary", "arbitrary")),
        name="cmp_prompt",
    )(q, fs, ov_t)


FLASH_RUN = 4


def _flash_body(qp_ref, ks_ref, vs_ref, kw_ref, vw_ref, os_ref, ow_ref, st_ref, acc_ref, *, window):
    _flash_branch(qp_ref, ks_ref, vs_ref, os_ref, st_ref, acc_ref, window=None)
    _flash_branch(qp_ref, kw_ref, vw_ref, ow_ref, st_ref, acc_ref, window=window)


def _flash_branch(qp_ref, kp_ref, vt_ref, o_ref, st_ref, acc_ref, *, window):
    i = pl.program_id(2)
    tq = qp_ref.shape[2]
    dq = (i * tq + lax.broadcasted_iota(jnp.int32, (tq, tq), 1)) - lax.broadcasted_iota(jnp.int32, (tq, tq), 0)

    def run(c, masks, m):
        n = len(masks)
        off = pl.multiple_of(c * tq, tq)
        vt = vt_ref[0, 0, :, pl.ds(off, n * tq)]
        out = list(m)
        for u, (causal, far) in enumerate(masks):
            k = kp_ref[0, 0, pl.ds(off + u * tq, tq), :]
            d = dq - (c + u) * tq
            for r in range(NSA_GROUP):
                s = _nt(k, qp_ref[0, r])
                if causal:
                    s = jnp.where(d >= 0, s, NEG)
                if far:
                    s = jnp.where(d <= window, s, NEG)
                st_ref[r, u * tq:(u + 1) * tq, :] = s
                out[r] = jnp.maximum(out[r], jnp.max(s, axis=0, keepdims=True))
        for r in range(NSA_GROUP):
            p = jnp.exp2(st_ref[r, 0:n * tq, :] - out[r]).astype(BF16)
            acc_ref[r] = jnp.exp2(m[r] - out[r]) * acc_ref[r] + _dot(vt, p)
        return tuple(out)

    def by_runs(lo, hi, mask, init):
        n = hi - lo
        carry = lax.fori_loop(0, n // FLASH_RUN, lambda j, m: run(lo + FLASH_RUN * j, (mask,) * FLASH_RUN, m), init)
        c2 = lo + (n // FLASH_RUN) * FLASH_RUN
        carry = lax.cond(n % FLASH_RUN >= 2, lambda m: run(c2, (mask,) * 2, m), lambda m: m, carry)
        return lax.cond(n % 2 == 1, lambda m: run(hi - 1, (mask,), m), lambda m: m, carry)

    acc_ref[...] = jnp.zeros(acc_ref.shape, F32)
    mx = tuple(jnp.full((1, tq), NEG, F32) for _ in range(NSA_GROUP))
    if window is None:
        by_runs(0, i + 1, (True, False), mx)
    else:
        back = window // tq
        steady = ((False, True),) + ((False, False),) * (back - 1) + ((True, False),)
        lax.cond(i >= back, lambda m: run(i - back, steady, m),
                 lambda m: by_runs(0, i + 1, (True, True), m), mx)
    for r in range(NSA_GROUP):
        acc = acc_ref[r]
        o_ref[0, r * HEAD_DIM:(r + 1) * HEAD_DIM, :] = (acc[0:HEAD_DIM] / acc[HEAD_DIM:HEAD_DIM + 1]).astype(o_ref.dtype)


def _flash(qp, k_slc, vt_slc, k_win, vt_win, window):
    b, _, t, _ = qp.shape
    nq = t // TQ
    assert window % TQ == 0 and window // TQ + 1 <= FLASH_RUN
    kspec = pl.BlockSpec((1, 1, t, LANES), lambda bi, g, i: (bi, g, 0, 0))
    vspec = pl.BlockSpec((1, 1, VT_ROWS, t), lambda bi, g, i: (bi, g, 0, 0))
    ospec = pl.BlockSpec((1, NSA_GROUP * HEAD_DIM, TQ), lambda bi, g, i: (bi, g, i))
    oshape = jax.ShapeDtypeStruct((b, NSA_HEADS * HEAD_DIM, t), BF16)
    return pl.pallas_call(
        functools.partial(_flash_body, window=window),
        grid=(b, NSA_KV_HEADS, nq),
        in_specs=[pl.BlockSpec((1, NSA_GROUP, TQ, LANES), lambda bi, g, i: (bi, g, i, 0)), kspec, vspec, kspec, vspec],
        out_specs=[ospec, ospec], out_shape=[oshape, oshape],
        scratch_shapes=[pltpu.VMEM((NSA_GROUP, FLASH_RUN * TQ, TQ), F32), pltpu.VMEM((NSA_GROUP, VT_ROWS, TQ), F32)],
        compiler_params=_cparams(("arbitrary", "arbitrary", "arbitrary")),
        name="flash",
    )(qp, k_slc, vt_slc, k_win, vt_win)


def _diag_blocks(o_full, o_ref, s=0):
    rows = o_ref.shape[1] // NSA_KV_HEADS
    for g in range(NSA_KV_HEADS):
        o_ref[s, g * rows:(g + 1) * rows, :] = o_full[g * rows:(g + 1) * rows, g * HEAD_DIM:(g + 1) * HEAD_DIM]


def _cmp_sample_body(pt_ref, *refs, n_pages, past_len, dec, sb):
    fs_refs = refs[:sb * n_pages]
    qt_ref, ov_ref, msum_ref, oc_ref, at_ref = refs[sb * n_pages:]
    nq = qt_ref.shape[1]
    nblk = n_pages * fs_refs[0].shape[0]
    nj = ov_ref.shape[0]
    pos_l = past_len + (lax.broadcasted_iota(jnp.int32, (nblk, nq), 1) // NSA_GROUP) % dec
    n_c = lax.broadcasted_iota(jnp.int32, (nblk, nq), 0)
    vis_t = jnp.logical_and(n_c * CMP_STRIDE + 2 * CMP_STRIDE - 1 <= pos_l, n_c < nblk - 1)
    jr = lax.broadcasted_iota(jnp.int32, (nj, nq), 0)
    pos = past_len + (lax.broadcasted_iota(jnp.int32, (nj, nq), 1) // NSA_GROUP) % dec
    n_sel = -(-(past_len + dec) // SEL_BLOCK)
    for s in range(sb):
        fs = jnp.concatenate([r[...] for r in fs_refs[s * n_pages:(s + 1) * n_pages]], axis=0)
        kc, vc = _finish_compress(fs)
        qt = qt_ref[s]
        p_t = _softmax_masked(_nt(kc, qt), vis_t, 0)
        _diag_blocks(_tn(p_t.astype(BF16), vc), oc_ref, s)
        a1 = _split_dot(ov_ref[...], p_t)
        hi = a1.astype(BF16)
        lo = (a1 - hi.astype(F32)).astype(BF16)
        sc = _dot(hi, msum_ref[...]) + _dot(lo, msum_ref[...])
        at_ref[s] = _select_mask(sc, jr, pos, n_sel)


def _cmp_sample(page_table, fs_phys, qt, ov_t, msum, past_len, dec):
    nb, n_pages = page_table.shape
    sb = CMP_SEQS
    nj = ov_t.shape[0]
    fs_specs =[pl.BlockSpec((SUBLANES, 1024), functools.partial(lambda b, pt, s, k: (pt[b * sb + s, k], 0), s=s, k=k))
                for s in range(sb) for k in range(n_pages)]
    grid_spec = pltpu.PrefetchScalarGridSpec(
        num_scalar_prefetch=1,
        grid=(nb // sb,),
        in_specs=fs_specs + [pl.BlockSpec((sb, LANES, 256), lambda b, pt: (b, 0, 0)),
                             pl.BlockSpec(ov_t.shape, lambda b, pt: (0, 0)),
                             pl.BlockSpec(msum.shape, lambda b, pt: (0, 0))],
        out_specs=[pl.BlockSpec((sb, 64, HEAD_DIM), lambda b, pt: (b, 0, 0)),
                   pl.BlockSpec((sb, nj, LANES), lambda b, pt: (b, 0, 0))],
    )
    return pl.pallas_call(
        functools.partial(_cmp_sample_body, n_pages=n_pages, past_len=past_len, dec=dec, sb=sb),
        grid_spec=grid_spec,
        out_shape=[jax.ShapeDtypeStruct((nb, 64, HEAD_DIM), F32),
                   jax.ShapeDtypeStruct((nb, nj, LANES), F32)],
        compiler_params=_cparams(("arbitrary",)),
        name="cmp_sample",
    )(page_table, *([fs_phys] * (sb * n_pages)), qt, ov_t, msum)


def _attn_sample_body(*refs, n_blocks, sb, dec, has_pt, has_bm, has_pm, emit_state):
    refs = list(refs[1:] if has_pt else refs)
    n_kv = sb * n_blocks if has_pt else 1
    kv_refs = refs[:n_kv]
    rest = refs[n_kv:]
    new_ref, qt_ref = rest[0], rest[1]
    rest = rest[2:]
    bm_ref = rest.pop(0) if has_bm else None
    pm_ref = rest.pop(0) if has_pm else None
    nm_ref, o_ref = rest[0], rest[1]
    so_ref = rest[2] if emit_state else None
    st_ref = rest[-1]
    nq = qt_ref.shape[2]
    rows = kv_refs[0].shape[3]
    per = rows // SEL_BLOCK

    def kv(s, kb, which):
        return kv_refs[s * n_blocks + kb][0, which] if has_pt else kv_refs[0][s, which]

    for s in range(sb):
        qbd = qt_ref[s]
        m = jnp.full((1, nq), NEG, F32)
        for kb in range(n_blocks):
            sc = _tn(kv(s, kb, 0).astype(BF16), qbd)
            if has_pm:
                sc = sc + pm_ref[kb * rows:(kb + 1) * rows, :]
            if has_bm:
                sc = sc + jnp.concatenate(
                    [jnp.broadcast_to(bm_ref[s, kb * per + u:kb * per + u + 1, :], (SEL_BLOCK, nq)) for u in range(per)],
                    axis=0)
            st_ref[s, kb * rows:(kb + 1) * rows, :] = sc
            m = jnp.maximum(m, jnp.max(sc, axis=0, keepdims=True))
        s_new = _dot(new_ref[s, :, 0:256].astype(BF16), qbd) + nm_ref[...]
        if has_bm:
            s_new = s_new + bm_ref[s, n_blocks * per:n_blocks * per + 1, :]
        m = jnp.maximum(m, jnp.max(s_new, axis=0, keepdims=True))
        e_new = jnp.exp2(s_new - m)
        l = jnp.sum(e_new, axis=0, keepdims=True)
        acc = _tn(new_ref[s, :, 256:512].astype(BF16), e_new.astype(BF16))
        for kb in range(n_blocks):
            e = jnp.exp2(st_ref[s, kb * rows:(kb + 1) * rows, :] - m)
            l = l + jnp.sum(e, axis=0, keepdims=True)
            acc = acc + _dot(kv(s, kb, 1).astype(BF16), e.astype(BF16))
        o_ref[s] = acc / jnp.maximum(l, 1e-30)
        if emit_state:
            lane = lax.broadcasted_iota(jnp.int32, (256, LANES), 1)
            tail = pltpu.roll(new_ref[s], SUBLANES - dec, axis=0)
            for which in range(2):
                shifted = pltpu.roll(kv(s, 0, which), rows - dec, axis=1)
                fill = jnp.concatenate([jnp.zeros((LANES - SUBLANES, 256), F32),
                                        tail[:, which * 256:(which + 1) * 256]], axis=0).T
                last = jnp.where(lane < LANES - dec, shifted[:, rows - LANES:], fill)
                so_ref[s, which, :, 0:rows - LANES] = shifted[:, 0:rows - LANES]
                so_ref[s, which, :, rows - LANES:rows] = last


def _attn_sample(kv, slot_blk, n_blocks, new, qt, new_mask, sb, dec,
                 page_table=None, blk_mask=None, past_mask=None, emit_state=False):
    nb = qt.shape[0]
    kv_rows = kv.shape[3]
    has_pt = page_table is not None
    has_bm = blk_mask is not None
    has_pm = past_mask is not None
    if has_pt:
        kv_specs = [pl.BlockSpec((1, 2, 256, kv_rows),
                                 functools.partial(lambda b, pt, s, k: (pt[b * sb + s, k], slot_blk, 0, 0), s=s, k=k))
                    for s in range(sb) for k in range(n_blocks)]
        im = lambda b, pt: (b, 0, 0)
        im4 = lambda b, pt: (b, 0, 0, 0)
        cm = lambda b, pt: (0, 0)
    else:
        kv_specs = [pl.BlockSpec((sb, 2, 256, kv_rows), lambda b: (b, slot_blk, 0, 0))]
        im = lambda b: (b, 0, 0)
        im4 = lambda b: (b, 0, 0, 0)
        cm = lambda b: (0, 0)
    in_specs = kv_specs + [pl.BlockSpec((sb, SUBLANES, 512), im), pl.BlockSpec((sb, 256, LANES), im)]
    args = [kv] * len(kv_specs) + [new, qt]
    if has_bm:
        in_specs.append(pl.BlockSpec((sb,) + blk_mask.shape[1:], im))
        args.append(blk_mask)
    if has_pm:
        in_specs.append(pl.BlockSpec(past_mask.shape, cm))
        args.append(past_mask)
    in_specs.append(pl.BlockSpec(new_mask.shape, cm))
    args.append(new_mask)
    out_specs = [pl.BlockSpec((sb, 256, LANES), im)]
    out_shape = [jax.ShapeDtypeStruct((nb, 256, LANES), F32)]
    if emit_state:
        out_specs.append(pl.BlockSpec((sb, 2, 256, kv_rows), im4))
        out_shape.append(jax.ShapeDtypeStruct((nb, 2, 256, kv_rows), F32))
    scratch = [pltpu.VMEM((sb, n_blocks * kv_rows, LANES), F32)]
    body = functools.partial(_attn_sample_body, n_blocks=n_blocks, sb=sb, dec=dec, has_pt=has_pt, has_bm=has_bm,
                             has_pm=has_pm, emit_state=emit_state)
    if has_pt:
        gs = pltpu.PrefetchScalarGridSpec(num_scalar_prefetch=1, grid=(nb // sb,), in_specs=in_specs,
                                          out_specs=out_specs, scratch_shapes=scratch)
        return pl.pallas_call(body, grid_spec=gs, out_shape=out_shape, compiler_params=_cparams(("arbitrary",)),
                              name="slc_sample")(page_table, *args)
    return pl.pallas_call(body, grid=(nb // sb,), in_specs=in_specs, out_specs=out_specs, scratch_shapes=scratch,
                          out_shape=out_shape, compiler_params=_cparams(("arbitrary",)), name="win_sample")(*args)


def _mix_project(o_refs, gate_ref, w_ref, tm, feature_major):
    n_in = len(o_refs)
    gated = gate_ref is not None
    if feature_major:
        gates_t = gate_ref[...].T
        heads = []
        for hd in range(NSA_HEADS):
            acc = jnp.zeros((HEAD_DIM, tm), F32)
            for kbr in range(n_in):
                row = gates_t[hd * 3 + kbr:hd * 3 + kbr + 1, :]
                acc = acc + row * o_refs[kbr][0, hd * HEAD_DIM:(hd + 1) * HEAD_DIM, :].astype(F32)
            heads.append(acc.astype(BF16))
        return _tn(jnp.concatenate(heads, axis=0), w_ref[...])
    if gated:
        lane = lax.broadcasted_iota(jnp.int32, (tm, LANES), 1)
        gates = gate_ref[...]
        chunks = []
        for c in range(1024 // LANES):
            acc = jnp.zeros((tm, LANES), F32)
            for kbr in range(n_in):
                c0 = (2 * c) * 3 + kbr
                c1 = (2 * c + 1) * 3 + kbr
                gexp = jnp.where(lane < HEAD_DIM, gates[:, c0:c0 + 1], gates[:, c1:c1 + 1])
                acc = acc + gexp * o_refs[kbr][:, c * LANES:(c + 1) * LANES].astype(F32)
            chunks.append(acc.astype(BF16))
        o = jnp.concatenate(chunks, axis=1)
    else:
        o = o_refs[0][...].astype(BF16)
    return _dot(o, w_ref[...])


def _ffn_body(*refs, sample, tiles_per_seq, nb, n_in, gated, feature_major):
    o_refs = refs[:n_in]
    refs = refs[n_in:]
    gate_ref = None
    if gated:
        gate_ref, refs = refs[0], refs[1:]
    wo_ref, g1_ref, refs = refs[0], refs[1], refs[2:]
    if sample:
        x_ref, g2_ref, g3_ref, wu_ref, cw_ref, cb_ref, wd_ref, buf_ref, y_ref, tail_ref = refs
    else:
        x_ref, g2_ref, g3_ref, wu_ref, cw_ref, cb_ref, wd_ref, y_ref, tail_ref, carry_ref = refs
    tm = x_ref.shape[0]
    x = x_ref[...] + _rms(_mix_project(o_refs, gate_ref, wo_ref, tm, feature_major), g1_ref[...])
    f = wd_ref.shape[0]
    h = _rms(x, g2_ref[...]).astype(BF16)
    acc = jnp.zeros(x.shape, F32)
    if not sample:
        first = pl.program_id(0) % tiles_per_seq == 0
        row = lax.broadcasted_iota(jnp.int32, (tm, FFN_FCHUNK), 0)
    for fc in range(f // FFN_FCHUNK):
        sl = slice(fc * FFN_FCHUNK, (fc + 1) * FFN_FCHUNK)
        gate = _dot(h, wu_ref[:, sl])
        val = _dot(h, wu_ref[:, f + fc * FFN_FCHUNK:f + (fc + 1) * FFN_FCHUNK])
        if sample:
            prev1 = jnp.concatenate([buf_ref[nb:2 * nb, sl], gate[0:tm - nb]], axis=0)
            prev2 = jnp.concatenate([buf_ref[0:2 * nb, sl], gate[0:tm - 2 * nb]], axis=0)
            tail_ref[:, sl] = gate[tm - 2 * nb:tm]
        else:
            carry = carry_ref[:, sl]
            c6 = jnp.where(first, 0.0, carry[6:7, :])
            c7 = jnp.where(first, 0.0, carry[7:8, :])
            prev1 = jnp.where(row == 0, c7, pltpu.roll(gate, 1, axis=0))
            prev2 = jnp.where(row == 0, c6, jnp.where(row == 1, c7, pltpu.roll(gate, 2, axis=0)))
            carry_ref[:, sl] = gate[tm - SUBLANES:tm]
            tail_ref[:, sl] = gate[tm - SUBLANES:tm]
        conv = cb_ref[0:1, sl] + cw_ref[0:1, sl] * prev2 + cw_ref[1:2, sl] * prev1 + cw_ref[2:3, sl] * gate
        act = (jax.nn.gelu(conv) * val).astype(BF16)
        acc = acc + _dot(act, wd_ref[sl, :])
    y_ref[...] = x + _rms(acc, g3_ref[...])


def _ffn(x, mix, g, wu, cw, cb, wd, seq_len=None, buf=None):
    m, d = x.shape
    f = wd.shape[0]
    os_, gates, w_o, feature_major = mix
    sample = buf is not None
    row = lambda i: (i, 0)
    if sample:
        tm = m
        nb = buf.shape[0] // 2
        tail_shape, tail_block = (2 * nb, f), (2 * nb, f)
        scratch = []
    else:
        tm = FFN_TM
        nb = 0
        tail_shape, tail_block = (m // tm * SUBLANES, f), (SUBLANES, f)
        scratch = [pltpu.VMEM((SUBLANES, f), F32)]
    if feature_major:
        nt = seq_len // tm
        in_specs = [pl.BlockSpec((1, 1024, tm), lambda i: (i // nt, 0, i % nt)) for _ in os_]
    else:
        in_specs = [pl.BlockSpec((tm, 1024), row) for _ in os_]
    args = list(os_)
    if gates is not None:
        in_specs.append(pl.BlockSpec((tm, LANES), row))
        args.append(gates)
    g1, g2, g3 = g[1:2], g[2:3], g[3:4]
    in_specs += [_resident(w_o.shape), _resident(g1.shape),
                 pl.BlockSpec((tm, d), row), _resident(g2.shape), _resident(g3.shape), _resident(wu.shape),
                 _resident(cw.shape), _resident(cb.shape), _resident(wd.shape)]
    args += [w_o, g1, x, g2, g3, wu, cw, cb, wd]
    if sample:
        in_specs.append(_resident(buf.shape))
        args.append(buf)
    return pl.pallas_call(
        functools.partial(_ffn_body, sample=sample, tiles_per_seq=(seq_len // tm if not sample else 1), nb=nb,
                          n_in=len(os_), gated=gates is not None, feature_major=feature_major),
        grid=(m // tm,), in_specs=in_specs,
        out_specs=[pl.BlockSpec((tm, d), row), pl.BlockSpec(tail_block, row)],
        out_shape=[jax.ShapeDtypeStruct((m, d), F32), jax.ShapeDtypeStruct(tail_shape, F32)],
        scratch_shapes=scratch,
        compiler_params=_cparams(("arbitrary",)), name="ffn_sample" if sample else "ffn_prompt",
    )(*args)


def _gla_proj_body(x_ref, g_ref, w_ref, wg_ref, bg_ref, q_ref, k_ref, v_ref, sr_ref, la_ref):
    h = _rms(x_ref[...], g_ref[...]).astype(BF16)
    nk = GLA_HEADS * GLA_DK
    nv = GLA_HEADS * GLA_DV
    for c in range(nk // 256):
        q_ref[:, c * 256:(c + 1) * 256] = _dot(h, w_ref[:, c * 256:(c + 1) * 256]) * (GLA_DK ** -0.5)
        k_ref[:, c * 256:(c + 1) * 256] = _dot(h, w_ref[:, nk + c * 256:nk + (c + 1) * 256])
    for c in range(nv // 256):
        v_ref[:, c * 256:(c + 1) * 256] = _dot(h, w_ref[:, 2 * nk + c * 256:2 * nk + (c + 1) * 256]).astype(BF16)
        r = _dot(h, w_ref[:, 2 * nk + nv + c * 256:2 * nk + nv + (c + 1) * 256])
        sr_ref[:, c * 256:(c + 1) * 256] = jax.nn.silu(r).astype(BF16)
    low = _dot(h, w_ref[:, 2 * nk + 2 * nv:2 * nk + 2 * nv + LANES]).astype(BF16)
    gz = _dot(low, wg_ref[...]) + bg_ref[...]
    log_sig = jnp.minimum(gz, 0.0) - jnp.log1p(jnp.exp(-jnp.abs(gz)))
    la_ref[...] = log_sig / GLA_GATE_TEMP


def _gla_proj(x, g, w, wg, bg):
    m, d = x.shape
    nk = GLA_HEADS * GLA_DK
    nv = GLA_HEADS * GLA_DV
    row = lambda i: (i, 0)
    return pl.pallas_call(
        _gla_proj_body, grid=(m // TM,),
        in_specs=[pl.BlockSpec((TM, d), row), _resident(g.shape), _resident(w.shape),
                  _resident(wg.shape), _resident(bg.shape)],
        out_specs=[pl.BlockSpec((TM, nk), row), pl.BlockSpec((TM, nk), row), pl.BlockSpec((TM, nv), row),
                   pl.BlockSpec((TM, nv), row), pl.BlockSpec((TM, nk), row)],
        out_shape=[jax.ShapeDtypeStruct((m, nk), F32), jax.ShapeDtypeStruct((m, nk), F32),
                   jax.ShapeDtypeStruct((m, nv), BF16), jax.ShapeDtypeStruct((m, nv), BF16),
                   jax.ShapeDtypeStruct((m, nk), F32)],
        compiler_params=_cparams(("arbitrary",)), name="gla_proj",
    )(x, g, w, wg, bg)


def _cumsum_rows(x):
    n = x.shape[0]
    row = lax.broadcasted_iota(jnp.int32, x.shape, 0)
    sh = 1
    while sh < n:
        x = x + jnp.where(row >= sh, pltpu.roll(x, sh, axis=0), 0.0)
        sh *= 2
    return x


def _gla_chunk(q, k, la, v, s_old):
    cs = q.shape[0]
    row = lax.broadcasted_iota(jnp.int32, (cs, 1), 0)
    trow = lax.broadcasted_iota(jnp.int32, (cs, cs), 0)
    scol = lax.broadcasted_iota(jnp.int32, (cs, cs), 1)
    r8 = lax.broadcasted_iota(jnp.int32, (SUBLANES, 1), 0)
    lane8 = lax.broadcasted_iota(jnp.int32, (SUBLANES, cs), 1)
    if True:
        cum = _cumsum_rows(la)
        last = cum[cs - 1:cs, :]
        out = _dot((q * jnp.exp(cum)).astype(BF16), s_old.astype(BF16))
        att = None
        hh = cs // 2
        while hh >= SUBLANES:
            nblk = cs // (2 * hh)
            ref = jnp.concatenate(
                [jnp.broadcast_to(cum[u * 2 * hh + hh - 1:u * 2 * hh + hh, :], (2 * hh, GLA_DK)) for u in range(nblk)],
                axis=0) if nblk > 1 else jnp.broadcast_to(cum[hh - 1:hh, :], (cs, GLA_DK))
            second = (row % (2 * hh)) >= hh
            qh = jnp.where(second, q * jnp.exp(jnp.minimum(cum - ref, 0.0)), 0.0).astype(BF16)
            kh = jnp.where(second, 0.0, k * jnp.exp(jnp.minimum(ref - cum, 0.0))).astype(BF16)
            a = _nt(qh, kh)
            if nblk > 1:
                a = jnp.where(trow // (2 * hh) == scol // (2 * hh), a, 0.0)
            att = a if att is None else att + a
            hh //= 2
        slabs = []
        for g8 in range(cs // SUBLANES):
            r0 = g8 * SUBLANES
            cg, qg, kg = cum[r0:r0 + SUBLANES], q[r0:r0 + SUBLANES], k[r0:r0 + SUBLANES]
            slab = jnp.zeros((SUBLANES, GLA_DV if att is None else cs), F32)
            for s in range(SUBLANES):
                e = jnp.exp(jnp.minimum(cg - cg[s:s + 1, :], 0.0))
                col = jnp.sum(e * qg * kg[s:s + 1, :], axis=1, keepdims=True)
                col = jnp.where(r8 >= s, col, 0.0)
                if att is None:
                    slab = slab + col * v[r0 + s:r0 + s + 1, :].astype(F32)
                else:
                    slab = jnp.where(lane8 == r0 + s, col, slab)
            slabs.append(slab)
        if att is None:
            out = out + slabs[0]
        else:
            att = att + jnp.concatenate(slabs, axis=0)
            out = out + _dot(att.astype(BF16), v)
        kt = (k * jnp.exp(last - cum)).astype(BF16)
        dcol = jnp.broadcast_to(jnp.exp(last), (SUBLANES, GLA_DK)).T[:, 0:1]
        return out, dcol * s_old + _tn(kt, v)


def _gla_rec_prompt_body(q_ref, k_ref, la_ref, v_ref, sr_ref, gn_ref, o_ref, so_ref, st_ref):
    c = pl.program_id(1)

    @pl.when(c == 0)
    def _():
        st_ref[...] = jnp.zeros(st_ref.shape, F32)

    for hd in range(GLA_HEADS):
        ksl = slice(hd * GLA_DK, (hd + 1) * GLA_DK)
        vsl = slice(hd * GLA_DV, (hd + 1) * GLA_DV)
        out, s_new = _gla_chunk(q_ref[:, ksl], k_ref[:, ksl], la_ref[:, ksl], v_ref[:, vsl], st_ref[hd])
        st_ref[hd] = s_new
        o_ref[:, vsl] = (_rms(out, gn_ref[...]) * sr_ref[:, vsl].astype(F32)).astype(o_ref.dtype)

    @pl.when(c == pl.num_programs(1) - 1)
    def _():
        so_ref[0] = st_ref[...]


def _gla_rec_sample_body(q_ref, k_ref, la_ref, v_ref, sr_ref, gn_ref, s0_ref, o_ref, so_ref, *, sb):
    for s in range(sb):
        rsl = slice(s * SUBLANES, (s + 1) * SUBLANES)
        for hd in range(GLA_HEADS):
            ksl = slice(hd * GLA_DK, (hd + 1) * GLA_DK)
            vsl = slice(hd * GLA_DV, (hd + 1) * GLA_DV)
            out, s_new = _gla_chunk(q_ref[rsl, ksl], k_ref[rsl, ksl], la_ref[rsl, ksl], v_ref[rsl, vsl], s0_ref[s, hd])
            so_ref[s, hd] = s_new
            o_ref[rsl, vsl] = (_rms(out, gn_ref[...]) * sr_ref[rsl, vsl].astype(F32)).astype(o_ref.dtype)


def _gla_rec(q, k, la, v, sr, gn, nb, chunk, s0=None):
    m = q.shape[0]
    nk = GLA_HEADS * GLA_DK
    nv = GLA_HEADS * GLA_DV
    out_shape = [jax.ShapeDtypeStruct((m, nv), BF16), jax.ShapeDtypeStruct((nb, GLA_HEADS, GLA_DK, GLA_DV), F32)]
    if s0 is None:
        nc = m // nb // chunk
        row = lambda b, c: (b * nc + c, 0)
        st_spec = pl.BlockSpec((1, GLA_HEADS, GLA_DK, GLA_DV), lambda b, c: (b, 0, 0, 0))
        return pl.pallas_call(
            _gla_rec_prompt_body, grid=(nb, nc),
            in_specs=[pl.BlockSpec((chunk, nk), row), pl.BlockSpec((chunk, nk), row), pl.BlockSpec((chunk, nk), row),
                      pl.BlockSpec((chunk, nv), row), pl.BlockSpec((chunk, nv), row),
                      pl.BlockSpec(gn.shape, lambda b, c: (0, 0))],
            out_specs=[pl.BlockSpec((chunk, nv), row), st_spec], out_shape=out_shape,
            scratch_shapes=[pltpu.VMEM((GLA_HEADS, GLA_DK, GLA_DV), F32)],
            compiler_params=_cparams(("arbitrary", "arbitrary")), name="gla_rec_prompt",
        )(q, k, la, v, sr, gn)
    sb = GLA_SEQS
    rows = sb * chunk
    row = lambda b: (b, 0)
    st_spec = pl.BlockSpec((sb, GLA_HEADS, GLA_DK, GLA_DV), lambda b: (b, 0, 0, 0))
    return pl.pallas_call(
        functools.partial(_gla_rec_sample_body, sb=sb), grid=(nb // sb,),
        in_specs=[pl.BlockSpec((rows, nk), row), pl.BlockSpec((rows, nk), row), pl.BlockSpec((rows, nk), row),
                  pl.BlockSpec((rows, nv), row), pl.BlockSpec((rows, nv), row),
                  pl.BlockSpec(gn.shape, lambda b: (0, 0)), st_spec],
        out_specs=[pl.BlockSpec((rows, nv), row), st_spec], out_shape=out_shape,
        compiler_params=_cparams(("arbitrary",)), name="gla_rec_sample",
    )(q, k, la, v, sr, gn, s0)


def _rope_tables(pos):
    half = HEAD_DIM // 2
    inv = ROPE_THETA ** (-jnp.arange(half, dtype=F32) / half)
    ang = pos.astype(F32)[:, None] * inv[None, :]
    cos, sin = jnp.cos(ang), jnp.sin(ang)
    z = jnp.zeros_like(sin)
    cos_t = jnp.tile(cos, (1, 4))
    sa = jnp.tile(jnp.concatenate([-sin, z], axis=1), (1, 2))
    sb = jnp.tile(jnp.concatenate([z, sin], axis=1), (1, 2))
    return cos_t, sa, sb


def _pad_cols(w, n):
    return jnp.pad(w, ((0, 0), (0, n - w.shape[1])))


def _cmp_weights(cmp_pe, cmp_w):
    g = NSA_KV_HEADS
    w = cmp_w.reshape(2, 2, CMP_STRIDE, HEAD_DIM, HEAD_DIM)
    eye = jnp.eye(g, dtype=F32)
    wbd = jnp.einsum('shjde,gk->sjgdhke', w, eye)
    wbd = wbd.reshape(2, CMP_STRIDE, g * HEAD_DIM, 2 * g * HEAD_DIM).astype(BF16)
    pe = cmp_pe.reshape(2, 2, CMP_STRIDE, 1, 1, HEAD_DIM)
    pe_t = jnp.broadcast_to(pe, (2, 2, CMP_STRIDE, SUBLANES, g, HEAD_DIM)).reshape(2, 2, CMP_STRIDE, SUBLANES, g * HEAD_DIM)
    return wbd, pe_t


def _overlap_t(n_blk, n_sel, rows, cols):
    i = jnp.arange(cols)[None, :]
    j = jnp.arange(rows)[:, None]
    ov = (i * CMP_STRIDE + 2 * CMP_STRIDE > j * SEL_BLOCK) & (i * CMP_STRIDE < (j + 1) * SEL_BLOCK)
    ov = ov & (i < n_blk) & (j < n_sel)
    return ov.astype(BF16)


def _nsa_layer_prompt(x, g, w_in, wbd, cbias, w_o, tabs, b, t):
    q, gates, rows_t, win_t, k_slc, vt_slc, k_win, vt_win = _nsa_proj_t(x, g[0:1], w_in, *tabs, b, t)
    fs = _compress_pages(rows_t.reshape(b, 4, NSA_KV_HEADS * HEAD_DIM, t), wbd, cbias)
    n_blk = t // CMP_STRIDE - 1
    n_sel = -(-t // SEL_BLOCK)
    oc, qp = _cmp_prompt(q, fs, _overlap_t(n_blk, n_sel, n_sel, t // CMP_STRIDE), b, t)
    o_s, o_w = _flash(qp, k_slc, vt_slc, k_win, vt_win, WINDOW)
    return ([oc, o_s, o_w], gates, w_o, True), rows_t, win_t


def _to_rows(o, nb, dec):
    o = o.reshape(nb, NSA_KV_HEADS, dec, NSA_GROUP, HEAD_DIM).transpose(2, 0, 1, 3, 4)
    return o.reshape(dec * nb, NSA_HEADS * HEAD_DIM).astype(BF16)


def _nsa_layer_sample(x, g, w_in, wbd, cbias, w_o, tabs, cache, page_table, win_state, nb, dec):
    n_phys, page = cache.shape[0], cache.shape[1]
    n_pages = page_table.shape[1]
    past_len = n_pages * page
    q, rows, win, gates = _nsa_proj(x, g[0:1], w_in, *tabs)
    pages_t = cache.transpose(0, 2, 3, 4, 1).reshape(n_phys, 4, NSA_KV_HEADS * HEAD_DIM, page)
    wb = win_state.shape[1]
    win_t = win_state.transpose(0, 2, 3, 4, 1).reshape(nb, 2, NSA_KV_HEADS * HEAD_DIM, wb)
    fs_phys = _compress_pages(pages_t, wbd, cbias)
    q5 = q.reshape(dec, nb, NSA_KV_HEADS, NSA_GROUP, HEAD_DIM).transpose(1, 2, 0, 3, 4)
    eye = jnp.eye(NSA_KV_HEADS, dtype=BF16)
    qt = q5.reshape(nb, NSA_KV_HEADS, dec * NSA_GROUP, 1, HEAD_DIM) * eye[None, :, None, :, None]
    qt = qt.reshape(nb, NSA_KV_HEADS * dec * NSA_GROUP, NSA_KV_HEADS * HEAD_DIM)
    qt = jnp.pad(qt, ((0, 0), (0, LANES - qt.shape[1]), (0, 0)))
    n_blk = (past_len + dec) // CMP_STRIDE - 1
    n_sel = -(-(past_len + dec) // SEL_BLOCK)
    nj = -(-n_sel // SUBLANES) * SUBLANES
    lane = jnp.arange(LANES)
    msum = (lane[:, None] // NSA_GROUP == lane[None, :] // NSA_GROUP).astype(BF16)
    oc, a_t = _cmp_sample(page_table, fs_phys, qt, _overlap_t(n_blk, n_sel, nj, past_len // CMP_STRIDE), msum, past_len, dec)
    t_q = ((lane // NSA_GROUP) % dec)[None, :]
    j8 = jnp.arange(SUBLANES)[:, None]
    new_mask = jnp.where((j8 < dec) & (j8 <= t_q), 0.0, NEG).astype(F32)
    win_mask = jnp.where(jnp.arange(wb)[:, None] >= t_q + wb - WINDOW, 0.0, NEG).astype(F32)
    qbd = qt.transpose(0, 2, 1)

    def new_rows(a):
        a = a.reshape(dec, nb, 512).transpose(1, 0, 2)
        return jnp.pad(a, ((0, 0), (0, SUBLANES - dec), (0, 0)))

    def from_t(o):
        o = o[:, :, :NSA_KV_HEADS * dec * NSA_GROUP].reshape(nb, NSA_KV_HEADS, HEAD_DIM, NSA_KV_HEADS, dec, NSA_GROUP)
        o = jnp.diagonal(o, axis1=1, axis2=3)
        return o.transpose(2, 0, 4, 3, 1).reshape(dec * nb, NSA_HEADS * HEAD_DIM).astype(BF16)

    (o_s,) = _attn_sample(pages_t, 1, n_pages, new_rows(rows[:, 512:1024]), qbd, new_mask, SLC_SEQS, dec,
                          page_table=page_table, blk_mask=a_t)
    o_w, win_next = _attn_sample(win_t, 0, 1, new_rows(win), qbd, new_mask, WIN_SEQS, dec,
                                 past_mask=win_mask, emit_state=True)
    mix = ([_to_rows(oc, nb, dec), from_t(o_s), from_t(o_w)], gates, w_o, False)
    win_next = win_next.reshape(nb, 2, NSA_KV_HEADS, HEAD_DIM, wb).transpose(0, 4, 1, 2, 3)
    return mix, rows, win_next


def _gla_layer(x, g, w_in, wg, bg, gn, w_o, nb, seq, s0=None):
    q, k, v, sr, la = _gla_proj(x, g[0:1], w_in, wg, bg)
    if s0 is None:
        o, s = _gla_rec(q, k, la, v, sr, gn, nb, GLA_CHUNK)
    else:
        def seqs(a):
            a = a.reshape(seq, nb, a.shape[1]).transpose(1, 0, 2)
            return jnp.pad(a, ((0, 0), (0, SUBLANES - seq), (0, 0))).reshape(nb * SUBLANES, a.shape[2])
        o, s = _gla_rec(seqs(q), seqs(k), seqs(la), seqs(v), seqs(sr), gn, nb, SUBLANES, s0=s0)
        o = o.reshape(nb, SUBLANES, o.shape[1])[:, :seq].transpose(1, 0, 2).reshape(seq * nb, o.shape[1])
    return ([o], None, w_o, False), s


def kernel(x_prompt, x_sample, cache_nsa_kv, state_win_kv, state_gla, state_ffn_conv, page_table, norm_gain, nsa_w_in, nsa_cmp_pe, nsa_cmp_w, nsa_w_o, gla_w_in, gla_w_gate_up, gla_b_gate, gla_norm_gain, gla_w_o, ffn_w_up, ffn_conv_w, ffn_conv_b, ffn_w_down):
    b, t, d = x_prompt.shape
    nb, dec, _ = x_sample.shape
    depth = norm_gain.shape[0]
    f = ffn_w_down.shape[1]
    past_len = page_table.shape[1] * cache_nsa_kv.shape[2]

    xp = x_prompt.reshape(b * t, d)
    xs = x_sample.transpose(1, 0, 2).reshape(dec * nb, d)
    tabs_p = _rope_tables(jnp.arange(t, dtype=jnp.int32))
    tabs_s = _rope_tables(past_len + jnp.repeat(jnp.arange(dec, dtype=jnp.int32), nb))

    nsa_p, nsa_s, win_p, win_s, gla_p, gla_s, ffn_p, ffn_s = [], [], [], [], [], [], [], []
    for i in range(depth):
        g = norm_gain[i]
        a = i // 2
        if i % 2 == 0:
            w_in = _pad_cols(nsa_w_in[a], 2688).astype(BF16)
            w_o = nsa_w_o[a].astype(BF16)
            wbd, pe_t = _cmp_weights(nsa_cmp_pe[a], nsa_cmp_w[a])
            cbias = _cmp_bias(pe_t, wbd)
            mix_p, rows_t, win_t = _nsa_layer_prompt(xp, g, w_in, wbd, cbias, w_o, tabs_p, b, t)
            nsa_p.append(rows_t.reshape(b, 4, NSA_KV_HEADS, HEAD_DIM, t).transpose(0, 4, 1, 2, 3))
            nw = min(WINDOW, t)
            win_p.append(win_t.reshape(b, 2, NSA_KV_HEADS, HEAD_DIM, t)[..., t - nw:].transpose(0, 4, 1, 2, 3))
            mix_s, rows, win_next = _nsa_layer_sample(xs, g, w_in, wbd, cbias, w_o, tabs_s, cache_nsa_kv[a], page_table,
                                                      state_win_kv[a], nb, dec)
            nsa_s.append(rows.reshape(dec, nb, 4, NSA_KV_HEADS, HEAD_DIM).transpose(1, 0, 2, 3, 4))
            win_s.append(win_next)
        else:
            w_in = _pad_cols(gla_w_in[a], 3200).astype(BF16)
            wg = jnp.pad(gla_w_gate_up[a], ((0, LANES - GLA_GATE_RANK), (0, 0))).astype(BF16)
            bg = gla_b_gate[a][None, :]
            gn = gla_norm_gain[a][None, :]
            w_o = gla_w_o[a].astype(BF16)
            mix_p, s = _gla_layer(xp, g, w_in, wg, bg, gn, w_o, b, t)
            gla_p.append(s)
            mix_s, s = _gla_layer(xs, g, w_in, wg, bg, gn, w_o, nb, dec, s0=state_gla[a])
            gla_s.append(s)
        wu = ffn_w_up[i].astype(BF16)
        wd = ffn_w_down[i].astype(BF16)
        cw = jnp.pad(ffn_conv_w[i], ((0, SUBLANES - ffn_conv_w.shape[1]), (0, 0)))
        cb = ffn_conv_b[i][None, :]
        xp, tail = _ffn(xp, mix_p, g, wu, cw, cb, wd, seq_len=t)
        tail = tail.reshape(b, t // FFN_TM, SUBLANES, f)[:, -1, SUBLANES - 2:]
        ffn_p.append(tail)
        buf = state_ffn_conv[i].transpose(1, 0, 2).reshape(2 * nb, f)
        xs, tail = _ffn(xs, mix_s, g, wu, cw, cb, wd, buf=buf)
        ffn_s.append(tail.reshape(2, nb, f).transpose(1, 0, 2))

    y_prompt = xp.reshape(b, t, d)
    y_sample = xs.reshape(dec, nb, d).transpose(1, 0, 2)
    return (y_prompt, y_sample, jnp.stack(nsa_p), jnp.stack(nsa_s), jnp.stack(win_p), jnp.stack(win_s),
            jnp.stack(gla_p), jnp.stack(gla_s), jnp.stack(ffn_p), jnp.stack(ffn_s))
```

```python
import functools
import math

import jax
import jax.numpy as jnp
from jax import lax
from jax.experimental import pallas as pl
from jax.experimental.pallas import tpu as pltpu

F32, BF16 = jnp.float32, jnp.bfloat16

HEAD_DIM = 64
NSA_HEADS = 16
NSA_KV_HEADS = 4
NSA_GROUP = 4
CMP_STRIDE = 16
SEL_BLOCK = 64
TOP_N = 16
WINDOW = 512
ROPE_THETA = 10000.0
GLA_HEADS = 4
GLA_DK = 128
GLA_DV = 256
GLA_GATE_RANK = 16
GLA_GATE_TEMP = 16.0
NORM_EPS = 1e-6
FORCE = 1e6
NEG = -1e30

LANES = 128
SUBLANES = 8
VMEM_LIMIT = 48 * 1024 * 1024

TM = 512
TQ = 256
VT_ROWS = 80
CMP_ROWS = 256
GLA_CHUNK = 128
SLC_SEQS = 2
WIN_SEQS = 4
CMP_SEQS = 4
GLA_SEQS = 4
FFN_FCHUNK = 2816
FFN_TM = 256


def _cparams(sem):
    return pltpu.CompilerParams(dimension_semantics=sem, vmem_limit_bytes=VMEM_LIMIT)


def _resident(shape):
    nd = len(shape)
    return pl.BlockSpec(shape, lambda *_: (0,) * nd, pipeline_mode=pl.Buffered(1))


def _rms(x, g):
    return x * lax.rsqrt(jnp.mean(x * x, axis=-1, keepdims=True) + NORM_EPS) * g


def _nt(a, b):
    return lax.dot_general(a, b, (((1,), (1,)), ((), ())), preferred_element_type=F32)


def _tn(a, b):
    return lax.dot_general(a, b, (((0,), (0,)), ((), ())), preferred_element_type=F32)


def _dot(a, b):
    return jnp.dot(a, b, preferred_element_type=F32)


def _split_dot(w, x):
    hi = x.astype(BF16)
    lo = (x - hi.astype(F32)).astype(BF16)
    return _dot(w, hi) + _dot(w, lo)


def _softmax_masked(s, mask, axis):
    s = jnp.where(mask, s, NEG)
    m = jnp.max(s, axis=axis, keepdims=True)
    e = jnp.where(mask, jnp.exp2(s - m), 0.0)
    return e / jnp.maximum(jnp.sum(e, axis=axis, keepdims=True), 1e-30)


Q_SCALE = HEAD_DIM ** -0.5 * math.log2(math.e)


def _nsa_proj_parts(x_ref, g_ref, w_ref, cos_ref, sa_ref, sb_ref, q_ref, gate_ref):
    h = _rms(x_ref[...], g_ref[...]).astype(BF16)
    cos, sa, sb = cos_ref[...], sa_ref[...], sb_ref[...]

    def rope(z):
        return z * cos + pltpu.roll(z, 96, axis=1) * sa + pltpu.roll(z, 32, axis=1) * sb

    def proj(c0):
        return _dot(h, w_ref[:, c0:c0 + 256])

    def rope256(z):
        return jnp.concatenate([rope(z[:, :LANES]), rope(z[:, LANES:])], axis=1)

    for c in range(4):
        q_ref[:, c * 256:(c + 1) * 256] = (rope256(proj(c * 256)) * Q_SCALE).astype(BF16)
    gate_ref[...] = jax.nn.sigmoid(_dot(h, w_ref[:, 2560:2688]))
    return (rope256(proj(1024)), proj(1280), rope256(proj(1536)), proj(1792), rope256(proj(2048)), proj(2304))


def _nsa_proj_body(x_ref, g_ref, w_ref, cos_ref, sa_ref, sb_ref, q_ref, rows_ref, win_ref, gate_ref):
    parts = _nsa_proj_parts(x_ref, g_ref, w_ref, cos_ref, sa_ref, sb_ref, q_ref, gate_ref)
    for c in range(4):
        rows_ref[:, c * 256:(c + 1) * 256] = parts[c]
    win_ref[:, 0:256] = parts[4]
    win_ref[:, 256:512] = parts[5]


def _nsa_proj_t_body(x_ref, g_ref, w_ref, cos_ref, sa_ref, sb_ref, q_ref, gate_ref,
                     rows_t_ref, win_t_ref, ks_ref, vs_ref, kw_ref, vw_ref, *, tiles_per_seq, n_sel):
    kc, vc, ks, vs, kw, vw = _nsa_proj_parts(x_ref, g_ref, w_ref, cos_ref, sa_ref, sb_ref, q_ref, gate_ref)
    tm = kc.shape[0]
    for c, part in enumerate((kc, vc, ks, vs)):
        rows_t_ref[0, c * 256:(c + 1) * 256, :] = part.T
    win_t_ref[0, 0:256, :] = kw.T
    win_t_ref[0, 256:512, :] = vw.T
    ones = jnp.ones((NSA_KV_HEADS, VT_ROWS - HEAD_DIM, tm), BF16)
    vs_ref[0] = jnp.concatenate([vs.T.reshape(NSA_KV_HEADS, HEAD_DIM, tm).astype(BF16), ones], axis=1)
    vw_ref[0] = jnp.concatenate([vw.T.reshape(NSA_KV_HEADS, HEAD_DIM, tm).astype(BF16), ones], axis=1)
    t = (pl.program_id(0) % tiles_per_seq) * tm + lax.broadcasted_iota(jnp.int32, (tm, HEAD_DIM), 0)
    lane = lax.broadcasted_iota(jnp.int32, (tm, HEAD_DIM), 1)
    onehot = jnp.where(jnp.where(lane < n_sel, t // SEL_BLOCK, -1) == lane, 1.0, 0.0).astype(BF16)
    zeros = jnp.zeros((tm, HEAD_DIM), BF16)
    for g in range(NSA_KV_HEADS):
        sl = slice(g * HEAD_DIM, (g + 1) * HEAD_DIM)
        ks_ref[0, g] = jnp.concatenate([ks[:, sl].astype(BF16), onehot], axis=1)
        kw_ref[0, g] = jnp.concatenate([kw[:, sl].astype(BF16), zeros], axis=1)


def _nsa_proj_t(x, g, w, cos, sa, sb, b, t):
    m, d = x.shape
    nt = t // TM
    n_sel = -(-t // SEL_BLOCK)
    row = lambda i: (i, 0)
    tab = lambda i: (i % nt, 0)
    fm = lambda i: (i // nt, 0, i % nt)
    kmap = lambda i: (i // nt, 0, i % nt, 0)
    vmap = lambda i: (i // nt, 0, 0, i % nt)
    kv = NSA_KV_HEADS
    return pl.pallas_call(
        functools.partial(_nsa_proj_t_body, tiles_per_seq=nt, n_sel=n_sel),
        grid=(m // TM,),
        in_specs=[pl.BlockSpec((TM, d), row), _resident(g.shape), _resident(w.shape),
                  pl.BlockSpec((TM, LANES), tab), pl.BlockSpec((TM, LANES), tab), pl.BlockSpec((TM, LANES), tab)],
        out_specs=[pl.BlockSpec((TM, 1024), row), pl.BlockSpec((TM, LANES), row),
                   pl.BlockSpec((1, 1024, TM), fm), pl.BlockSpec((1, 512, TM), fm),
                   pl.BlockSpec((1, kv, TM, LANES), kmap), pl.BlockSpec((1, kv, VT_ROWS, TM), vmap),
                   pl.BlockSpec((1, kv, TM, LANES), kmap), pl.BlockSpec((1, kv, VT_ROWS, TM), vmap)],
        out_shape=[jax.ShapeDtypeStruct((m, 1024), BF16), jax.ShapeDtypeStruct((m, LANES), F32),
                   jax.ShapeDtypeStruct((b, 1024, t), F32), jax.ShapeDtypeStruct((b, 512, t), F32),
                   jax.ShapeDtypeStruct((b, kv, t, LANES), BF16), jax.ShapeDtypeStruct((b, kv, VT_ROWS, t), BF16),
                   jax.ShapeDtypeStruct((b, kv, t, LANES), BF16), jax.ShapeDtypeStruct((b, kv, VT_ROWS, t), BF16)],
        compiler_params=_cparams(("arbitrary",)),
        name="nsa_proj_t",
    )(x, g, w, cos, sa, sb)


def _nsa_proj(x, g, w, cos, sa, sb):
    m, d = x.shape
    ntab = cos.shape[0] // TM
    row = lambda i: (i, 0)
    tab = lambda i: (i % ntab, 0)
    return pl.pallas_call(
        _nsa_proj_body,
        grid=(m // TM,),
        in_specs=[pl.BlockSpec((TM, d), row), _resident(g.shape), _resident(w.shape),
                  pl.BlockSpec((TM, LANES), tab), pl.BlockSpec((TM, LANES), tab), pl.BlockSpec((TM, LANES), tab)],
        out_specs=[pl.BlockSpec((TM, 1024), row), pl.BlockSpec((TM, 1024), row),
                   pl.BlockSpec((TM, 512), row), pl.BlockSpec((TM, LANES), row)],
        out_shape=[jax.ShapeDtypeStruct((m, 1024), BF16), jax.ShapeDtypeStruct((m, 1024), F32),
                   jax.ShapeDtypeStruct((m, 512), F32), jax.ShapeDtypeStruct((m, LANES), F32)],
        compiler_params=_cparams(("arbitrary",)),
        name="nsa_proj",
    )(x, g, w, cos, sa, sb)


def _cmp_bias_body(pe_ref, w_ref, o_ref):
    for s in range(2):
        for half in range(2):
            acc = jnp.zeros((SUBLANES, 256), F32)
            for j in range(CMP_STRIDE):
                acc = acc + _dot(pe_ref[s, half, j].astype(BF16), w_ref[s, j, :, half * 256:(half + 1) * 256])
            o_ref[:, s * 512 + half * 256:s * 512 + (half + 1) * 256] = acc


def _cmp_bias(pe_t, wbd):
    return pl.pallas_call(_cmp_bias_body, out_shape=jax.ShapeDtypeStruct((SUBLANES, 1024), F32),
                          compiler_params=_cparams(None), name="cmp_bias")(pe_t, wbd)


CMP_HALF = CMP_ROWS // 2
CMP_PITCH = CMP_HALF + SUBLANES


def _compress_pages_body(x_ref, w_ref, b_ref, o_ref, xa_ref, xb_ref):
    per_page = LANES // CMP_STRIDE
    pages_per_block = x_ref.shape[3] // LANES
    pages_half = CMP_HALF // per_page

    def move(xs_ref, p0):
        for q in range(pages_half):
            blk, off = divmod(p0 + q, pages_per_block)
            for s in range(2):
                xt = x_ref[blk, s, :, off * LANES:(off + 1) * LANES].T
                for c in range(per_page):
                    for half in range(2):
                        xs_ref[2 * s + half, pl.ds(q * per_page + c, CMP_STRIDE, stride=CMP_PITCH), :] = (
                            xt[c * CMP_STRIDE:(c + 1) * CMP_STRIDE, half * LANES:(half + 1) * LANES])

    def project(xs_ref, r0):
        for s in range(2):
            acc = jnp.zeros((CMP_HALF, 512), F32)
            for j in range(CMP_STRIDE):
                xj = jnp.concatenate([xs_ref[2 * s, j * CMP_PITCH:j * CMP_PITCH + CMP_HALF, :],
                                      xs_ref[2 * s + 1, j * CMP_PITCH:j * CMP_PITCH + CMP_HALF, :]], axis=1)
                acc = acc + _dot(xj.astype(BF16), w_ref[s, j])
            o_ref[r0:r0 + CMP_HALF, s * 512:(s + 1) * 512] = acc + b_ref[0:1, s * 512:(s + 1) * 512]

    move(xa_ref, 0)
    move(xb_ref, pages_half)
    project(xa_ref, 0)
    project(xb_ref, CMP_HALF)


def _compress_pages(pages_t, wbd, bias):
    n, _, _, npos = pages_t.shape
    chunks = npos // CMP_STRIDE
    pp = CMP_ROWS // chunks
    return pl.pallas_call(
        _compress_pages_body,
        grid=(n // pp,),
        in_specs=[pl.BlockSpec((pp, 2, 256, npos), lambda i: (i, 0, 0, 0)),
                  _resident(wbd.shape), _resident(bias.shape)],
        out_specs=pl.BlockSpec((CMP_ROWS, 1024), lambda i: (i, 0)),
        out_shape=jax.ShapeDtypeStruct((n * chunks, 1024), F32),
        scratch_shapes=[pltpu.VMEM((4, CMP_STRIDE * CMP_PITCH, LANES), F32),
                        pltpu.VMEM((4, CMP_STRIDE * CMP_PITCH, LANES), F32)],
        compiler_params=_cparams(("arbitrary",)),
        name="compress_pages",
    )(pages_t, wbd, bias)


def _finish_compress(fs):
    n = fs.shape[0]
    kc = fs[:, 0:256] + pltpu.roll(fs[:, 256:512], n - 1, axis=0)
    vc = fs[:, 512:768] + pltpu.roll(fs[:, 768:1024], n - 1, axis=0)
    return kc.astype(BF16), vc.astype(BF16)


def _select_mask(sc_raw, jr, pos, n_real):
    sc = jnp.where(jr * SEL_BLOCK > pos, -FORCE, sc_raw)
    sc = jnp.where(jr == pos // SEL_BLOCK, FORCE, jnp.where(jr == 0, FORCE, sc))
    sc = jnp.where(jr >= n_real, -2.0 * FORCE, sc)
    cnt = jnp.zeros(sc.shape, F32)
    for i in range(n_real):
        ri = sc[i:i + 1, :]
        ge = jnp.where(ri >= sc, 1.0, 0.0)
        gt = jnp.where(ri > sc, 1.0, 0.0)
        cnt = cnt + jnp.where(jr > i, ge, gt)
    keep = jnp.where(sc > -0.5 * FORCE, 0.0, NEG)
    return jnp.where(cnt < float(min(TOP_N, n_real)), keep, NEG)


def _cmp_prompt_body(q_ref, fs_ref, ov_ref, oc_ref, qp_ref):
    i = pl.program_id(1)
    tq = q_ref.shape[0]
    nblk = fs_ref.shape[0]
    fs = fs_ref[...]
    kc = (fs[:, 0:256] + pltpu.roll(fs[:, 256:512], nblk - 1, axis=0)).astype(BF16)
    vc_t = (fs[:, 512:768] + pltpu.roll(fs[:, 768:1024], nblk - 1, axis=0)).T.astype(BF16)
    t_row = i * tq + lax.broadcasted_iota(jnp.int32, (nblk, tq), 1)
    n_col = lax.broadcasted_iota(jnp.int32, (nblk, tq), 0)
    vis_t = n_col * CMP_STRIDE + 2 * CMP_STRIDE - 1 <= t_row
    nsel = ov_ref.shape[0]
    jr = lax.broadcasted_iota(jnp.int32, (nsel, tq), 0)
    pos = i * tq + lax.broadcasted_iota(jnp.int32, (nsel, tq), 1)
    eye = jnp.where(lax.broadcasted_iota(jnp.int32, (tq, tq), 0) == lax.broadcasted_iota(jnp.int32, (tq, tq), 1),
                    1.0, 0.0).astype(BF16)
    for g in range(NSA_KV_HEADS):
        kg = kc[:, g * HEAD_DIM:(g + 1) * HEAD_DIM]
        vg_t = vc_t[g * HEAD_DIM:(g + 1) * HEAD_DIM, :]
        ps_t = jnp.zeros((nblk, tq), F32)
        qs = []
        for r in range(NSA_GROUP):
            hd = g * NSA_GROUP + r
            qh = q_ref[:, hd * HEAD_DIM:(hd + 1) * HEAD_DIM]
            qs.append(qh)
            p_t = _softmax_masked(_nt(kg, qh), vis_t, 0)
            oc_ref[0, hd * HEAD_DIM:(hd + 1) * HEAD_DIM, :] = _dot(vg_t, p_t.astype(BF16)).astype(oc_ref.dtype)
            ps_t = ps_t + p_t
        a_t = _select_mask(_split_dot(ov_ref[...], ps_t), jr, pos, nsel)
        a_t = jnp.concatenate([a_t, jnp.zeros((HEAD_DIM - nsel, tq), F32)], axis=0).astype(BF16)
        a = _nt(eye, a_t).astype(BF16)
        for r in range(NSA_GROUP):
            qp_ref[0, g * NSA_GROUP + r] = jnp.concatenate([qs[r], a], axis=1)


def _cmp_prompt(q, fs, ov_t, b, t):
    nq = t // TQ
    nblk = t // CMP_STRIDE
    return pl.pallas_call(
        _cmp_prompt_body,
        grid=(b, nq),
        in_specs=[pl.BlockSpec((TQ, 1024), lambda bi, i: (bi * nq + i, 0)),
                  pl.BlockSpec((nblk, 1024), lambda bi, i: (bi, 0)),
                  _resident(ov_t.shape)],
        out_specs=[pl.BlockSpec((1, 1024, TQ), lambda bi, i: (bi, 0, i)),
                   pl.BlockSpec((1, NSA_HEADS, TQ, LANES), lambda bi, i: (bi, 0, i, 0))],
        out_shape=[jax.ShapeDtypeStruct((b, 1024, t), BF16),
                   jax.ShapeDtypeStruct((b, NSA_HEADS, t, LANES), BF16)],
        compiler_params=_cparams(("arbitrary", "arbitrary")),
        name="cmp_prompt",
    )(q, fs, ov_t)


FLASH_RUN = 4


def _flash_body(qp_ref, ks_ref, vs_ref, kw_ref, vw_ref, os_ref, ow_ref, st_ref, acc_ref, *, window):
    _flash_branch(qp_ref, ks_ref, vs_ref, os_ref, st_ref, acc_ref, window=None)
    _flash_branch(qp_ref, kw_ref, vw_ref, ow_ref, st_ref, acc_ref, window=window)


def _flash_branch(qp_ref, kp_ref, vt_ref, o_ref, st_ref, acc_ref, *, window):
    i = pl.program_id(2)
    tq = qp_ref.shape[2]
    dq = (i * tq + lax.broadcasted_iota(jnp.int32, (tq, tq), 1)) - lax.broadcasted_iota(jnp.int32, (tq, tq), 0)

    def run(c, masks, m):
        n = len(masks)
        off = pl.multiple_of(c * tq, tq)
        vt = vt_ref[0, 0, :, pl.ds(off, n * tq)]
        out = list(m)
        for u, (causal, far) in enumerate(masks):
            k = kp_ref[0, 0, pl.ds(off + u * tq, tq), :]
            d = dq - (c + u) * tq
            for r in range(NSA_GROUP):
                s = _nt(k, qp_ref[0, r])
                if causal:
                    s = jnp.where(d >= 0, s, NEG)
                if far:
                    s = jnp.where(d <= window, s, NEG)
                st_ref[r, u * tq:(u + 1) * tq, :] = s
                out[r] = jnp.maximum(out[r], jnp.max(s, axis=0, keepdims=True))
        for r in range(NSA_GROUP):
            p = jnp.exp2(st_ref[r, 0:n * tq, :] - out[r]).astype(BF16)
            acc_ref[r] = jnp.exp2(m[r] - out[r]) * acc_ref[r] + _dot(vt, p)
        return tuple(out)

    def by_runs(lo, hi, mask, init):
        n = hi - lo
        carry = lax.fori_loop(0, n // FLASH_RUN, lambda j, m: run(lo + FLASH_RUN * j, (mask,) * FLASH_RUN, m), init)
        c2 = lo + (n // FLASH_RUN) * FLASH_RUN
        carry = lax.cond(n % FLASH_RUN >= 2, lambda m: run(c2, (mask,) * 2, m), lambda m: m, carry)
        return lax.cond(n % 2 == 1, lambda m: run(hi - 1, (mask,), m), lambda m: m, carry)

    acc_ref[...] = jnp.zeros(acc_ref.shape, F32)
    mx = tuple(jnp.full((1, tq), NEG, F32) for _ in range(NSA_GROUP))
    if window is None:
        by_runs(0, i + 1, (True, False), mx)
    else:
        back = window // tq
        steady = ((False, True),) + ((False, False),) * (back - 1) + ((True, False),)
        lax.cond(i >= back, lambda m: run(i - back, steady, m),
                 lambda m: by_runs(0, i + 1, (True, True), m), mx)
    for r in range(NSA_GROUP):
        acc = acc_ref[r]
        o_ref[0, r * HEAD_DIM:(r + 1) * HEAD_DIM, :] = (acc[0:HEAD_DIM] / acc[HEAD_DIM:HEAD_DIM + 1]).astype(o_ref.dtype)


def _flash(qp, k_slc, vt_slc, k_win, vt_win, window):
    b, _, t, _ = qp.shape
    nq = t // TQ
    assert window % TQ == 0 and window // TQ + 1 <= FLASH_RUN
    kspec = pl.BlockSpec((1, 1, t, LANES), lambda bi, g, i: (bi, g, 0, 0))
    vspec = pl.BlockSpec((1, 1, VT_ROWS, t), lambda bi, g, i: (bi, g, 0, 0))
    ospec = pl.BlockSpec((1, NSA_GROUP * HEAD_DIM, TQ), lambda bi, g, i: (bi, g, i))
    oshape = jax.ShapeDtypeStruct((b, NSA_HEADS * HEAD_DIM, t), BF16)
    return pl.pallas_call(
        functools.partial(_flash_body, window=window),
        grid=(b, NSA_KV_HEADS, nq),
        in_specs=[pl.BlockSpec((1, NSA_GROUP, TQ, LANES), lambda bi, g, i: (bi, g, i, 0)), kspec, vspec, kspec, vspec],
        out_specs=[ospec, ospec], out_shape=[oshape, oshape],
        scratch_shapes=[pltpu.VMEM((NSA_GROUP, FLASH_RUN * TQ, TQ), F32), pltpu.VMEM((NSA_GROUP, VT_ROWS, TQ), F32)],
        compiler_params=_cparams(("arbitrary", "arbitrary", "arbitrary")),
        name="flash",
    )(qp, k_slc, vt_slc, k_win, vt_win)


def _diag_blocks(o_full, o_ref, s=0):
    rows = o_ref.shape[1] // NSA_KV_HEADS
    for g in range(NSA_KV_HEADS):
        o_ref[s, g * rows:(g + 1) * rows, :] = o_full[g * rows:(g + 1) * rows, g * HEAD_DIM:(g + 1) * HEAD_DIM]


def _cmp_sample_body(pt_ref, *refs, n_pages, past_len, dec, sb):
    fs_refs = refs[:sb * n_pages]
    qt_ref, ov_ref, msum_ref, oc_ref, at_ref = refs[sb * n_pages:]
    nq = qt_ref.shape[1]
    nblk = n_pages * fs_refs[0].shape[0]
    nj = ov_ref.shape[0]
    pos_l = past_len + (lax.broadcasted_iota(jnp.int32, (nblk, nq), 1) // NSA_GROUP) % dec
    n_c = lax.broadcasted_iota(jnp.int32, (nblk, nq), 0)
    vis_t = jnp.logical_and(n_c * CMP_STRIDE + 2 * CMP_STRIDE - 1 <= pos_l, n_c < nblk - 1)
    jr = lax.broadcasted_iota(jnp.int32, (nj, nq), 0)
    pos = past_len + (lax.broadcasted_iota(jnp.int32, (nj, nq), 1) // NSA_GROUP) % dec
    n_sel = -(-(past_len + dec) // SEL_BLOCK)
    for s in range(sb):
        fs = jnp.concatenate([r[...] for r in fs_refs[s * n_pages:(s + 1) * n_pages]], axis=0)
        kc, vc = _finish_compress(fs)
        qt = qt_ref[s]
        p_t = _softmax_masked(_nt(kc, qt), vis_t, 0)
        _diag_blocks(_tn(p_t.astype(BF16), vc), oc_ref, s)
        a1 = _split_dot(ov_ref[...], p_t)
        hi = a1.astype(BF16)
        lo = (a1 - hi.astype(F32)).astype(BF16)
        sc = _dot(hi, msum_ref[...]) + _dot(lo, msum_ref[...])
        at_ref[s] = _select_mask(sc, jr, pos, n_sel)


def _cmp_sample(page_table, fs_phys, qt, ov_t, msum, past_len, dec):
    nb, n_pages = page_table.shape
    sb = CMP_SEQS
    nj = ov_t.shape[0]
    fs_specs =[pl.BlockSpec((SUBLANES, 1024), functools.partial(lambda b, pt, s, k: (pt[b * sb + s, k], 0), s=s, k=k))
                for s in range(sb) for k in range(n_pages)]
    grid_spec = pltpu.PrefetchScalarGridSpec(
        num_scalar_prefetch=1,
        grid=(nb // sb,),
        in_specs=fs_specs + [pl.BlockSpec((sb, LANES, 256), lambda b, pt: (b, 0, 0)),
                             pl.BlockSpec(ov_t.shape, lambda b, pt: (0, 0)),
                             pl.BlockSpec(msum.shape, lambda b, pt: (0, 0))],
        out_specs=[pl.BlockSpec((sb, 64, HEAD_DIM), lambda b, pt: (b, 0, 0)),
                   pl.BlockSpec((sb, nj, LANES), lambda b, pt: (b, 0, 0))],
    )
    return pl.pallas_call(
        functools.partial(_cmp_sample_body, n_pages=n_pages, past_len=past_len, dec=dec, sb=sb),
        grid_spec=grid_spec,
        out_shape=[jax.ShapeDtypeStruct((nb, 64, HEAD_DIM), F32),
                   jax.ShapeDtypeStruct((nb, nj, LANES), F32)],
        compiler_params=_cparams(("arbitrary",)),
        name="cmp_sample",
    )(page_table, *([fs_phys] * (sb * n_pages)), qt, ov_t, msum)


def _attn_sample_body(*refs, n_blocks, sb, dec, has_pt, has_bm, has_pm, emit_state):
    refs = list(refs[1:] if has_pt else refs)
    n_kv = sb * n_blocks if has_pt else 1
    kv_refs = refs[:n_kv]
    rest = refs[n_kv:]
    new_ref, qt_ref = rest[0], rest[1]
    rest = rest[2:]
    bm_ref = rest.pop(0) if has_bm else None
    pm_ref = rest.pop(0) if has_pm else None
    nm_ref, o_ref = rest[0], rest[1]
    so_ref = rest[2] if emit_state else None
    st_ref = rest[-1]
    nq = qt_ref.shape[2]
    rows = kv_refs[0].shape[3]
    per = rows // SEL_BLOCK

    def kv(s, kb, which):
        return kv_refs[s * n_blocks + kb][0, which] if has_pt else kv_refs[0][s, which]

    for s in range(sb):
        qbd = qt_ref[s]
        m = jnp.full((1, nq), NEG, F32)
        for kb in range(n_blocks):
            sc = _tn(kv(s, kb, 0).astype(BF16), qbd)
            if has_pm:
                sc = sc + pm_ref[kb * rows:(kb + 1) * rows, :]
            if has_bm:
                sc = sc + jnp.concatenate(
                    [jnp.broadcast_to(bm_ref[s, kb * per + u:kb * per + u + 1, :], (SEL_BLOCK, nq)) for u in range(per)],
                    axis=0)
            st_ref[s, kb * rows:(kb + 1) * rows, :] = sc
            m = jnp.maximum(m, jnp.max(sc, axis=0, keepdims=True))
        s_new = _dot(new_ref[s, :, 0:256].astype(BF16), qbd) + nm_ref[...]
        if has_bm:
            s_new = s_new + bm_ref[s, n_blocks * per:n_blocks * per + 1, :]
        m = jnp.maximum(m, jnp.max(s_new, axis=0, keepdims=True))
        e_new = jnp.exp2(s_new - m)
        l = jnp.sum(e_new, axis=0, keepdims=True)
        acc = _tn(new_ref[s, :, 256:512].astype(BF16), e_new.astype(BF16))
        for kb in range(n_blocks):
            e = jnp.exp2(st_ref[s, kb * rows:(kb + 1) * rows, :] - m)
            l = l + jnp.sum(e, axis=0, keepdims=True)
            acc = acc + _dot(kv(s, kb, 1).astype(BF16), e.astype(BF16))
        o_ref[s] = acc / jnp.maximum(l, 1e-30)
        if emit_state:
            lane = lax.broadcasted_iota(jnp.int32, (256, LANES), 1)
            tail = pltpu.roll(new_ref[s], SUBLANES - dec, axis=0)
            for which in range(2):
                shifted = pltpu.roll(kv(s, 0, which), rows - dec, axis=1)
                fill = jnp.concatenate([jnp.zeros((LANES - SUBLANES, 256), F32),
                                        tail[:, which * 256:(which + 1) * 256]], axis=0).T
                last = jnp.where(lane < LANES - dec, shifted[:, rows - LANES:], fill)
                so_ref[s, which, :, 0:rows - LANES] = shifted[:, 0:rows - LANES]
                so_ref[s, which, :, rows - LANES:rows] = last


def _attn_sample(kv, slot_blk, n_blocks, new, qt, new_mask, sb, dec,
                 page_table=None, blk_mask=None, past_mask=None, emit_state=False):
    nb = qt.shape[0]
    kv_rows = kv.shape[3]
    has_pt = page_table is not None
    has_bm = blk_mask is not None
    has_pm = past_mask is not None
    if has_pt:
        kv_specs = [pl.BlockSpec((1, 2, 256, kv_rows),
                                 functools.partial(lambda b, pt, s, k: (pt[b * sb + s, k], slot_blk, 0, 0), s=s, k=k))
                    for s in range(sb) for k in range(n_blocks)]
        im = lambda b, pt: (b, 0, 0)
        im4 = lambda b, pt: (b, 0, 0, 0)
        cm = lambda b, pt: (0, 0)
    else:
        kv_specs = [pl.BlockSpec((sb, 2, 256, kv_rows), lambda b: (b, slot_blk, 0, 0))]
        im = lambda b: (b, 0, 0)
        im4 = lambda b: (b, 0, 0, 0)
        cm = lambda b: (0, 0)
    in_specs = kv_specs + [pl.BlockSpec((sb, SUBLANES, 512), im), pl.BlockSpec((sb, 256, LANES), im)]
    args = [kv] * len(kv_specs) + [new, qt]
    if has_bm:
        in_specs.append(pl.BlockSpec((sb,) + blk_mask.shape[1:], im))
        args.append(blk_mask)
    if has_pm:
        in_specs.append(pl.BlockSpec(past_mask.shape, cm))
        args.append(past_mask)
    in_specs.append(pl.BlockSpec(new_mask.shape, cm))
    args.append(new_mask)
    out_specs = [pl.BlockSpec((sb, 256, LANES), im)]
    out_shape = [jax.ShapeDtypeStruct((nb, 256, LANES), F32)]
    if emit_state:
        out_specs.append(pl.BlockSpec((sb, 2, 256, kv_rows), im4))
        out_shape.append(jax.ShapeDtypeStruct((nb, 2, 256, kv_rows), F32))
    scratch = [pltpu.VMEM((sb, n_blocks * kv_rows, LANES), F32)]
    body = functools.partial(_attn_sample_body, n_blocks=n_blocks, sb=sb, dec=dec, has_pt=has_pt, has_bm=has_bm,
                             has_pm=has_pm, emit_state=emit_state)
    if has_pt:
        gs = pltpu.PrefetchScalarGridSpec(num_scalar_prefetch=1, grid=(nb // sb,), in_specs=in_specs,
                                          out_specs=out_specs, scratch_shapes=scratch)
        return pl.pallas_call(body, grid_spec=gs, out_shape=out_shape, compiler_params=_cparams(("arbitrary",)),
                              name="slc_sample")(page_table, *args)
    return pl.pallas_call(body, grid=(nb // sb,), in_specs=in_specs, out_specs=out_specs, scratch_shapes=scratch,
                          out_shape=out_shape, compiler_params=_cparams(("arbitrary",)), name="win_sample")(*args)


def _mix_project(o_refs, gate_ref, w_ref, tm, feature_major):
    n_in = len(o_refs)
    gated = gate_ref is not None
    if feature_major:
        gates_t = gate_ref[...].T
        heads = []
        for hd in range(NSA_HEADS):
            acc = jnp.zeros((HEAD_DIM, tm), F32)
            for kbr in range(n_in):
                row = gates_t[hd * 3 + kbr:hd * 3 + kbr + 1, :]
                acc = acc + row * o_refs[kbr][0, hd * HEAD_DIM:(hd + 1) * HEAD_DIM, :].astype(F32)
            heads.append(acc.astype(BF16))
        return _tn(jnp.concatenate(heads, axis=0), w_ref[...])
    if gated:
        lane = lax.broadcasted_iota(jnp.int32, (tm, LANES), 1)
        gates = gate_ref[...]
        chunks = []
        for c in range(1024 // LANES):
            acc = jnp.zeros((tm, LANES), F32)
            for kbr in range(n_in):
                c0 = (2 * c) * 3 + kbr
                c1 = (2 * c + 1) * 3 + kbr
                gexp = jnp.where(lane < HEAD_DIM, gates[:, c0:c0 + 1], gates[:, c1:c1 + 1])
                acc = acc + gexp * o_refs[kbr][:, c * LANES:(c + 1) * LANES].astype(F32)
            chunks.append(acc.astype(BF16))
        o = jnp.concatenate(chunks, axis=1)
    else:
        o = o_refs[0][...].astype(BF16)
    return _dot(o, w_ref[...])


def _ffn_body(*refs, sample, tiles_per_seq, nb, n_in, gated, feature_major):
    o_refs = refs[:n_in]
    refs = refs[n_in:]
    gate_ref = None
    if gated:
        gate_ref, refs = refs[0], refs[1:]
    wo_ref, g1_ref, refs = refs[0], refs[1], refs[2:]
    if sample:
        x_ref, g2_ref, g3_ref, wu_ref, cw_ref, cb_ref, wd_ref, buf_ref, y_ref, tail_ref = refs
    else:
        x_ref, g2_ref, g3_ref, wu_ref, cw_ref, cb_ref, wd_ref, y_ref, tail_ref, carry_ref = refs
    tm = x_ref.shape[0]
    x = x_ref[...] + _rms(_mix_project(o_refs, gate_ref, wo_ref, tm, feature_major), g1_ref[...])
    f = wd_ref.shape[0]
    h = _rms(x, g2_ref[...]).astype(BF16)
    acc = jnp.zeros(x.shape, F32)
    if not sample:
        first = pl.program_id(0) % tiles_per_seq == 0
        row = lax.broadcasted_iota(jnp.int32, (tm, FFN_FCHUNK), 0)
    for fc in range(f // FFN_FCHUNK):
        sl = slice(fc * FFN_FCHUNK, (fc + 1) * FFN_FCHUNK)
        gate = _dot(h, wu_ref[:, sl])
        val = _dot(h, wu_ref[:, f + fc * FFN_FCHUNK:f + (fc + 1) * FFN_FCHUNK])
        if sample:
            prev1 = jnp.concatenate([buf_ref[nb:2 * nb, sl], gate[0:tm - nb]], axis=0)
            prev2 = jnp.concatenate([buf_ref[0:2 * nb, sl], gate[0:tm - 2 * nb]], axis=0)
            tail_ref[:, sl] = gate[tm - 2 * nb:tm]
        else:
            carry = carry_ref[:, sl]
            c6 = jnp.where(first, 0.0, carry[6:7, :])
            c7 = jnp.where(first, 0.0, carry[7:8, :])
            prev1 = jnp.where(row == 0, c7, pltpu.roll(gate, 1, axis=0))
            prev2 = jnp.where(row == 0, c6, jnp.where(row == 1, c7, pltpu.roll(gate, 2, axis=0)))
            carry_ref[:, sl] = gate[tm - SUBLANES:tm]
            tail_ref[:, sl] = gate[tm - SUBLANES:tm]
        conv = cb_ref[0:1, sl] + cw_ref[0:1, sl] * prev2 + cw_ref[1:2, sl] * prev1 + cw_ref[2:3, sl] * gate
        act = (jax.nn.gelu(conv) * val).astype(BF16)
        acc = acc + _dot(act, wd_ref[sl, :])
    y_ref[...] = x + _rms(acc, g3_ref[...])


def _ffn(x, mix, g, wu, cw, cb, wd, seq_len=None, buf=None):
    m, d = x.shape
    f = wd.shape[0]
    os_, gates, w_o, feature_major = mix
    sample = buf is not None
    row = lambda i: (i, 0)
    if sample:
        tm = m
        nb = buf.shape[0] // 2
        tail_shape, tail_block = (2 * nb, f), (2 * nb, f)
        scratch = []
    else:
        tm = FFN_TM
        nb = 0
        tail_shape, tail_block = (m // tm * SUBLANES, f), (SUBLANES, f)
        scratch = [pltpu.VMEM((SUBLANES, f), F32)]
    if feature_major:
        nt = seq_len // tm
        in_specs = [pl.BlockSpec((1, 1024, tm), lambda i: (i // nt, 0, i % nt)) for _ in os_]
    else:
        in_specs = [pl.BlockSpec((tm, 1024), row) for _ in os_]
    args = list(os_)
    if gates is not None:
        in_specs.append(pl.BlockSpec((tm, LANES), row))
        args.append(gates)
    g1, g2, g3 = g[1:2], g[2:3], g[3:4]
    in_specs += [_resident(w_o.shape), _resident(g1.shape),
                 pl.BlockSpec((tm, d), row), _resident(g2.shape), _resident(g3.shape), _resident(wu.shape),
                 _resident(cw.shape), _resident(cb.shape), _resident(wd.shape)]
    args += [w_o, g1, x, g2, g3, wu, cw, cb, wd]
    if sample:
        in_specs.append(_resident(buf.shape))
        args.append(buf)
    return pl.pallas_call(
        functools.partial(_ffn_body, sample=sample, tiles_per_seq=(seq_len // tm if not sample else 1), nb=nb,
                          n_in=len(os_), gated=gates is not None, feature_major=feature_major),
        grid=(m // tm,), in_specs=in_specs,
        out_specs=[pl.BlockSpec((tm, d), row), pl.BlockSpec(tail_block, row)],
        out_shape=[jax.ShapeDtypeStruct((m, d), F32), jax.ShapeDtypeStruct(tail_shape, F32)],
        scratch_shapes=scratch,
        compiler_params=_cparams(("arbitrary",)), name="ffn_sample" if sample else "ffn_prompt",
    )(*args)


def _gla_proj_body(x_ref, g_ref, w_ref, wg_ref, bg_ref, q_ref, k_ref, v_ref, sr_ref, la_ref):
    h = _rms(x_ref[...], g_ref[...]).astype(BF16)
    nk = GLA_HEADS * GLA_DK
    nv = GLA_HEADS * GLA_DV
    for c in range(nk // 256):
        q_ref[:, c * 256:(c + 1) * 256] = _dot(h, w_ref[:, c * 256:(c + 1) * 256]) * (GLA_DK ** -0.5)
        k_ref[:, c * 256:(c + 1) * 256] = _dot(h, w_ref[:, nk + c * 256:nk + (c + 1) * 256])
    for c in range(nv // 256):
        v_ref[:, c * 256:(c + 1) * 256] = _dot(h, w_ref[:, 2 * nk + c * 256:2 * nk + (c + 1) * 256]).astype(BF16)
        r = _dot(h, w_ref[:, 2 * nk + nv + c * 256:2 * nk + nv + (c + 1) * 256])
        sr_ref[:, c * 256:(c + 1) * 256] = jax.nn.silu(r).astype(BF16)
    low = _dot(h, w_ref[:, 2 * nk + 2 * nv:2 * nk + 2 * nv + LANES]).astype(BF16)
    gz = _dot(low, wg_ref[...]) + bg_ref[...]
    log_sig = jnp.minimum(gz, 0.0) - jnp.log1p(jnp.exp(-jnp.abs(gz)))
    la_ref[...] = log_sig / GLA_GATE_TEMP


def _gla_proj(x, g, w, wg, bg):
    m, d = x.shape
    nk = GLA_HEADS * GLA_DK
    nv = GLA_HEADS * GLA_DV
    row = lambda i: (i, 0)
    return pl.pallas_call(
        _gla_proj_body, grid=(m // TM,),
        in_specs=[pl.BlockSpec((TM, d), row), _resident(g.shape), _resident(w.shape),
                  _resident(wg.shape), _resident(bg.shape)],
        out_specs=[pl.BlockSpec((TM, nk), row), pl.BlockSpec((TM, nk), row), pl.BlockSpec((TM, nv), row),
                   pl.BlockSpec((TM, nv), row), pl.BlockSpec((TM, nk), row)],
        out_shape=[jax.ShapeDtypeStruct((m, nk), F32), jax.ShapeDtypeStruct((m, nk), F32),
                   jax.ShapeDtypeStruct((m, nv), BF16), jax.ShapeDtypeStruct((m, nv), BF16),
                   jax.ShapeDtypeStruct((m, nk), F32)],
        compiler_params=_cparams(("arbitrary",)), name="gla_proj",
    )(x, g, w, wg, bg)


def _cumsum_rows(x):
    n = x.shape[0]
    row = lax.broadcasted_iota(jnp.int32, x.shape, 0)
    sh = 1
    while sh < n:
        x = x + jnp.where(row >= sh, pltpu.roll(x, sh, axis=0), 0.0)
        sh *= 2
    return x


def _gla_chunk(q, k, la, v, s_old):
    cs = q.shape[0]
    row = lax.broadcasted_iota(jnp.int32, (cs, 1), 0)
    trow = lax.broadcasted_iota(jnp.int32, (cs, cs), 0)
    scol = lax.broadcasted_iota(jnp.int32, (cs, cs), 1)
    r8 = lax.broadcasted_iota(jnp.int32, (SUBLANES, 1), 0)
    lane8 = lax.broadcasted_iota(jnp.int32, (SUBLANES, cs), 1)
    cum = _cumsum_rows(la * math.log2(math.e))
    last = cum[cs - 1:cs, :]
    out = _dot((q * jnp.exp2(cum)).astype(BF16), s_old.astype(BF16))
    att = None
    hh = cs // 2
    while hh >= SUBLANES:
        nblk = cs // (2 * hh)
        ref = jnp.concatenate(
            [jnp.broadcast_to(cum[u * 2 * hh + hh - 1:u * 2 * hh + hh, :], (2 * hh, GLA_DK)) for u in range(nblk)],
            axis=0) if nblk > 1 else jnp.broadcast_to(cum[hh - 1:hh, :], (cs, GLA_DK))
        second = (row % (2 * hh)) >= hh
        qh = jnp.where(second, q * jnp.exp2(cum - ref), 0.0).astype(BF16)
        kh = jnp.where(second, 0.0, k * jnp.exp2(ref - cum)).astype(BF16)
        a = _nt(qh, kh)
        if nblk > 1:
            a = jnp.where(trow // (2 * hh) == scol // (2 * hh), a, 0.0)
        att = a if att is None else att + a
        hh //= 2
    slabs = []
    for g8 in range(cs // SUBLANES):
        r0 = g8 * SUBLANES
        cg, qg, kg = cum[r0:r0 + SUBLANES], q[r0:r0 + SUBLANES], k[r0:r0 + SUBLANES]
        slab = jnp.zeros((SUBLANES, GLA_DV if att is None else cs), F32)
        for s in range(SUBLANES):
            e = jnp.exp2(cg - cg[s:s + 1, :])
            col = jnp.sum(e * qg * kg[s:s + 1, :], axis=1, keepdims=True)
            col = jnp.where(r8 >= s, col, 0.0)
            if att is None:
                slab = slab + col * v[r0 + s:r0 + s + 1, :].astype(F32)
            else:
                slab = jnp.where(lane8 == r0 + s, col, slab)
        slabs.append(slab)
    if att is None:
        out = out + slabs[0]
    else:
        att = att + jnp.concatenate(slabs, axis=0)
        out = out + _dot(att.astype(BF16), v)
    kt = (k * jnp.exp2(last - cum)).astype(BF16)
    dcol = jnp.broadcast_to(jnp.exp2(last), (SUBLANES, GLA_DK)).T[:, 0:1]
    return out, dcol * s_old + _tn(kt, v)


def _gla_rec_prompt_body(q_ref, k_ref, la_ref, v_ref, sr_ref, gn_ref, o_ref, so_ref, st_ref):
    c = pl.program_id(1)

    @pl.when(c == 0)
    def _():
        st_ref[...] = jnp.zeros(st_ref.shape, F32)

    for hd in range(GLA_HEADS):
        ksl = slice(hd * GLA_DK, (hd + 1) * GLA_DK)
        vsl = slice(hd * GLA_DV, (hd + 1) * GLA_DV)
        out, s_new = _gla_chunk(q_ref[:, ksl], k_ref[:, ksl], la_ref[:, ksl], v_ref[:, vsl], st_ref[hd])
        st_ref[hd] = s_new
        o_ref[:, vsl] = (_rms(out, gn_ref[...]) * sr_ref[:, vsl].astype(F32)).astype(o_ref.dtype)

    @pl.when(c == pl.num_programs(1) - 1)
    def _():
        so_ref[0] = st_ref[...]


def _gla_rec_sample_body(q_ref, k_ref, la_ref, v_ref, sr_ref, gn_ref, s0_ref, o_ref, so_ref, *, sb):
    for s in range(sb):
        rsl = slice(s * SUBLANES, (s + 1) * SUBLANES)
        for hd in range(GLA_HEADS):
            ksl = slice(hd * GLA_DK, (hd + 1) * GLA_DK)
            vsl = slice(hd * GLA_DV, (hd + 1) * GLA_DV)
            out, s_new = _gla_chunk(q_ref[rsl, ksl], k_ref[rsl, ksl], la_ref[rsl, ksl], v_ref[rsl, vsl], s0_ref[s, hd])
            so_ref[s, hd] = s_new
            o_ref[rsl, vsl] = (_rms(out, gn_ref[...]) * sr_ref[rsl, vsl].astype(F32)).astype(o_ref.dtype)


def _gla_rec(q, k, la, v, sr, gn, nb, chunk, s0=None):
    m = q.shape[0]
    nk = GLA_HEADS * GLA_DK
    nv = GLA_HEADS * GLA_DV
    out_shape = [jax.ShapeDtypeStruct((m, nv), BF16), jax.ShapeDtypeStruct((nb, GLA_HEADS, GLA_DK, GLA_DV), F32)]
    if s0 is None:
        nc = m // nb // chunk
        row = lambda b, c: (b * nc + c, 0)
        st_spec = pl.BlockSpec((1, GLA_HEADS, GLA_DK, GLA_DV), lambda b, c: (b, 0, 0, 0))
        return pl.pallas_call(
            _gla_rec_prompt_body, grid=(nb, nc),
            in_specs=[pl.BlockSpec((chunk, nk), row), pl.BlockSpec((chunk, nk), row), pl.BlockSpec((chunk, nk), row),
                      pl.BlockSpec((chunk, nv), row), pl.BlockSpec((chunk, nv), row),
                      pl.BlockSpec(gn.shape, lambda b, c: (0, 0))],
            out_specs=[pl.BlockSpec((chunk, nv), row), st_spec], out_shape=out_shape,
            scratch_shapes=[pltpu.VMEM((GLA_HEADS, GLA_DK, GLA_DV), F32)],
            compiler_params=_cparams(("arbitrary", "arbitrary")), name="gla_rec_prompt",
        )(q, k, la, v, sr, gn)
    sb = GLA_SEQS
    rows = sb * chunk
    row = lambda b: (b, 0)
    st_spec = pl.BlockSpec((sb, GLA_HEADS, GLA_DK, GLA_DV), lambda b: (b, 0, 0, 0))
    return pl.pallas_call(
        functools.partial(_gla_rec_sample_body, sb=sb), grid=(nb // sb,),
        in_specs=[pl.BlockSpec((rows, nk), row), pl.BlockSpec((rows, nk), row), pl.BlockSpec((rows, nk), row),
                  pl.BlockSpec((rows, nv), row), pl.BlockSpec((rows, nv), row),
                  pl.BlockSpec(gn.shape, lambda b: (0, 0)), st_spec],
        out_specs=[pl.BlockSpec((rows, nv), row), st_spec], out_shape=out_shape,
        compiler_params=_cparams(("arbitrary",)), name="gla_rec_sample",
    )(q, k, la, v, sr, gn, s0)


def _rope_tables(pos):
    half = HEAD_DIM // 2
    inv = ROPE_THETA ** (-jnp.arange(half, dtype=F32) / half)
    ang = pos.astype(F32)[:, None] * inv[None, :]
    cos, sin = jnp.cos(ang), jnp.sin(ang)
    z = jnp.zeros_like(sin)
    cos_t = jnp.tile(cos, (1, 4))
    sa = jnp.tile(jnp.concatenate([-sin, z], axis=1), (1, 2))
    sb = jnp.tile(jnp.concatenate([z, sin], axis=1), (1, 2))
    return cos_t, sa, sb


def _pad_cols(w, n):
    return jnp.pad(w, ((0, 0), (0, n - w.shape[1])))


def _cmp_weights(cmp_pe, cmp_w):
    g = NSA_KV_HEADS
    w = cmp_w.reshape(2, 2, CMP_STRIDE, HEAD_DIM, HEAD_DIM)
    eye = jnp.eye(g, dtype=F32)
    wbd = jnp.einsum('shjde,gk->sjgdhke', w, eye)
    wbd = wbd.reshape(2, CMP_STRIDE, g * HEAD_DIM, 2 * g * HEAD_DIM).astype(BF16)
    pe = cmp_pe.reshape(2, 2, CMP_STRIDE, 1, 1, HEAD_DIM)
    pe_t = jnp.broadcast_to(pe, (2, 2, CMP_STRIDE, SUBLANES, g, HEAD_DIM)).reshape(2, 2, CMP_STRIDE, SUBLANES, g * HEAD_DIM)
    return wbd, pe_t


def _overlap_t(n_blk, n_sel, rows, cols):
    i = jnp.arange(cols)[None, :]
    j = jnp.arange(rows)[:, None]
    ov = (i * CMP_STRIDE + 2 * CMP_STRIDE > j * SEL_BLOCK) & (i * CMP_STRIDE < (j + 1) * SEL_BLOCK)
    ov = ov & (i < n_blk) & (j < n_sel)
    return ov.astype(BF16)


def _nsa_layer_prompt(x, g, w_in, wbd, cbias, w_o, tabs, b, t):
    q, gates, rows_t, win_t, k_slc, vt_slc, k_win, vt_win = _nsa_proj_t(x, g[0:1], w_in, *tabs, b, t)
    fs = _compress_pages(rows_t.reshape(b, 4, NSA_KV_HEADS * HEAD_DIM, t), wbd, cbias)
    n_blk = t // CMP_STRIDE - 1
    n_sel = -(-t // SEL_BLOCK)
    oc, qp = _cmp_prompt(q, fs, _overlap_t(n_blk, n_sel, n_sel, t // CMP_STRIDE), b, t)
    o_s, o_w = _flash(qp, k_slc, vt_slc, k_win, vt_win, WINDOW)
    return ([oc, o_s, o_w], gates, w_o, True), rows_t, win_t


def _to_rows(o, nb, dec):
    o = o.reshape(nb, NSA_KV_HEADS, dec, NSA_GROUP, HEAD_DIM).transpose(2, 0, 1, 3, 4)
    return o.reshape(dec * nb, NSA_HEADS * HEAD_DIM).astype(BF16)


def _nsa_layer_sample(x, g, w_in, wbd, cbias, w_o, tabs, cache, page_table, win_state, nb, dec):
    n_phys, page = cache.shape[0], cache.shape[1]
    n_pages = page_table.shape[1]
    past_len = n_pages * page
    q, rows, win, gates = _nsa_proj(x, g[0:1], w_in, *tabs)
    pages_t = cache.transpose(0, 2, 3, 4, 1).reshape(n_phys, 4, NSA_KV_HEADS * HEAD_DIM, page)
    wb = win_state.shape[1]
    win_t = win_state.transpose(0, 2, 3, 4, 1).reshape(nb, 2, NSA_KV_HEADS * HEAD_DIM, wb)
    fs_phys = _compress_pages(pages_t, wbd, cbias)
    q5 = q.reshape(dec, nb, NSA_KV_HEADS, NSA_GROUP, HEAD_DIM).transpose(1, 2, 0, 3, 4)
    eye = jnp.eye(NSA_KV_HEADS, dtype=BF16)
    qt = q5.reshape(nb, NSA_KV_HEADS, dec * NSA_GROUP, 1, HEAD_DIM) * eye[None, :, None, :, None]
    qt = qt.reshape(nb, NSA_KV_HEADS * dec * NSA_GROUP, NSA_KV_HEADS * HEAD_DIM)
    qt = jnp.pad(qt, ((0, 0), (0, LANES - qt.shape[1]), (0, 0)))
    n_blk = (past_len + dec) // CMP_STRIDE - 1
    n_sel = -(-(past_len + dec) // SEL_BLOCK)
    nj = -(-n_sel // SUBLANES) * SUBLANES
    lane = jnp.arange(LANES)
    msum = (lane[:, None] // NSA_GROUP == lane[None, :] // NSA_GROUP).astype(BF16)
    oc, a_t = _cmp_sample(page_table, fs_phys, qt, _overlap_t(n_blk, n_sel, nj, past_len // CMP_STRIDE), msum, past_len, dec)
    t_q = ((lane // NSA_GROUP) % dec)[None, :]
    j8 = jnp.arange(SUBLANES)[:, None]
    new_mask = jnp.where((j8 < dec) & (j8 <= t_q), 0.0, NEG).astype(F32)
    win_mask = jnp.where(jnp.arange(wb)[:, None] >= t_q + wb - WINDOW, 0.0, NEG).astype(F32)
    qbd = qt.transpose(0, 2, 1)

    def new_rows(a):
        a = a.reshape(dec, nb, 512).transpose(1, 0, 2)
        return jnp.pad(a, ((0, 0), (0, SUBLANES - dec), (0, 0)))

    def from_t(o):
        o = o[:, :, :NSA_KV_HEADS * dec * NSA_GROUP].reshape(nb, NSA_KV_HEADS, HEAD_DIM, NSA_KV_HEADS, dec, NSA_GROUP)
        o = jnp.diagonal(o, axis1=1, axis2=3)
        return o.transpose(2, 0, 4, 3, 1).reshape(dec * nb, NSA_HEADS * HEAD_DIM).astype(BF16)

    (o_s,) = _attn_sample(pages_t, 1, n_pages, new_rows(rows[:, 512:1024]), qbd, new_mask, SLC_SEQS, dec,
                          page_table=page_table, blk_mask=a_t)
    o_w, win_next = _attn_sample(win_t, 0, 1, new_rows(win), qbd, new_mask, WIN_SEQS, dec,
                                 past_mask=win_mask, emit_state=True)
    mix = ([_to_rows(oc, nb, dec), from_t(o_s), from_t(o_w)], gates, w_o, False)
    win_next = win_next.reshape(nb, 2, NSA_KV_HEADS, HEAD_DIM, wb).transpose(0, 4, 1, 2, 3)
    return mix, rows, win_next


def _gla_layer(x, g, w_in, wg, bg, gn, w_o, nb, seq, s0=None):
    q, k, v, sr, la = _gla_proj(x, g[0:1], w_in, wg, bg)
    if s0 is None:
        o, s = _gla_rec(q, k, la, v, sr, gn, nb, GLA_CHUNK)
    else:
        def seqs(a):
            a = a.reshape(seq, nb, a.shape[1]).transpose(1, 0, 2)
            return jnp.pad(a, ((0, 0), (0, SUBLANES - seq), (0, 0))).reshape(nb * SUBLANES, a.shape[2])
        o, s = _gla_rec(seqs(q), seqs(k), seqs(la), seqs(v), seqs(sr), gn, nb, SUBLANES, s0=s0)
        o = o.reshape(nb, SUBLANES, o.shape[1])[:, :seq].transpose(1, 0, 2).reshape(seq * nb, o.shape[1])
    return ([o], None, w_o, False), s


def kernel(x_prompt, x_sample, cache_nsa_kv, state_win_kv, state_gla, state_ffn_conv, page_table, norm_gain, nsa_w_in, nsa_cmp_pe, nsa_cmp_w, nsa_w_o, gla_w_in, gla_w_gate_up, gla_b_gate, gla_norm_gain, gla_w_o, ffn_w_up, ffn_conv_w, ffn_conv_b, ffn_w_down):
    b, t, d = x_prompt.shape
    nb, dec, _ = x_sample.shape
    depth = norm_gain.shape[0]
    f = ffn_w_down.shape[1]
    past_len = page_table.shape[1] * cache_nsa_kv.shape[2]

    xp = x_prompt.reshape(b * t, d)
    xs = x_sample.transpose(1, 0, 2).reshape(dec * nb, d)
    tabs_p = _rope_tables(jnp.arange(t, dtype=jnp.int32))
    tabs_s = _rope_tables(past_len + jnp.repeat(jnp.arange(dec, dtype=jnp.int32), nb))

    nsa_p, nsa_s, win_p, win_s, gla_p, gla_s, ffn_p, ffn_s = [], [], [], [], [], [], [], []
    for i in range(depth):
        g = norm_gain[i]
        a = i // 2
        if i % 2 == 0:
            w_in = _pad_cols(nsa_w_in[a], 2688).astype(BF16)
            w_o = nsa_w_o[a].astype(BF16)
            wbd, pe_t = _cmp_weights(nsa_cmp_pe[a], nsa_cmp_w[a])
            cbias = _cmp_bias(pe_t, wbd)
            mix_p, rows_t, win_t = _nsa_layer_prompt(xp, g, w_in, wbd, cbias, w_o, tabs_p, b, t)
            nsa_p.append(rows_t.reshape(b, 4, NSA_KV_HEADS, HEAD_DIM, t).transpose(0, 4, 1, 2, 3))
            nw = min(WINDOW, t)
            win_p.append(win_t.reshape(b, 2, NSA_KV_HEADS, HEAD_DIM, t)[..., t - nw:].transpose(0, 4, 1, 2, 3))
            mix_s, rows, win_next = _nsa_layer_sample(xs, g, w_in, wbd, cbias, w_o, tabs_s, cache_nsa_kv[a], page_table,
                                                      state_win_kv[a], nb, dec)
            nsa_s.append(rows.reshape(dec, nb, 4, NSA_KV_HEADS, HEAD_DIM).transpose(1, 0, 2, 3, 4))
            win_s.append(win_next)
        else:
            w_in = _pad_cols(gla_w_in[a], 3200).astype(BF16)
            wg = jnp.pad(gla_w_gate_up[a], ((0, LANES - GLA_GATE_RANK), (0, 0))).astype(BF16)
            bg = gla_b_gate[a][None, :]
            gn = gla_norm_gain[a][None, :]
            w_o = gla_w_o[a].astype(BF16)
            mix_p, s = _gla_layer(xp, g, w_in, wg, bg, gn, w_o, b, t)
            gla_p.append(s)
            mix_s, s = _gla_layer(xs, g, w_in, wg, bg, gn, w_o, nb, dec, s0=state_gla[a])
            gla_s.append(s)
        wu = ffn_w_up[i].astype(BF16)
        wd = ffn_w_down[i].astype(BF16)
        cw = jnp.pad(ffn_conv_w[i], ((0, SUBLANES - ffn_conv_w.shape[1]), (0, 0)))
        cb = ffn_conv_b[i][None, :]
        xp, tail = _ffn(xp, mix_p, g, wu, cw, cb, wd, seq_len=t)
        tail = tail.reshape(b, t // FFN_TM, SUBLANES, f)[:, -1, SUBLANES - 2:]
        ffn_p.append(tail)
        buf = state_ffn_conv[i].transpose(1, 0, 2).reshape(2 * nb, f)
        xs, tail = _ffn(xs, mix_s, g, wu, cw, cb, wd, buf=buf)
        ffn_s.append(tail.reshape(2, nb, f).transpose(1, 0, 2))

    y_prompt = xp.reshape(b, t, d)
    y_sample = xs.reshape(dec, nb, d).transpose(1, 0, 2)
    return (y_prompt, y_sample, jnp.stack(nsa_p), jnp.stack(nsa_s), jnp.stack(win_p), jnp.stack(win_s),
            jnp.stack(gla_p), jnp.stack(gla_s), jnp.stack(ffn_p), jnp.stack(ffn_s))
```

```python
import functools
import math

import jax
import jax.numpy as jnp
from jax import lax
from jax.experimental import pallas as pl
from jax.experimental.pallas import tpu as pltpu

F32, BF16 = jnp.float32, jnp.bfloat16

HEAD_DIM = 64
NSA_HEADS = 16
NSA_KV_HEADS = 4
NSA_GROUP = 4
CMP_STRIDE = 16
SEL_BLOCK = 64
TOP_N = 16
WINDOW = 512
ROPE_THETA = 10000.0
GLA_HEADS = 4
GLA_DK = 128
GLA_DV = 256
GLA_GATE_RANK = 16
GLA_GATE_TEMP = 16.0
NORM_EPS = 1e-6
FORCE = 1e6
NEG = -1e30

LANES = 128
SUBLANES = 8
VMEM_LIMIT = 48 * 1024 * 1024

TM = 512
TQ = 256
VT_ROWS = 80
CMP_ROWS = 256
GLA_CHUNK = 128
SLC_SEQS = 2
WIN_SEQS = 4
CMP_SEQS = 4
GLA_SEQS = 4
FFN_FCHUNK = 2816
FFN_TM = 512


def _cparams(sem):
    return pltpu.CompilerParams(dimension_semantics=sem, vmem_limit_bytes=VMEM_LIMIT)


def _resident(shape):
    nd = len(shape)
    return pl.BlockSpec(shape, lambda *_: (0,) * nd, pipeline_mode=pl.Buffered(1))


def _rms(x, g):
    return x * lax.rsqrt(jnp.mean(x * x, axis=-1, keepdims=True) + NORM_EPS) * g


def _nt(a, b):
    return lax.dot_general(a, b, (((1,), (1,)), ((), ())), preferred_element_type=F32)


def _tn(a, b):
    return lax.dot_general(a, b, (((0,), (0,)), ((), ())), preferred_element_type=F32)


def _dot(a, b):
    return jnp.dot(a, b, preferred_element_type=F32)


def _split_dot(w, x):
    hi = x.astype(BF16)
    lo = (x - hi.astype(F32)).astype(BF16)
    return _dot(w, hi) + _dot(w, lo)


def _softmax_masked(s, mask, axis):
    s = jnp.where(mask, s, NEG)
    m = jnp.max(s, axis=axis, keepdims=True)
    e = jnp.where(mask, jnp.exp2(s - m), 0.0)
    return e / jnp.maximum(jnp.sum(e, axis=axis, keepdims=True), 1e-30)


Q_SCALE = HEAD_DIM ** -0.5 * math.log2(math.e)


def _nsa_proj_parts(x_ref, g_ref, w_ref, cos_ref, sa_ref, sb_ref, q_ref, gate_ref):
    h = _rms(x_ref[...], g_ref[...]).astype(BF16)
    cos, sa, sb = cos_ref[...], sa_ref[...], sb_ref[...]

    def rope(z):
        return z * cos + pltpu.roll(z, 96, axis=1) * sa + pltpu.roll(z, 32, axis=1) * sb

    def proj(c0):
        return _dot(h, w_ref[:, c0:c0 + 256])

    def rope256(z):
        return jnp.concatenate([rope(z[:, :LANES]), rope(z[:, LANES:])], axis=1)

    for c in range(4):
        q_ref[:, c * 256:(c + 1) * 256] = (rope256(proj(c * 256)) * Q_SCALE).astype(BF16)
    gate_ref[...] = jax.nn.sigmoid(_dot(h, w_ref[:, 2560:2688]))
    return (rope256(proj(1024)), proj(1280), rope256(proj(1536)), proj(1792), rope256(proj(2048)), proj(2304))


def _nsa_proj_body(x_ref, g_ref, w_ref, cos_ref, sa_ref, sb_ref, q_ref, rows_ref, win_ref, gate_ref):
    parts = _nsa_proj_parts(x_ref, g_ref, w_ref, cos_ref, sa_ref, sb_ref, q_ref, gate_ref)
    for c in range(4):
        rows_ref[:, c * 256:(c + 1) * 256] = parts[c]
    win_ref[:, 0:256] = parts[4]
    win_ref[:, 256:512] = parts[5]


def _nsa_proj_t_body(x_ref, g_ref, w_ref, cos_ref, sa_ref, sb_ref, q_ref, gate_ref,
                     rows_t_ref, win_t_ref, ks_ref, vs_ref, kw_ref, vw_ref, *, tiles_per_seq, n_sel):
    kc, vc, ks, vs, kw, vw = _nsa_proj_parts(x_ref, g_ref, w_ref, cos_ref, sa_ref, sb_ref, q_ref, gate_ref)
    tm = kc.shape[0]
    for c, part in enumerate((kc, vc, ks, vs)):
        rows_t_ref[0, c * 256:(c + 1) * 256, :] = part.T
    win_t_ref[0, 0:256, :] = kw.T
    win_t_ref[0, 256:512, :] = vw.T
    ones = jnp.ones((NSA_KV_HEADS, VT_ROWS - HEAD_DIM, tm), BF16)
    vs_ref[0] = jnp.concatenate([vs.T.reshape(NSA_KV_HEADS, HEAD_DIM, tm).astype(BF16), ones], axis=1)
    vw_ref[0] = jnp.concatenate([vw.T.reshape(NSA_KV_HEADS, HEAD_DIM, tm).astype(BF16), ones], axis=1)
    t = (pl.program_id(0) % tiles_per_seq) * tm + lax.broadcasted_iota(jnp.int32, (tm, HEAD_DIM), 0)
    lane = lax.broadcasted_iota(jnp.int32, (tm, HEAD_DIM), 1)
    onehot = jnp.where(jnp.where(lane < n_sel, t // SEL_BLOCK, -1) == lane, 1.0, 0.0).astype(BF16)
    zeros = jnp.zeros((tm, HEAD_DIM), BF16)
    for g in range(NSA_KV_HEADS):
        sl = slice(g * HEAD_DIM, (g + 1) * HEAD_DIM)
        ks_ref[0, g] = jnp.concatenate([ks[:, sl].astype(BF16), onehot], axis=1)
        kw_ref[0, g] = jnp.concatenate([kw[:, sl].astype(BF16), zeros], axis=1)


def _nsa_proj_t(x, g, w, cos, sa, sb, b, t):
    m, d = x.shape
    nt = t // TM
    n_sel = -(-t // SEL_BLOCK)
    row = lambda i: (i, 0)
    tab = lambda i: (i % nt, 0)
    fm = lambda i: (i // nt, 0, i % nt)
    kmap = lambda i: (i // nt, 0, i % nt, 0)
    vmap = lambda i: (i // nt, 0, 0, i % nt)
    kv = NSA_KV_HEADS
    return pl.pallas_call(
        functools.partial(_nsa_proj_t_body, tiles_per_seq=nt, n_sel=n_sel),
        grid=(m // TM,),
        in_specs=[pl.BlockSpec((TM, d), row), _resident(g.shape), _resident(w.shape),
                  pl.BlockSpec((TM, LANES), tab), pl.BlockSpec((TM, LANES), tab), pl.BlockSpec((TM, LANES), tab)],
        out_specs=[pl.BlockSpec((TM, 1024), row), pl.BlockSpec((TM, LANES), row),
                   pl.BlockSpec((1, 1024, TM), fm), pl.BlockSpec((1, 512, TM), fm),
                   pl.BlockSpec((1, kv, TM, LANES), kmap), pl.BlockSpec((1, kv, VT_ROWS, TM), vmap),
                   pl.BlockSpec((1, kv, TM, LANES), kmap), pl.BlockSpec((1, kv, VT_ROWS, TM), vmap)],
        out_shape=[jax.ShapeDtypeStruct((m, 1024), BF16), jax.ShapeDtypeStruct((m, LANES), F32),
                   jax.ShapeDtypeStruct((b, 1024, t), F32), jax.ShapeDtypeStruct((b, 512, t), F32),
                   jax.ShapeDtypeStruct((b, kv, t, LANES), BF16), jax.ShapeDtypeStruct((b, kv, VT_ROWS, t), BF16),
                   jax.ShapeDtypeStruct((b, kv, t, LANES), BF16), jax.ShapeDtypeStruct((b, kv, VT_ROWS, t), BF16)],
        compiler_params=_cparams(("arbitrary",)),
        name="nsa_proj_t",
    )(x, g, w, cos, sa, sb)


def _nsa_proj(x, g, w, cos, sa, sb):
    m, d = x.shape
    ntab = cos.shape[0] // TM
    row = lambda i: (i, 0)
    tab = lambda i: (i % ntab, 0)
    return pl.pallas_call(
        _nsa_proj_body,
        grid=(m // TM,),
        in_specs=[pl.BlockSpec((TM, d), row), _resident(g.shape), _resident(w.shape),
                  pl.BlockSpec((TM, LANES), tab), pl.BlockSpec((TM, LANES), tab), pl.BlockSpec((TM, LANES), tab)],
        out_specs=[pl.BlockSpec((TM, 1024), row), pl.BlockSpec((TM, 1024), row),
                   pl.BlockSpec((TM, 512), row), pl.BlockSpec((TM, LANES), row)],
        out_shape=[jax.ShapeDtypeStruct((m, 1024), BF16), jax.ShapeDtypeStruct((m, 1024), F32),
                   jax.ShapeDtypeStruct((m, 512), F32), jax.ShapeDtypeStruct((m, LANES), F32)],
        compiler_params=_cparams(("arbitrary",)),
        name="nsa_proj",
    )(x, g, w, cos, sa, sb)


def _cmp_bias_body(pe_ref, w_ref, o_ref):
    for s in range(2):
        for half in range(2):
            acc = jnp.zeros((SUBLANES, 256), F32)
            for j in range(CMP_STRIDE):
                acc = acc + _dot(pe_ref[s, half, j].astype(BF16), w_ref[s, j, :, half * 256:(half + 1) * 256])
            o_ref[:, s * 512 + half * 256:s * 512 + (half + 1) * 256] = acc


def _cmp_bias(pe_t, wbd):
    return pl.pallas_call(_cmp_bias_body, out_shape=jax.ShapeDtypeStruct((SUBLANES, 1024), F32),
                          compiler_params=_cparams(None), name="cmp_bias")(pe_t, wbd)


CMP_HALF = CMP_ROWS // 2
CMP_PITCH = CMP_HALF + SUBLANES


def _compress_pages_body(x_ref, w_ref, b_ref, o_ref, xa_ref, xb_ref):
    per_page = LANES // CMP_STRIDE
    pages_per_block = x_ref.shape[3] // LANES
    pages_half = CMP_HALF // per_page

    def move(xs_ref, p0):
        for q in range(pages_half):
            blk, off = divmod(p0 + q, pages_per_block)
            for s in range(2):
                xt = x_ref[blk, s, :, off * LANES:(off + 1) * LANES].T
                for c in range(per_page):
                    for half in range(2):
                        xs_ref[2 * s + half, pl.ds(q * per_page + c, CMP_STRIDE, stride=CMP_PITCH), :] = (
                            xt[c * CMP_STRIDE:(c + 1) * CMP_STRIDE, half * LANES:(half + 1) * LANES])

    def project(xs_ref, r0):
        for s in range(2):
            acc = jnp.zeros((CMP_HALF, 512), F32)
            for j in range(CMP_STRIDE):
                xj = jnp.concatenate([xs_ref[2 * s, j * CMP_PITCH:j * CMP_PITCH + CMP_HALF, :],
                                      xs_ref[2 * s + 1, j * CMP_PITCH:j * CMP_PITCH + CMP_HALF, :]], axis=1)
                acc = acc + _dot(xj.astype(BF16), w_ref[s, j])
            o_ref[r0:r0 + CMP_HALF, s * 512:(s + 1) * 512] = acc + b_ref[0:1, s * 512:(s + 1) * 512]

    move(xa_ref, 0)
    move(xb_ref, pages_half)
    project(xa_ref, 0)
    project(xb_ref, CMP_HALF)


def _compress_pages(pages_t, wbd, bias):
    n, _, _, npos = pages_t.shape
    chunks = npos // CMP_STRIDE
    pp = CMP_ROWS // chunks
    return pl.pallas_call(
        _compress_pages_body,
        grid=(n // pp,),
        in_specs=[pl.BlockSpec((pp, 2, 256, npos), lambda i: (i, 0, 0, 0)),
                  _resident(wbd.shape), _resident(bias.shape)],
        out_specs=pl.BlockSpec((CMP_ROWS, 1024), lambda i: (i, 0)),
        out_shape=jax.ShapeDtypeStruct((n * chunks, 1024), F32),
        scratch_shapes=[pltpu.VMEM((4, CMP_STRIDE * CMP_PITCH, LANES), F32),
                        pltpu.VMEM((4, CMP_STRIDE * CMP_PITCH, LANES), F32)],
        compiler_params=_cparams(("arbitrary",)),
        name="compress_pages",
    )(pages_t, wbd, bias)


def _finish_compress(fs):
    n = fs.shape[0]
    kc = fs[:, 0:256] + pltpu.roll(fs[:, 256:512], n - 1, axis=0)
    vc = fs[:, 512:768] + pltpu.roll(fs[:, 768:1024], n - 1, axis=0)
    return kc.astype(BF16), vc.astype(BF16)


def _select_mask(sc_raw, jr, pos, n_real):
    sc = jnp.where(jr * SEL_BLOCK > pos, -FORCE, sc_raw)
    sc = jnp.where(jr == pos // SEL_BLOCK, FORCE, jnp.where(jr == 0, FORCE, sc))
    sc = jnp.where(jr >= n_real, -2.0 * FORCE, sc)
    cnt = jnp.zeros(sc.shape, F32)
    for i in range(n_real):
        ri = sc[i:i + 1, :]
        ge = jnp.where(ri >= sc, 1.0, 0.0)
        gt = jnp.where(ri > sc, 1.0, 0.0)
        cnt = cnt + jnp.where(jr > i, ge, gt)
    keep = jnp.where(sc > -0.5 * FORCE, 0.0, NEG)
    return jnp.where(cnt < float(min(TOP_N, n_real)), keep, NEG)


def _cmp_prompt_body(q_ref, fs_ref, ov_ref, oc_ref, qp_ref):
    i = pl.program_id(1)
    tq = q_ref.shape[0]
    nblk = fs_ref.shape[0]
    fs = fs_ref[...]
    kc = (fs[:, 0:256] + pltpu.roll(fs[:, 256:512], nblk - 1, axis=0)).astype(BF16)
    vc_t = (fs[:, 512:768] + pltpu.roll(fs[:, 768:1024], nblk - 1, axis=0)).T.astype(BF16)
    t_row = i * tq + lax.broadcasted_iota(jnp.int32, (nblk, tq), 1)
    n_col = lax.broadcasted_iota(jnp.int32, (nblk, tq), 0)
    vis_t = n_col * CMP_STRIDE + 2 * CMP_STRIDE - 1 <= t_row
    nsel = ov_ref.shape[0]
    jr = lax.broadcasted_iota(jnp.int32, (nsel, tq), 0)
    pos = i * tq + lax.broadcasted_iota(jnp.int32, (nsel, tq), 1)
    eye = jnp.where(lax.broadcasted_iota(jnp.int32, (tq, tq), 0) == lax.broadcasted_iota(jnp.int32, (tq, tq), 1),
                    1.0, 0.0).astype(BF16)
    for g in range(NSA_KV_HEADS):
        kg = kc[:, g * HEAD_DIM:(g + 1) * HEAD_DIM]
        vg_t = vc_t[g * HEAD_DIM:(g + 1) * HEAD_DIM, :]
        ps_t = jnp.zeros((nblk, tq), F32)
        qs = []
        for r in range(NSA_GROUP):
            hd = g * NSA_GROUP + r
            qh = q_ref[:, hd * HEAD_DIM:(hd + 1) * HEAD_DIM]
            qs.append(qh)
            p_t = _softmax_masked(_nt(kg, qh), vis_t, 0)
            oc_ref[0, hd * HEAD_DIM:(hd + 1) * HEAD_DIM, :] = _dot(vg_t, p_t.astype(BF16)).astype(oc_ref.dtype)
            ps_t = ps_t + p_t
        a_t = _select_mask(_split_dot(ov_ref[...], ps_t), jr, pos, nsel)
        a_t = jnp.concatenate([a_t, jnp.zeros((HEAD_DIM - nsel, tq), F32)], axis=0).astype(BF16)
        a = _nt(eye, a_t).astype(BF16)
        for r in range(NSA_GROUP):
            qp_ref[0, g * NSA_GROUP + r] = jnp.concatenate([qs[r], a], axis=1)


def _cmp_prompt(q, fs, ov_t, b, t):
    nq = t // TQ
    nblk = t // CMP_STRIDE
    return pl.pallas_call(
        _cmp_prompt_body,
        grid=(b, nq),
        in_specs=[pl.BlockSpec((TQ, 1024), lambda bi, i: (bi * nq + i, 0)),
                  pl.BlockSpec((nblk, 1024), lambda bi, i: (bi, 0)),
                  _resident(ov_t.shape)],
        out_specs=[pl.BlockSpec((1, 1024, TQ), lambda bi, i: (bi, 0, i)),
                   pl.BlockSpec((1, NSA_HEADS, TQ, LANES), lambda bi, i: (bi, 0, i, 0))],
        out_shape=[jax.ShapeDtypeStruct((b, 1024, t), BF16),
                   jax.ShapeDtypeStruct((b, NSA_HEADS, t, LANES), BF16)],
        compiler_params=_cparams(("arbitrary", "arbitrary")),
        name="cmp_prompt",
    )(q, fs, ov_t)


FLASH_RUN = 4


def _flash_body(qp_ref, ks_ref, vs_ref, kw_ref, vw_ref, os_ref, ow_ref, st_ref, acc_ref, *, window):
    _flash_branch(qp_ref, ks_ref, vs_ref, os_ref, st_ref, acc_ref, window=None)
    _flash_branch(qp_ref, kw_ref, vw_ref, ow_ref, st_ref, acc_ref, window=window)


def _flash_branch(qp_ref, kp_ref, vt_ref, o_ref, st_ref, acc_ref, *, window):
    i = pl.program_id(2)
    tq = qp_ref.shape[2]
    dq = (i * tq + lax.broadcasted_iota(jnp.int32, (tq, tq), 1)) - lax.broadcasted_iota(jnp.int32, (tq, tq), 0)

    def run(c, masks, m):
        n = len(masks)
        off = pl.multiple_of(c * tq, tq)
        vt = vt_ref[0, 0, :, pl.ds(off, n * tq)]
        out = list(m)
        for u, (causal, far) in enumerate(masks):
            k = kp_ref[0, 0, pl.ds(off + u * tq, tq), :]
            d = dq - (c + u) * tq
            for r in range(NSA_GROUP):
                s = _nt(k, qp_ref[0, r])
                if causal:
                    s = jnp.where(d >= 0, s, NEG)
                if far:
                    s = jnp.where(d <= window, s, NEG)
                st_ref[r, u * tq:(u + 1) * tq, :] = s
                out[r] = jnp.maximum(out[r], jnp.max(s, axis=0, keepdims=True))
        for r in range(NSA_GROUP):
            p = jnp.exp2(st_ref[r, 0:n * tq, :] - out[r]).astype(BF16)
            acc_ref[r] = jnp.exp2(m[r] - out[r]) * acc_ref[r] + _dot(vt, p)
        return tuple(out)

    def by_runs(lo, hi, mask, init):
        n = hi - lo
        carry = lax.fori_loop(0, n // FLASH_RUN, lambda j, m: run(lo + FLASH_RUN * j, (mask,) * FLASH_RUN, m), init)
        c2 = lo + (n // FLASH_RUN) * FLASH_RUN
        carry = lax.cond(n % FLASH_RUN >= 2, lambda m: run(c2, (mask,) * 2, m), lambda m: m, carry)
        return lax.cond(n % 2 == 1, lambda m: run(hi - 1, (mask,), m), lambda m: m, carry)

    acc_ref[...] = jnp.zeros(acc_ref.shape, F32)
    mx = tuple(jnp.full((1, tq), NEG, F32) for _ in range(NSA_GROUP))
    if window is None:
        by_runs(0, i + 1, (True, False), mx)
    else:
        back = window // tq
        steady = ((False, True),) + ((False, False),) * (back - 1) + ((True, False),)
        lax.cond(i >= back, lambda m: run(i - back, steady, m),
                 lambda m: by_runs(0, i + 1, (True, True), m), mx)
    for r in range(NSA_GROUP):
        acc = acc_ref[r]
        o_ref[0, r * HEAD_DIM:(r + 1) * HEAD_DIM, :] = (acc[0:HEAD_DIM] / acc[HEAD_DIM:HEAD_DIM + 1]).astype(o_ref.dtype)


def _flash(qp, k_slc, vt_slc, k_win, vt_win, window):
    b, _, t, _ = qp.shape
    nq = t // TQ
    assert window % TQ == 0 and window // TQ + 1 <= FLASH_RUN
    kspec = pl.BlockSpec((1, 1, t, LANES), lambda bi, g, i: (bi, g, 0, 0))
    vspec = pl.BlockSpec((1, 1, VT_ROWS, t), lambda bi, g, i: (bi, g, 0, 0))
    ospec = pl.BlockSpec((1, NSA_GROUP * HEAD_DIM, TQ), lambda bi, g, i: (bi, g, i))
    oshape = jax.ShapeDtypeStruct((b, NSA_HEADS * HEAD_DIM, t), BF16)
    return pl.pallas_call(
        functools.partial(_flash_body, window=window),
        grid=(b, NSA_KV_HEADS, nq),
        in_specs=[pl.BlockSpec((1, NSA_GROUP, TQ, LANES), lambda bi, g, i: (bi, g, i, 0)), kspec, vspec, kspec, vspec],
        out_specs=[ospec, ospec], out_shape=[oshape, oshape],
        scratch_shapes=[pltpu.VMEM((NSA_GROUP, FLASH_RUN * TQ, TQ), F32), pltpu.VMEM((NSA_GROUP, VT_ROWS, TQ), F32)],
        compiler_params=_cparams(("arbitrary", "arbitrary", "arbitrary")),
        name="flash",
    )(qp, k_slc, vt_slc, k_win, vt_win)


def _diag_blocks(o_full, o_ref, s=0):
    rows = o_ref.shape[1] // NSA_KV_HEADS
    for g in range(NSA_KV_HEADS):
        o_ref[s, g * rows:(g + 1) * rows, :] = o_full[g * rows:(g + 1) * rows, g * HEAD_DIM:(g + 1) * HEAD_DIM]


def _cmp_sample_body(pt_ref, *refs, n_pages, past_len, dec, sb):
    fs_refs = refs[:sb * n_pages]
    qt_ref, ov_ref, msum_ref, oc_ref, at_ref = refs[sb * n_pages:]
    nq = qt_ref.shape[1]
    nblk = n_pages * fs_refs[0].shape[0]
    nj = ov_ref.shape[0]
    pos_l = past_len + (lax.broadcasted_iota(jnp.int32, (nblk, nq), 1) // NSA_GROUP) % dec
    n_c = lax.broadcasted_iota(jnp.int32, (nblk, nq), 0)
    vis_t = jnp.logical_and(n_c * CMP_STRIDE + 2 * CMP_STRIDE - 1 <= pos_l, n_c < nblk - 1)
    jr = lax.broadcasted_iota(jnp.int32, (nj, nq), 0)
    pos = past_len + (lax.broadcasted_iota(jnp.int32, (nj, nq), 1) // NSA_GROUP) % dec
    n_sel = -(-(past_len + dec) // SEL_BLOCK)
    for s in range(sb):
        fs = jnp.concatenate([r[...] for r in fs_refs[s * n_pages:(s + 1) * n_pages]], axis=0)
        kc, vc = _finish_compress(fs)
        qt = qt_ref[s]
        p_t = _softmax_masked(_nt(kc, qt), vis_t, 0)
        _diag_blocks(_tn(p_t.astype(BF16), vc), oc_ref, s)
        a1 = _split_dot(ov_ref[...], p_t)
        hi = a1.astype(BF16)
        lo = (a1 - hi.astype(F32)).astype(BF16)
        sc = _dot(hi, msum_ref[...]) + _dot(lo, msum_ref[...])
        at_ref[s] = _select_mask(sc, jr, pos, n_sel)


def _cmp_sample(page_table, fs_phys, qt, ov_t, msum, past_len, dec):
    nb, n_pages = page_table.shape
    sb = CMP_SEQS
    nj = ov_t.shape[0]
    fs_specs =[pl.BlockSpec((SUBLANES, 1024), functools.partial(lambda b, pt, s, k: (pt[b * sb + s, k], 0), s=s, k=k))
                for s in range(sb) for k in range(n_pages)]
    grid_spec = pltpu.PrefetchScalarGridSpec(
        num_scalar_prefetch=1,
        grid=(nb // sb,),
        in_specs=fs_specs + [pl.BlockSpec((sb, LANES, 256), lambda b, pt: (b, 0, 0)),
                             pl.BlockSpec(ov_t.shape, lambda b, pt: (0, 0)),
                             pl.BlockSpec(msum.shape, lambda b, pt: (0, 0))],
        out_specs=[pl.BlockSpec((sb, 64, HEAD_DIM), lambda b, pt: (b, 0, 0)),
                   pl.BlockSpec((sb, nj, LANES), lambda b, pt: (b, 0, 0))],
    )
    return pl.pallas_call(
        functools.partial(_cmp_sample_body, n_pages=n_pages, past_len=past_len, dec=dec, sb=sb),
        grid_spec=grid_spec,
        out_shape=[jax.ShapeDtypeStruct((nb, 64, HEAD_DIM), F32),
                   jax.ShapeDtypeStruct((nb, nj, LANES), F32)],
        compiler_params=_cparams(("arbitrary",)),
        name="cmp_sample",
    )(page_table, *([fs_phys] * (sb * n_pages)), qt, ov_t, msum)


def _attn_sample_body(*refs, n_blocks, sb, dec, has_pt, has_bm, has_pm, emit_state):
    refs = list(refs[1:] if has_pt else refs)
    n_kv = sb * n_blocks if has_pt else 1
    kv_refs = refs[:n_kv]
    rest = refs[n_kv:]
    new_ref, qt_ref = rest[0], rest[1]
    rest = rest[2:]
    bm_ref = rest.pop(0) if has_bm else None
    pm_ref = rest.pop(0) if has_pm else None
    nm_ref, o_ref = rest[0], rest[1]
    so_ref = rest[2] if emit_state else None
    st_ref = rest[-1]
    nq = qt_ref.shape[2]
    rows = kv_refs[0].shape[3]
    per = rows // SEL_BLOCK

    def kv(s, kb, which):
        return kv_refs[s * n_blocks + kb][0, which] if has_pt else kv_refs[0][s, which]

    for s in range(sb):
        qbd = qt_ref[s]
        m = jnp.full((1, nq), NEG, F32)
        for kb in range(n_blocks):
            sc = _tn(kv(s, kb, 0).astype(BF16), qbd)
            if has_pm:
                sc = sc + pm_ref[kb * rows:(kb + 1) * rows, :]
            if has_bm:
                sc = sc + jnp.concatenate(
                    [jnp.broadcast_to(bm_ref[s, kb * per + u:kb * per + u + 1, :], (SEL_BLOCK, nq)) for u in range(per)],
                    axis=0)
            st_ref[s, kb * rows:(kb + 1) * rows, :] = sc
            m = jnp.maximum(m, jnp.max(sc, axis=0, keepdims=True))
        s_new = _dot(new_ref[s, :, 0:256].astype(BF16), qbd) + nm_ref[...]
        if has_bm:
            s_new = s_new + bm_ref[s, n_blocks * per:n_blocks * per + 1, :]
        m = jnp.maximum(m, jnp.max(s_new, axis=0, keepdims=True))
        e_new = jnp.exp2(s_new - m)
        l = jnp.sum(e_new, axis=0, keepdims=True)
        acc = _tn(new_ref[s, :, 256:512].astype(BF16), e_new.astype(BF16))
        for kb in range(n_blocks):
            e = jnp.exp2(st_ref[s, kb * rows:(kb + 1) * rows, :] - m)
            l = l + jnp.sum(e, axis=0, keepdims=True)
            acc = acc + _dot(kv(s, kb, 1).astype(BF16), e.astype(BF16))
        o_ref[s] = acc / jnp.maximum(l, 1e-30)
        if emit_state:
            lane = lax.broadcasted_iota(jnp.int32, (256, LANES), 1)
            tail = pltpu.roll(new_ref[s], SUBLANES - dec, axis=0)
            for which in range(2):
                shifted = pltpu.roll(kv(s, 0, which), rows - dec, axis=1)
                fill = jnp.concatenate([jnp.zeros((LANES - SUBLANES, 256), F32),
                                        tail[:, which * 256:(which + 1) * 256]], axis=0).T
                last = jnp.where(lane < LANES - dec, shifted[:, rows - LANES:], fill)
                so_ref[s, which, :, 0:rows - LANES] = shifted[:, 0:rows - LANES]
                so_ref[s, which, :, rows - LANES:rows] = last


def _attn_sample(kv, slot_blk, n_blocks, new, qt, new_mask, sb, dec,
                 page_table=None, blk_mask=None, past_mask=None, emit_state=False):
    nb = qt.shape[0]
    kv_rows = kv.shape[3]
    has_pt = page_table is not None
    has_bm = blk_mask is not None
    has_pm = past_mask is not None
    if has_pt:
        kv_specs = [pl.BlockSpec((1, 2, 256, kv_rows),
                                 functools.partial(lambda b, pt, s, k: (pt[b * sb + s, k], slot_blk, 0, 0), s=s, k=k))
                    for s in range(sb) for k in range(n_blocks)]
        im = lambda b, pt: (b, 0, 0)
        im4 = lambda b, pt: (b, 0, 0, 0)
        cm = lambda b, pt: (0, 0)
    else:
        kv_specs = [pl.BlockSpec((sb, 2, 256, kv_rows), lambda b: (b, slot_blk, 0, 0))]
        im = lambda b: (b, 0, 0)
        im4 = lambda b: (b, 0, 0, 0)
        cm = lambda b: (0, 0)
    in_specs = kv_specs + [pl.BlockSpec((sb, SUBLANES, 512), im), pl.BlockSpec((sb, 256, LANES), im)]
    args = [kv] * len(kv_specs) + [new, qt]
    if has_bm:
        in_specs.append(pl.BlockSpec((sb,) + blk_mask.shape[1:], im))
        args.append(blk_mask)
    if has_pm:
        in_specs.append(pl.BlockSpec(past_mask.shape, cm))
        args.append(past_mask)
    in_specs.append(pl.BlockSpec(new_mask.shape, cm))
    args.append(new_mask)
    out_specs = [pl.BlockSpec((sb, 256, LANES), im)]
    out_shape = [jax.ShapeDtypeStruct((nb, 256, LANES), F32)]
    if emit_state:
        out_specs.append(pl.BlockSpec((sb, 2, 256, kv_rows), im4))
        out_shape.append(jax.ShapeDtypeStruct((nb, 2, 256, kv_rows), F32))
    scratch = [pltpu.VMEM((sb, n_blocks * kv_rows, LANES), F32)]
    body = functools.partial(_attn_sample_body, n_blocks=n_blocks, sb=sb, dec=dec, has_pt=has_pt, has_bm=has_bm,
                             has_pm=has_pm, emit_state=emit_state)
    if has_pt:
        gs = pltpu.PrefetchScalarGridSpec(num_scalar_prefetch=1, grid=(nb // sb,), in_specs=in_specs,
                                          out_specs=out_specs, scratch_shapes=scratch)
        return pl.pallas_call(body, grid_spec=gs, out_shape=out_shape, compiler_params=_cparams(("arbitrary",)),
                              name="slc_sample")(page_table, *args)
    return pl.pallas_call(body, grid=(nb // sb,), in_specs=in_specs, out_specs=out_specs, scratch_shapes=scratch,
                          out_shape=out_shape, compiler_params=_cparams(("arbitrary",)), name="win_sample")(*args)


def _mix_project(o_refs, gate_ref, w_ref, tm, feature_major):
    n_in = len(o_refs)
    gated = gate_ref is not None
    if feature_major:
        gates_t = gate_ref[...].T
        heads = []
        for hd in range(NSA_HEADS):
            acc = jnp.zeros((HEAD_DIM, tm), F32)
            for kbr in range(n_in):
                row = gates_t[hd * 3 + kbr:hd * 3 + kbr + 1, :]
                acc = acc + row * o_refs[kbr][0, hd * HEAD_DIM:(hd + 1) * HEAD_DIM, :].astype(F32)
            heads.append(acc.astype(BF16))
        return _tn(jnp.concatenate(heads, axis=0), w_ref[...])
    if gated:
        lane = lax.broadcasted_iota(jnp.int32, (tm, LANES), 1)
        gates = gate_ref[...]
        chunks = []
        for c in range(1024 // LANES):
            acc = jnp.zeros((tm, LANES), F32)
            for kbr in range(n_in):
                c0 = (2 * c) * 3 + kbr
                c1 = (2 * c + 1) * 3 + kbr
                gexp = jnp.where(lane < HEAD_DIM, gates[:, c0:c0 + 1], gates[:, c1:c1 + 1])
                acc = acc + gexp * o_refs[kbr][:, c * LANES:(c + 1) * LANES].astype(F32)
            chunks.append(acc.astype(BF16))
        o = jnp.concatenate(chunks, axis=1)
    else:
        o = o_refs[0][...].astype(BF16)
    return _dot(o, w_ref[...])


def _ffn_body(*refs, sample, tiles_per_seq, nb, n_in, gated, feature_major):
    o_refs = refs[:n_in]
    refs = refs[n_in:]
    gate_ref = None
    if gated:
        gate_ref, refs = refs[0], refs[1:]
    wo_ref, g1_ref, refs = refs[0], refs[1], refs[2:]
    if sample:
        x_ref, g2_ref, g3_ref, wu_ref, cw_ref, cb_ref, wd_ref, buf_ref, y_ref, tail_ref = refs
    else:
        x_ref, g2_ref, g3_ref, wu_ref, cw_ref, cb_ref, wd_ref, y_ref, tail_ref, carry_ref = refs
    tm = x_ref.shape[0]
    x = x_ref[...] + _rms(_mix_project(o_refs, gate_ref, wo_ref, tm, feature_major), g1_ref[...])
    f = wd_ref.shape[0]
    h = _rms(x, g2_ref[...]).astype(BF16)
    acc = jnp.zeros(x.shape, F32)
    if not sample:
        first = pl.program_id(0) % tiles_per_seq == 0
        row = lax.broadcasted_iota(jnp.int32, (tm, FFN_FCHUNK), 0)
    for fc in range(f // FFN_FCHUNK):
        sl = slice(fc * FFN_FCHUNK, (fc + 1) * FFN_FCHUNK)
        gate = _dot(h, wu_ref[:, sl])
        val = _dot(h, wu_ref[:, f + fc * FFN_FCHUNK:f + (fc + 1) * FFN_FCHUNK])
        if sample:
            prev1 = jnp.concatenate([buf_ref[nb:2 * nb, sl], gate[0:tm - nb]], axis=0)
            prev2 = jnp.concatenate([buf_ref[0:2 * nb, sl], gate[0:tm - 2 * nb]], axis=0)
            tail_ref[:, sl] = gate[tm - 2 * nb:tm]
        else:
            carry = carry_ref[:, sl]
            c6 = jnp.where(first, 0.0, carry[6:7, :])
            c7 = jnp.where(first, 0.0, carry[7:8, :])
            prev1 = jnp.where(row == 0, c7, pltpu.roll(gate, 1, axis=0))
            prev2 = jnp.where(row == 0, c6, jnp.where(row == 1, c7, pltpu.roll(gate, 2, axis=0)))
            carry_ref[:, sl] = gate[tm - SUBLANES:tm]
            tail_ref[:, sl] = gate[tm - SUBLANES:tm]
        conv = cb_ref[0:1, sl] + cw_ref[0:1, sl] * prev2 + cw_ref[1:2, sl] * prev1 + cw_ref[2:3, sl] * gate
        act = (jax.nn.gelu(conv) * val).astype(BF16)
        acc = acc + _dot(act, wd_ref[sl, :])
    y_ref[...] = x + _rms(acc, g3_ref[...])


def _ffn(x, mix, g, wu, cw, cb, wd, seq_len=None, buf=None):
    m, d = x.shape
    f = wd.shape[0]
    os_, gates, w_o, feature_major = mix
    sample = buf is not None
    row = lambda i: (i, 0)
    if sample:
        tm = m
        nb = buf.shape[0] // 2
        tail_shape, tail_block = (2 * nb, f), (2 * nb, f)
        scratch = []
    else:
        tm = FFN_TM
        nb = 0
        tail_shape, tail_block = (m // tm * SUBLANES, f), (SUBLANES, f)
        scratch = [pltpu.VMEM((SUBLANES, f), F32)]
    if feature_major:
        nt = seq_len // tm
        in_specs = [pl.BlockSpec((1, 1024, tm), lambda i: (i // nt, 0, i % nt)) for _ in os_]
    else:
        in_specs = [pl.BlockSpec((tm, 1024), row) for _ in os_]
    args = list(os_)
    if gates is not None:
        in_specs.append(pl.BlockSpec((tm, LANES), row))
        args.append(gates)
    g1, g2, g3 = g[1:2], g[2:3], g[3:4]
    in_specs += [_resident(w_o.shape), _resident(g1.shape),
                 pl.BlockSpec((tm, d), row), _resident(g2.shape), _resident(g3.shape), _resident(wu.shape),
                 _resident(cw.shape), _resident(cb.shape), _resident(wd.shape)]
    args += [w_o, g1, x, g2, g3, wu, cw, cb, wd]
    if sample:
        in_specs.append(_resident(buf.shape))
        args.append(buf)
    return pl.pallas_call(
        functools.partial(_ffn_body, sample=sample, tiles_per_seq=(seq_len // tm if not sample else 1), nb=nb,
                          n_in=len(os_), gated=gates is not None, feature_major=feature_major),
        grid=(m // tm,), in_specs=in_specs,
        out_specs=[pl.BlockSpec((tm, d), row), pl.BlockSpec(tail_block, row)],
        out_shape=[jax.ShapeDtypeStruct((m, d), F32), jax.ShapeDtypeStruct(tail_shape, F32)],
        scratch_shapes=scratch,
        compiler_params=_cparams(("arbitrary",)), name="ffn_sample" if sample else "ffn_prompt",
    )(*args)


def _gla_proj_body(x_ref, g_ref, w_ref, wg_ref, bg_ref, q_ref, k_ref, v_ref, sr_ref, la_ref):
    h = _rms(x_ref[...], g_ref[...]).astype(BF16)
    nk = GLA_HEADS * GLA_DK
    nv = GLA_HEADS * GLA_DV
    for c in range(nk // 256):
        q_ref[:, c * 256:(c + 1) * 256] = _dot(h, w_ref[:, c * 256:(c + 1) * 256]) * (GLA_DK ** -0.5)
        k_ref[:, c * 256:(c + 1) * 256] = _dot(h, w_ref[:, nk + c * 256:nk + (c + 1) * 256])
    for c in range(nv // 256):
        v_ref[:, c * 256:(c + 1) * 256] = _dot(h, w_ref[:, 2 * nk + c * 256:2 * nk + (c + 1) * 256]).astype(BF16)
        r = _dot(h, w_ref[:, 2 * nk + nv + c * 256:2 * nk + nv + (c + 1) * 256])
        sr_ref[:, c * 256:(c + 1) * 256] = jax.nn.silu(r).astype(BF16)
    low = _dot(h, w_ref[:, 2 * nk + 2 * nv:2 * nk + 2 * nv + LANES]).astype(BF16)
    gz = _dot(low, wg_ref[...]) + bg_ref[...]
    log_sig = jnp.minimum(gz, 0.0) - jnp.log1p(jnp.exp(-jnp.abs(gz)))
    la_ref[...] = log_sig / GLA_GATE_TEMP


def _gla_proj(x, g, w, wg, bg):
    m, d = x.shape
    nk = GLA_HEADS * GLA_DK
    nv = GLA_HEADS * GLA_DV
    row = lambda i: (i, 0)
    return pl.pallas_call(
        _gla_proj_body, grid=(m // TM,),
        in_specs=[pl.BlockSpec((TM, d), row), _resident(g.shape), _resident(w.shape),
                  _resident(wg.shape), _resident(bg.shape)],
        out_specs=[pl.BlockSpec((TM, nk), row), pl.BlockSpec((TM, nk), row), pl.BlockSpec((TM, nv), row),
                   pl.BlockSpec((TM, nv), row), pl.BlockSpec((TM, nk), row)],
        out_shape=[jax.ShapeDtypeStruct((m, nk), F32), jax.ShapeDtypeStruct((m, nk), F32),
                   jax.ShapeDtypeStruct((m, nv), BF16), jax.ShapeDtypeStruct((m, nv), BF16),
                   jax.ShapeDtypeStruct((m, nk), F32)],
        compiler_params=_cparams(("arbitrary",)), name="gla_proj",
    )(x, g, w, wg, bg)


def _cumsum_rows(x):
    n = x.shape[0]
    row = lax.broadcasted_iota(jnp.int32, x.shape, 0)
    sh = 1
    while sh < n:
        x = x + jnp.where(row >= sh, pltpu.roll(x, sh, axis=0), 0.0)
        sh *= 2
    return x


def _gla_chunk(q, k, la, v, s_old):
    cs = q.shape[0]
    row = lax.broadcasted_iota(jnp.int32, (cs, 1), 0)
    trow = lax.broadcasted_iota(jnp.int32, (cs, cs), 0)
    scol = lax.broadcasted_iota(jnp.int32, (cs, cs), 1)
    r8 = lax.broadcasted_iota(jnp.int32, (SUBLANES, 1), 0)
    lane8 = lax.broadcasted_iota(jnp.int32, (SUBLANES, cs), 1)
    cum = _cumsum_rows(la * math.log2(math.e))
    last = cum[cs - 1:cs, :]
    out = _dot((q * jnp.exp2(cum)).astype(BF16), s_old.astype(BF16))
    att = None
    hh = cs // 2
    while hh >= SUBLANES:
        nblk = cs // (2 * hh)
        ref = jnp.concatenate(
            [jnp.broadcast_to(cum[u * 2 * hh + hh - 1:u * 2 * hh + hh, :], (2 * hh, GLA_DK)) for u in range(nblk)],
            axis=0) if nblk > 1 else jnp.broadcast_to(cum[hh - 1:hh, :], (cs, GLA_DK))
        second = (row % (2 * hh)) >= hh
        qh = jnp.where(second, q * jnp.exp2(cum - ref), 0.0).astype(BF16)
        kh = jnp.where(second, 0.0, k * jnp.exp2(ref - cum)).astype(BF16)
        a = _nt(qh, kh)
        if nblk > 1:
            a = jnp.where(trow // (2 * hh) == scol // (2 * hh), a, 0.0)
        att = a if att is None else att + a
        hh //= 2
    slabs = []
    for g8 in range(cs // SUBLANES):
        r0 = g8 * SUBLANES
        cg, qg, kg = cum[r0:r0 + SUBLANES], q[r0:r0 + SUBLANES], k[r0:r0 + SUBLANES]
        slab = jnp.zeros((SUBLANES, GLA_DV if att is None else cs), F32)
        for s in range(SUBLANES):
            e = jnp.exp2(cg - cg[s:s + 1, :])
            col = jnp.sum(e * qg * kg[s:s + 1, :], axis=1, keepdims=True)
            col = jnp.where(r8 >= s, col, 0.0)
            if att is None:
                slab = slab + col * v[r0 + s:r0 + s + 1, :].astype(F32)
            else:
                slab = jnp.where(lane8 == r0 + s, col, slab)
        slabs.append(slab)
    if att is None:
        out = out + slabs[0]
    else:
        att = att + jnp.concatenate(slabs, axis=0)
        out = out + _dot(att.astype(BF16), v)
    kt = (k * jnp.exp2(last - cum)).astype(BF16)
    dcol = jnp.broadcast_to(jnp.exp2(last), (SUBLANES, GLA_DK)).T[:, 0:1]
    return out, dcol * s_old + _tn(kt, v)


def _gla_rec_prompt_body(q_ref, k_ref, la_ref, v_ref, sr_ref, gn_ref, o_ref, so_ref, st_ref):
    c = pl.program_id(1)

    @pl.when(c == 0)
    def _():
        st_ref[...] = jnp.zeros(st_ref.shape, F32)

    for hd in range(GLA_HEADS):
        ksl = slice(hd * GLA_DK, (hd + 1) * GLA_DK)
        vsl = slice(hd * GLA_DV, (hd + 1) * GLA_DV)
        out, s_new = _gla_chunk(q_ref[:, ksl], k_ref[:, ksl], la_ref[:, ksl], v_ref[:, vsl], st_ref[hd])
        st_ref[hd] = s_new
        o_ref[:, vsl] = (_rms(out, gn_ref[...]) * sr_ref[:, vsl].astype(F32)).astype(o_ref.dtype)

    @pl.when(c == pl.num_programs(1) - 1)
    def _():
        so_ref[0] = st_ref[...]


def _gla_rec_sample_body(q_ref, k_ref, la_ref, v_ref, sr_ref, gn_ref, s0_ref, o_ref, so_ref, *, sb):
    for s in range(sb):
        rsl = slice(s * SUBLANES, (s + 1) * SUBLANES)
        for hd in range(GLA_HEADS):
            ksl = slice(hd * GLA_DK, (hd + 1) * GLA_DK)
            vsl = slice(hd * GLA_DV, (hd + 1) * GLA_DV)
            out, s_new = _gla_chunk(q_ref[rsl, ksl], k_ref[rsl, ksl], la_ref[rsl, ksl], v_ref[rsl, vsl], s0_ref[s, hd])
            so_ref[s, hd] = s_new
            o_ref[rsl, vsl] = (_rms(out, gn_ref[...]) * sr_ref[rsl, vsl].astype(F32)).astype(o_ref.dtype)


def _gla_rec(q, k, la, v, sr, gn, nb, chunk, s0=None):
    m = q.shape[0]
    nk = GLA_HEADS * GLA_DK
    nv = GLA_HEADS * GLA_DV
    out_shape = [jax.ShapeDtypeStruct((m, nv), BF16), jax.ShapeDtypeStruct((nb, GLA_HEADS, GLA_DK, GLA_DV), F32)]
    if s0 is None:
        nc = m // nb // chunk
        row = lambda b, c: (b * nc + c, 0)
        st_spec = pl.BlockSpec((1, GLA_HEADS, GLA_DK, GLA_DV), lambda b, c: (b, 0, 0, 0))
        return pl.pallas_call(
            _gla_rec_prompt_body, grid=(nb, nc),
            in_specs=[pl.BlockSpec((chunk, nk), row), pl.BlockSpec((chunk, nk), row), pl.BlockSpec((chunk, nk), row),
                      pl.BlockSpec((chunk, nv), row), pl.BlockSpec((chunk, nv), row),
                      pl.BlockSpec(gn.shape, lambda b, c: (0, 0))],
            out_specs=[pl.BlockSpec((chunk, nv), row), st_spec], out_shape=out_shape,
            scratch_shapes=[pltpu.VMEM((GLA_HEADS, GLA_DK, GLA_DV), F32)],
            compiler_params=_cparams(("arbitrary", "arbitrary")), name="gla_rec_prompt",
        )(q, k, la, v, sr, gn)
    sb = GLA_SEQS
    rows = sb * chunk
    row = lambda b: (b, 0)
    st_spec = pl.BlockSpec((sb, GLA_HEADS, GLA_DK, GLA_DV), lambda b: (b, 0, 0, 0))
    return pl.pallas_call(
        functools.partial(_gla_rec_sample_body, sb=sb), grid=(nb // sb,),
        in_specs=[pl.BlockSpec((rows, nk), row), pl.BlockSpec((rows, nk), row), pl.BlockSpec((rows, nk), row),
                  pl.BlockSpec((rows, nv), row), pl.BlockSpec((rows, nv), row),
                  pl.BlockSpec(gn.shape, lambda b: (0, 0)), st_spec],
        out_specs=[pl.BlockSpec((rows, nv), row), st_spec], out_shape=out_shape,
        compiler_params=_cparams(("arbitrary",)), name="gla_rec_sample",
    )(q, k, la, v, sr, gn, s0)


def _rope_tables(pos):
    half = HEAD_DIM // 2
    inv = ROPE_THETA ** (-jnp.arange(half, dtype=F32) / half)
    ang = pos.astype(F32)[:, None] * inv[None, :]
    cos, sin = jnp.cos(ang), jnp.sin(ang)
    z = jnp.zeros_like(sin)
    cos_t = jnp.tile(cos, (1, 4))
    sa = jnp.tile(jnp.concatenate([-sin, z], axis=1), (1, 2))
    sb = jnp.tile(jnp.concatenate([z, sin], axis=1), (1, 2))
    return cos_t, sa, sb


def _pad_cols(w, n):
    return jnp.pad(w, ((0, 0), (0, n - w.shape[1])))


def _cmp_weights(cmp_pe, cmp_w):
    g = NSA_KV_HEADS
    w = cmp_w.reshape(2, 2, CMP_STRIDE, HEAD_DIM, HEAD_DIM)
    eye = jnp.eye(g, dtype=F32)
    wbd = jnp.einsum('shjde,gk->sjgdhke', w, eye)
    wbd = wbd.reshape(2, CMP_STRIDE, g * HEAD_DIM, 2 * g * HEAD_DIM).astype(BF16)
    pe = cmp_pe.reshape(2, 2, CMP_STRIDE, 1, 1, HEAD_DIM)
    pe_t = jnp.broadcast_to(pe, (2, 2, CMP_STRIDE, SUBLANES, g, HEAD_DIM)).reshape(2, 2, CMP_STRIDE, SUBLANES, g * HEAD_DIM)
    return wbd, pe_t


def _overlap_t(n_blk, n_sel, rows, cols):
    i = jnp.arange(cols)[None, :]
    j = jnp.arange(rows)[:, None]
    ov = (i * CMP_STRIDE + 2 * CMP_STRIDE > j * SEL_BLOCK) & (i * CMP_STRIDE < (j + 1) * SEL_BLOCK)
    ov = ov & (i < n_blk) & (j < n_sel)
    return ov.astype(BF16)


def _nsa_layer_prompt(x, g, w_in, wbd, cbias, w_o, tabs, b, t):
    q, gates, rows_t, win_t, k_slc, vt_slc, k_win, vt_win = _nsa_proj_t(x, g[0:1], w_in, *tabs, b, t)
    fs = _compress_pages(rows_t.reshape(b, 4, NSA_KV_HEADS * HEAD_DIM, t), wbd, cbias)
    n_blk = t // CMP_STRIDE - 1
    n_sel = -(-t // SEL_BLOCK)
    oc, qp = _cmp_prompt(q, fs, _overlap_t(n_blk, n_sel, n_sel, t // CMP_STRIDE), b, t)
    o_s, o_w = _flash(qp, k_slc, vt_slc, k_win, vt_win, WINDOW)
    return ([oc, o_s, o_w], gates, w_o, True), rows_t, win_t


def _to_rows(o, nb, dec):
    o = o.reshape(nb, NSA_KV_HEADS, dec, NSA_GROUP, HEAD_DIM).transpose(2, 0, 1, 3, 4)
    return o.reshape(dec * nb, NSA_HEADS * HEAD_DIM).astype(BF16)


def _nsa_layer_sample(x, g, w_in, wbd, cbias, w_o, tabs, cache, page_table, win_state, nb, dec):
    n_phys, page = cache.shape[0], cache.shape[1]
    n_pages = page_table.shape[1]
    past_len = n_pages * page
    q, rows, win, gates = _nsa_proj(x, g[0:1], w_in, *tabs)
    pages_t = cache.transpose(0, 2, 3, 4, 1).reshape(n_phys, 4, NSA_KV_HEADS * HEAD_DIM, page)
    wb = win_state.shape[1]
    win_t = win_state.transpose(0, 2, 3, 4, 1).reshape(nb, 2, NSA_KV_HEADS * HEAD_DIM, wb)
    fs_phys = _compress_pages(pages_t, wbd, cbias)
    q5 = q.reshape(dec, nb, NSA_KV_HEADS, NSA_GROUP, HEAD_DIM).transpose(1, 2, 0, 3, 4)
    eye = jnp.eye(NSA_KV_HEADS, dtype=BF16)
    qt = q5.reshape(nb, NSA_KV_HEADS, dec * NSA_GROUP, 1, HEAD_DIM) * eye[None, :, None, :, None]
    qt = qt.reshape(nb, NSA_KV_HEADS * dec * NSA_GROUP, NSA_KV_HEADS * HEAD_DIM)
    qt = jnp.pad(qt, ((0, 0), (0, LANES - qt.shape[1]), (0, 0)))
    n_blk = (past_len + dec) // CMP_STRIDE - 1
    n_sel = -(-(past_len + dec) // SEL_BLOCK)
    nj = -(-n_sel // SUBLANES) * SUBLANES
    lane = jnp.arange(LANES)
    msum = (lane[:, None] // NSA_GROUP == lane[None, :] // NSA_GROUP).astype(BF16)
    oc, a_t = _cmp_sample(page_table, fs_phys, qt, _overlap_t(n_blk, n_sel, nj, past_len // CMP_STRIDE), msum, past_len, dec)
    t_q = ((lane // NSA_GROUP) % dec)[None, :]
    j8 = jnp.arange(SUBLANES)[:, None]
    new_mask = jnp.where((j8 < dec) & (j8 <= t_q), 0.0, NEG).astype(F32)
    win_mask = jnp.where(jnp.arange(wb)[:, None] >= t_q + wb - WINDOW, 0.0, NEG).astype(F32)
    qbd = qt.transpose(0, 2, 1)

    def new_rows(a):
        a = a.reshape(dec, nb, 512).transpose(1, 0, 2)
        return jnp.pad(a, ((0, 0), (0, SUBLANES - dec), (0, 0)))

    def from_t(o):
        o = o[:, :, :NSA_KV_HEADS * dec * NSA_GROUP].reshape(nb, NSA_KV_HEADS, HEAD_DIM, NSA_KV_HEADS, dec, NSA_GROUP)
        o = jnp.diagonal(o, axis1=1, axis2=3)
        return o.transpose(2, 0, 4, 3, 1).reshape(dec * nb, NSA_HEADS * HEAD_DIM).astype(BF16)

    (o_s,) = _attn_sample(pages_t, 1, n_pages, new_rows(rows[:, 512:1024]), qbd, new_mask, SLC_SEQS, dec,
                          page_table=page_table, blk_mask=a_t)
    o_w, win_next = _attn_sample(win_t, 0, 1, new_rows(win), qbd, new_mask, WIN_SEQS, dec,
                                 past_mask=win_mask, emit_state=True)
    mix = ([_to_rows(oc, nb, dec), from_t(o_s), from_t(o_w)], gates, w_o, False)
    win_next = win_next.reshape(nb, 2, NSA_KV_HEADS, HEAD_DIM, wb).transpose(0, 4, 1, 2, 3)
    return mix, rows, win_next


def _gla_layer(x, g, w_in, wg, bg, gn, w_o, nb, seq, s0=None):
    q, k, v, sr, la = _gla_proj(x, g[0:1], w_in, wg, bg)
    if s0 is None:
        o, s = _gla_rec(q, k, la, v, sr, gn, nb, GLA_CHUNK)
    else:
        def seqs(a):
            a = a.reshape(seq, nb, a.shape[1]).transpose(1, 0, 2)
            return jnp.pad(a, ((0, 0), (0, SUBLANES - seq), (0, 0))).reshape(nb * SUBLANES, a.shape[2])
        o, s = _gla_rec(seqs(q), seqs(k), seqs(la), seqs(v), seqs(sr), gn, nb, SUBLANES, s0=s0)
        o = o.reshape(nb, SUBLANES, o.shape[1])[:, :seq].transpose(1, 0, 2).reshape(seq * nb, o.shape[1])
    return ([o], None, w_o, False), s


def kernel(x_prompt, x_sample, cache_nsa_kv, state_win_kv, state_gla, state_ffn_conv, page_table, norm_gain, nsa_w_in, nsa_cmp_pe, nsa_cmp_w, nsa_w_o, gla_w_in, gla_w_gate_up, gla_b_gate, gla_norm_gain, gla_w_o, ffn_w_up, ffn_conv_w, ffn_conv_b, ffn_w_down):
    b, t, d = x_prompt.shape
    nb, dec, _ = x_sample.shape
    depth = norm_gain.shape[0]
    f = ffn_w_down.shape[1]
    past_len = page_table.shape[1] * cache_nsa_kv.shape[2]

    xp = x_prompt.reshape(b * t, d)
    xs = x_sample.transpose(1, 0, 2).reshape(dec * nb, d)
    tabs_p = _rope_tables(jnp.arange(t, dtype=jnp.int32))
    tabs_s = _rope_tables(past_len + jnp.repeat(jnp.arange(dec, dtype=jnp.int32), nb))

    nsa_p, nsa_s, win_p, win_s, gla_p, gla_s, ffn_p, ffn_s = [], [], [], [], [], [], [], []
    for i in range(depth):
        g = norm_gain[i]
        a = i // 2
        if i % 2 == 0:
            w_in = _pad_cols(nsa_w_in[a], 2688).astype(BF16)
            w_o = nsa_w_o[a].astype(BF16)
            wbd, pe_t = _cmp_weights(nsa_cmp_pe[a], nsa_cmp_w[a])
            cbias = _cmp_bias(pe_t, wbd)
            mix_p, rows_t, win_t = _nsa_layer_prompt(xp, g, w_in, wbd, cbias, w_o, tabs_p, b, t)
            nsa_p.append(rows_t.reshape(b, 4, NSA_KV_HEADS, HEAD_DIM, t).transpose(0, 4, 1, 2, 3))
            nw = min(WINDOW, t)
            win_p.append(win_t.reshape(b, 2, NSA_KV_HEADS, HEAD_DIM, t)[..., t - nw:].transpose(0, 4, 1, 2, 3))
            mix_s, rows, win_next = _nsa_layer_sample(xs, g, w_in, wbd, cbias, w_o, tabs_s, cache_nsa_kv[a], page_table,
                                                      state_win_kv[a], nb, dec)
            nsa_s.append(rows.reshape(dec, nb, 4, NSA_KV_HEADS, HEAD_DIM).transpose(1, 0, 2, 3, 4))
            win_s.append(win_next)
        else:
            w_in = _pad_cols(gla_w_in[a], 3200).astype(BF16)
            wg = jnp.pad(gla_w_gate_up[a], ((0, LANES - GLA_GATE_RANK), (0, 0))).astype(BF16)
            bg = gla_b_gate[a][None, :]
            gn = gla_norm_gain[a][None, :]
            w_o = gla_w_o[a].astype(BF16)
            mix_p, s = _gla_layer(xp, g, w_in, wg, bg, gn, w_o, b, t)
            gla_p.append(s)
            mix_s, s = _gla_layer(xs, g, w_in, wg, bg, gn, w_o, nb, dec, s0=state_gla[a])
            gla_s.append(s)
        wu = ffn_w_up[i].astype(BF16)
        wd = ffn_w_down[i].astype(BF16)
        cw = jnp.pad(ffn_conv_w[i], ((0, SUBLANES - ffn_conv_w.shape[1]), (0, 0)))
        cb = ffn_conv_b[i][None, :]
        xp, tail = _ffn(xp, mix_p, g, wu, cw, cb, wd, seq_len=t)
        tail = tail.reshape(b, t // FFN_TM, SUBLANES, f)[:, -1, SUBLANES - 2:]
        ffn_p.append(tail)
        buf = state_ffn_conv[i].transpose(1, 0, 2).reshape(2 * nb, f)
        xs, tail = _ffn(xs, mix_s, g, wu, cw, cb, wd, buf=buf)
        ffn_s.append(tail.reshape(2, nb, f).transpose(1, 0, 2))

    y_prompt = xp.reshape(b, t, d)
    y_sample = xs.reshape(dec, nb, d).transpose(1, 0, 2)
    return (y_prompt, y_sample, jnp.stack(nsa_p), jnp.stack(nsa_s), jnp.stack(win_p), jnp.stack(win_s),
            jnp.stack(gla_p), jnp.stack(gla_s), jnp.stack(ffn_p), jnp.stack(ffn_s))
```

```python
import functools
import math

import jax
import jax.numpy as jnp
from jax import lax
from jax.experimental import pallas as pl
from jax.experimental.pallas import tpu as pltpu

F32, BF16 = jnp.float32, jnp.bfloat16

HEAD_DIM = 64
NSA_HEADS = 16
NSA_KV_HEADS = 4
NSA_GROUP = 4
CMP_STRIDE = 16
SEL_BLOCK = 64
TOP_N = 16
WINDOW = 512
ROPE_THETA = 10000.0
GLA_HEADS = 4
GLA_DK = 128
GLA_DV = 256
GLA_GATE_RANK = 16
GLA_GATE_TEMP = 16.0
NORM_EPS = 1e-6
FORCE = 1e6
NEG = -1e30

LANES = 128
SUBLANES = 8
VMEM_LIMIT = 48 * 1024 * 1024

TM = 512
TQ = 256
VT_ROWS = 80
CMP_ROWS = 256
GLA_CHUNK = 128
SLC_SEQS = 4
WIN_SEQS = 4
CMP_SEQS = 8
GLA_SEQS = 8
FFN_FCHUNK = 2816
FFN_TM = 512


def _cparams(sem):
    return pltpu.CompilerParams(dimension_semantics=sem, vmem_limit_bytes=VMEM_LIMIT)


def _resident(shape):
    nd = len(shape)
    return pl.BlockSpec(shape, lambda *_: (0,) * nd, pipeline_mode=pl.Buffered(1))


def _rms(x, g):
    return x * lax.rsqrt(jnp.mean(x * x, axis=-1, keepdims=True) + NORM_EPS) * g


def _nt(a, b):
    return lax.dot_general(a, b, (((1,), (1,)), ((), ())), preferred_element_type=F32)


def _tn(a, b):
    return lax.dot_general(a, b, (((0,), (0,)), ((), ())), preferred_element_type=F32)


def _dot(a, b):
    return jnp.dot(a, b, preferred_element_type=F32)


def _split_dot(w, x):
    hi = x.astype(BF16)
    lo = (x - hi.astype(F32)).astype(BF16)
    return _dot(w, hi) + _dot(w, lo)


def _softmax_masked(s, mask, axis):
    s = jnp.where(mask, s, NEG)
    m = jnp.max(s, axis=axis, keepdims=True)
    e = jnp.where(mask, jnp.exp2(s - m), 0.0)
    return e / jnp.maximum(jnp.sum(e, axis=axis, keepdims=True), 1e-30)


Q_SCALE = HEAD_DIM ** -0.5 * math.log2(math.e)


def _nsa_proj_parts(x_ref, g_ref, w_ref, cos_ref, sa_ref, sb_ref, q_ref, gate_ref):
    h = _rms(x_ref[...], g_ref[...]).astype(BF16)
    cos, sa, sb = cos_ref[...], sa_ref[...], sb_ref[...]

    def rope(z):
        return z * cos + pltpu.roll(z, 96, axis=1) * sa + pltpu.roll(z, 32, axis=1) * sb

    def proj(c0):
        return _dot(h, w_ref[:, c0:c0 + 256])

    def rope256(z):
        return jnp.concatenate([rope(z[:, :LANES]), rope(z[:, LANES:])], axis=1)

    for c in range(4):
        q_ref[:, c * 256:(c + 1) * 256] = (rope256(proj(c * 256)) * Q_SCALE).astype(BF16)
    gate_ref[...] = jax.nn.sigmoid(_dot(h, w_ref[:, 2560:2688]))
    return (rope256(proj(1024)), proj(1280), rope256(proj(1536)), proj(1792), rope256(proj(2048)), proj(2304))


def _nsa_proj_body(x_ref, g_ref, w_ref, cos_ref, sa_ref, sb_ref, q_ref, rows_ref, win_ref, gate_ref):
    parts = _nsa_proj_parts(x_ref, g_ref, w_ref, cos_ref, sa_ref, sb_ref, q_ref, gate_ref)
    for c in range(4):
        rows_ref[:, c * 256:(c + 1) * 256] = parts[c]
    win_ref[:, 0:256] = parts[4]
    win_ref[:, 256:512] = parts[5]


def _nsa_proj_t_body(x_ref, g_ref, w_ref, cos_ref, sa_ref, sb_ref, q_ref, gate_ref,
                     rows_t_ref, win_t_ref, ks_ref, vs_ref, kw_ref, vw_ref, *, tiles_per_seq, n_sel):
    kc, vc, ks, vs, kw, vw = _nsa_proj_parts(x_ref, g_ref, w_ref, cos_ref, sa_ref, sb_ref, q_ref, gate_ref)
    tm = kc.shape[0]
    for c, part in enumerate((kc, vc, ks, vs)):
        rows_t_ref[0, c * 256:(c + 1) * 256, :] = part.T
    win_t_ref[0, 0:256, :] = kw.T
    win_t_ref[0, 256:512, :] = vw.T
    ones = jnp.ones((NSA_KV_HEADS, VT_ROWS - HEAD_DIM, tm), BF16)
    vs_ref[0] = jnp.concatenate([vs.T.reshape(NSA_KV_HEADS, HEAD_DIM, tm).astype(BF16), ones], axis=1)
    vw_ref[0] = jnp.concatenate([vw.T.reshape(NSA_KV_HEADS, HEAD_DIM, tm).astype(BF16), ones], axis=1)
    t = (pl.program_id(0) % tiles_per_seq) * tm + lax.broadcasted_iota(jnp.int32, (tm, HEAD_DIM), 0)
    lane = lax.broadcasted_iota(jnp.int32, (tm, HEAD_DIM), 1)
    onehot = jnp.where(jnp.where(lane < n_sel, t // SEL_BLOCK, -1) == lane, 1.0, 0.0).astype(BF16)
    zeros = jnp.zeros((tm, HEAD_DIM), BF16)
    for g in range(NSA_KV_HEADS):
        sl = slice(g * HEAD_DIM, (g + 1) * HEAD_DIM)
        ks_ref[0, g] = jnp.concatenate([ks[:, sl].astype(BF16), onehot], axis=1)
        kw_ref[0, g] = jnp.concatenate([kw[:, sl].astype(BF16), zeros], axis=1)


def _nsa_proj_t(x, g, w, cos, sa, sb, b, t):
    m, d = x.shape
    nt = t // TM
    n_sel = -(-t // SEL_BLOCK)
    row = lambda i: (i, 0)
    tab = lambda i: (i % nt, 0)
    fm = lambda i: (i // nt, 0, i % nt)
    kmap = lambda i: (i // nt, 0, i % nt, 0)
    vmap = lambda i: (i // nt, 0, 0, i % nt)
    kv = NSA_KV_HEADS
    return pl.pallas_call(
        functools.partial(_nsa_proj_t_body, tiles_per_seq=nt, n_sel=n_sel),
        grid=(m // TM,),
        in_specs=[pl.BlockSpec((TM, d), row), _resident(g.shape), _resident(w.shape),
                  pl.BlockSpec((TM, LANES), tab), pl.BlockSpec((TM, LANES), tab), pl.BlockSpec((TM, LANES), tab)],
        out_specs=[pl.BlockSpec((TM, 1024), row), pl.BlockSpec((TM, LANES), row),
                   pl.BlockSpec((1, 1024, TM), fm), pl.BlockSpec((1, 512, TM), fm),
                   pl.BlockSpec((1, kv, TM, LANES), kmap), pl.BlockSpec((1, kv, VT_ROWS, TM), vmap),
                   pl.BlockSpec((1, kv, TM, LANES), kmap), pl.BlockSpec((1, kv, VT_ROWS, TM), vmap)],
        out_shape=[jax.ShapeDtypeStruct((m, 1024), BF16), jax.ShapeDtypeStruct((m, LANES), F32),
                   jax.ShapeDtypeStruct((b, 1024, t), F32), jax.ShapeDtypeStruct((b, 512, t), F32),
                   jax.ShapeDtypeStruct((b, kv, t, LANES), BF16), jax.ShapeDtypeStruct((b, kv, VT_ROWS, t), BF16),
                   jax.ShapeDtypeStruct((b, kv, t, LANES), BF16), jax.ShapeDtypeStruct((b, kv, VT_ROWS, t), BF16)],
        compiler_params=_cparams(("arbitrary",)),
        name="nsa_proj_t",
    )(x, g, w, cos, sa, sb)


def _nsa_proj(x, g, w, cos, sa, sb):
    m, d = x.shape
    ntab = cos.shape[0] // TM
    row = lambda i: (i, 0)
    tab = lambda i: (i % ntab, 0)
    return pl.pallas_call(
        _nsa_proj_body,
        grid=(m // TM,),
        in_specs=[pl.BlockSpec((TM, d), row), _resident(g.shape), _resident(w.shape),
                  pl.BlockSpec((TM, LANES), tab), pl.BlockSpec((TM, LANES), tab), pl.BlockSpec((TM, LANES), tab)],
        out_specs=[pl.BlockSpec((TM, 1024), row), pl.BlockSpec((TM, 1024), row),
                   pl.BlockSpec((TM, 512), row), pl.BlockSpec((TM, LANES), row)],
        out_shape=[jax.ShapeDtypeStruct((m, 1024), BF16), jax.ShapeDtypeStruct((m, 1024), F32),
                   jax.ShapeDtypeStruct((m, 512), F32), jax.ShapeDtypeStruct((m, LANES), F32)],
        compiler_params=_cparams(("arbitrary",)),
        name="nsa_proj",
    )(x, g, w, cos, sa, sb)


def _cmp_bias_body(pe_ref, w_ref, o_ref):
    for s in range(2):
        for half in range(2):
            acc = jnp.zeros((SUBLANES, 256), F32)
            for j in range(CMP_STRIDE):
                acc = acc + _dot(pe_ref[s, half, j].astype(BF16), w_ref[s, j, :, half * 256:(half + 1) * 256])
            o_ref[:, s * 512 + half * 256:s * 512 + (half + 1) * 256] = acc


def _cmp_bias(pe_t, wbd):
    return pl.pallas_call(_cmp_bias_body, out_shape=jax.ShapeDtypeStruct((SUBLANES, 1024), F32),
                          compiler_params=_cparams(None), name="cmp_bias")(pe_t, wbd)


CMP_HALF = CMP_ROWS // 2
CMP_PITCH = CMP_HALF + SUBLANES


def _compress_pages_body(x_ref, w_ref, b_ref, o_ref, xa_ref, xb_ref):
    per_page = LANES // CMP_STRIDE
    pages_per_block = x_ref.shape[3] // LANES
    pages_half = CMP_HALF // per_page

    def move(xs_ref, p0):
        for q in range(pages_half):
            blk, off = divmod(p0 + q, pages_per_block)
            for s in range(2):
                xt = x_ref[blk, s, :, off * LANES:(off + 1) * LANES].T
                for c in range(per_page):
                    for half in range(2):
                        xs_ref[2 * s + half, pl.ds(q * per_page + c, CMP_STRIDE, stride=CMP_PITCH), :] = (
                            xt[c * CMP_STRIDE:(c + 1) * CMP_STRIDE, half * LANES:(half + 1) * LANES])

    def project(xs_ref, r0):
        for s in range(2):
            acc = jnp.zeros((CMP_HALF, 512), F32)
            for j in range(CMP_STRIDE):
                xj = jnp.concatenate([xs_ref[2 * s, j * CMP_PITCH:j * CMP_PITCH + CMP_HALF, :],
                                      xs_ref[2 * s + 1, j * CMP_PITCH:j * CMP_PITCH + CMP_HALF, :]], axis=1)
                acc = acc + _dot(xj.astype(BF16), w_ref[s, j])
            o_ref[r0:r0 + CMP_HALF, s * 512:(s + 1) * 512] = acc + b_ref[0:1, s * 512:(s + 1) * 512]

    move(xa_ref, 0)
    move(xb_ref, pages_half)
    project(xa_ref, 0)
    project(xb_ref, CMP_HALF)


def _compress_pages(pages_t, wbd, bias):
    n, _, _, npos = pages_t.shape
    chunks = npos // CMP_STRIDE
    pp = CMP_ROWS // chunks
    return pl.pallas_call(
        _compress_pages_body,
        grid=(n // pp,),
        in_specs=[pl.BlockSpec((pp, 2, 256, npos), lambda i: (i, 0, 0, 0)),
                  _resident(wbd.shape), _resident(bias.shape)],
        out_specs=pl.BlockSpec((CMP_ROWS, 1024), lambda i: (i, 0)),
        out_shape=jax.ShapeDtypeStruct((n * chunks, 1024), F32),
        scratch_shapes=[pltpu.VMEM((4, CMP_STRIDE * CMP_PITCH, LANES), F32),
                        pltpu.VMEM((4, CMP_STRIDE * CMP_PITCH, LANES), F32)],
        compiler_params=_cparams(("arbitrary",)),
        name="compress_pages",
    )(pages_t, wbd, bias)


def _finish_compress(fs):
    n = fs.shape[0]
    kc = fs[:, 0:256] + pltpu.roll(fs[:, 256:512], n - 1, axis=0)
    vc = fs[:, 512:768] + pltpu.roll(fs[:, 768:1024], n - 1, axis=0)
    return kc.astype(BF16), vc.astype(BF16)


def _select_mask(sc_raw, jr, pos, n_real):
    sc = jnp.where(jr * SEL_BLOCK > pos, -FORCE, sc_raw)
    sc = jnp.where(jr == pos // SEL_BLOCK, FORCE, jnp.where(jr == 0, FORCE, sc))
    sc = jnp.where(jr >= n_real, -2.0 * FORCE, sc)
    cnt = jnp.zeros(sc.shape, F32)
    for i in range(n_real):
        ri = sc[i:i + 1, :]
        ge = jnp.where(ri >= sc, 1.0, 0.0)
        gt = jnp.where(ri > sc, 1.0, 0.0)
        cnt = cnt + jnp.where(jr > i, ge, gt)
    keep = jnp.where(sc > -0.5 * FORCE, 0.0, NEG)
    return jnp.where(cnt < float(min(TOP_N, n_real)), keep, NEG)


def _cmp_prompt_body(q_ref, fs_ref, ov_ref, oc_ref, qp_ref):
    i = pl.program_id(1)
    tq = q_ref.shape[0]
    nblk = fs_ref.shape[0]
    fs = fs_ref[...]
    kc = (fs[:, 0:256] + pltpu.roll(fs[:, 256:512], nblk - 1, axis=0)).astype(BF16)
    vc_t = (fs[:, 512:768] + pltpu.roll(fs[:, 768:1024], nblk - 1, axis=0)).T.astype(BF16)
    t_row = i * tq + lax.broadcasted_iota(jnp.int32, (nblk, tq), 1)
    n_col = lax.broadcasted_iota(jnp.int32, (nblk, tq), 0)
    vis_t = n_col * CMP_STRIDE + 2 * CMP_STRIDE - 1 <= t_row
    nsel = ov_ref.shape[0]
    jr = lax.broadcasted_iota(jnp.int32, (nsel, tq), 0)
    pos = i * tq + lax.broadcasted_iota(jnp.int32, (nsel, tq), 1)
    eye = jnp.where(lax.broadcasted_iota(jnp.int32, (tq, tq), 0) == lax.broadcasted_iota(jnp.int32, (tq, tq), 1),
                    1.0, 0.0).astype(BF16)
    for g in range(NSA_KV_HEADS):
        kg = kc[:, g * HEAD_DIM:(g + 1) * HEAD_DIM]
        vg_t = vc_t[g * HEAD_DIM:(g + 1) * HEAD_DIM, :]
        ps_t = jnp.zeros((nblk, tq), F32)
        qs = []
        for r in range(NSA_GROUP):
            hd = g * NSA_GROUP + r
            qh = q_ref[:, hd * HEAD_DIM:(hd + 1) * HEAD_DIM]
            qs.append(qh)
            p_t = _softmax_masked(_nt(kg, qh), vis_t, 0)
            oc_ref[0, hd * HEAD_DIM:(hd + 1) * HEAD_DIM, :] = _dot(vg_t, p_t.astype(BF16)).astype(oc_ref.dtype)
            ps_t = ps_t + p_t
        a_t = _select_mask(_split_dot(ov_ref[...], ps_t), jr, pos, nsel)
        a_t = jnp.concatenate([a_t, jnp.zeros((HEAD_DIM - nsel, tq), F32)], axis=0).astype(BF16)
        a = _nt(eye, a_t).astype(BF16)
        for r in range(NSA_GROUP):
            qp_ref[0, g * NSA_GROUP + r] = jnp.concatenate([qs[r], a], axis=1)


def _cmp_prompt(q, fs, ov_t, b, t):
    nq = t // TQ
    nblk = t // CMP_STRIDE
    return pl.pallas_call(
        _cmp_prompt_body,
        grid=(b, nq),
        in_specs=[pl.BlockSpec((TQ, 1024), lambda bi, i: (bi * nq + i, 0)),
                  pl.BlockSpec((nblk, 1024), lambda bi, i: (bi, 0)),
                  _resident(ov_t.shape)],
        out_specs=[pl.BlockSpec((1, 1024, TQ), lambda bi, i: (bi, 0, i)),
                   pl.BlockSpec((1, NSA_HEADS, TQ, LANES), lambda bi, i: (bi, 0, i, 0))],
        out_shape=[jax.ShapeDtypeStruct((b, 1024, t), BF16),
                   jax.ShapeDtypeStruct((b, NSA_HEADS, t, LANES), BF16)],
        compiler_params=_cparams(("arbitrary", "arbitrary")),
        name="cmp_prompt",
    )(q, fs, ov_t)


FLASH_RUN = 4


def _flash_body(qp_ref, ks_ref, vs_ref, kw_ref, vw_ref, os_ref, ow_ref, st_ref, acc_ref, *, window):
    _flash_branch(qp_ref, ks_ref, vs_ref, os_ref, st_ref, acc_ref, window=None)
    _flash_branch(qp_ref, kw_ref, vw_ref, ow_ref, st_ref, acc_ref, window=window)


def _flash_branch(qp_ref, kp_ref, vt_ref, o_ref, st_ref, acc_ref, *, window):
    i = pl.program_id(2)
    tq = qp_ref.shape[2]
    dq = (i * tq + lax.broadcasted_iota(jnp.int32, (tq, tq), 1)) - lax.broadcasted_iota(jnp.int32, (tq, tq), 0)

    def run(c, masks, m):
        n = len(masks)
        off = pl.multiple_of(c * tq, tq)
        vt = vt_ref[0, 0, :, pl.ds(off, n * tq)]
        out = list(m)
        for u, (causal, far) in enumerate(masks):
            k = kp_ref[0, 0, pl.ds(off + u * tq, tq), :]
            d = dq - (c + u) * tq
            for r in range(NSA_GROUP):
                s = _nt(k, qp_ref[0, r])
                if causal:
                    s = jnp.where(d >= 0, s, NEG)
                if far:
                    s = jnp.where(d <= window, s, NEG)
                st_ref[r, u * tq:(u + 1) * tq, :] = s
                out[r] = jnp.maximum(out[r], jnp.max(s, axis=0, keepdims=True))
        for r in range(NSA_GROUP):
            p = jnp.exp2(st_ref[r, 0:n * tq, :] - out[r]).astype(BF16)
            acc_ref[r] = jnp.exp2(m[r] - out[r]) * acc_ref[r] + _dot(vt, p)
        return tuple(out)

    def by_runs(lo, hi, mask, init):
        n = hi - lo
        carry = lax.fori_loop(0, n // FLASH_RUN, lambda j, m: run(lo + FLASH_RUN * j, (mask,) * FLASH_RUN, m), init)
        c2 = lo + (n // FLASH_RUN) * FLASH_RUN
        carry = lax.cond(n % FLASH_RUN >= 2, lambda m: run(c2, (mask,) * 2, m), lambda m: m, carry)
        return lax.cond(n % 2 == 1, lambda m: run(hi - 1, (mask,), m), lambda m: m, carry)

    acc_ref[...] = jnp.zeros(acc_ref.shape, F32)
    mx = tuple(jnp.full((1, tq), NEG, F32) for _ in range(NSA_GROUP))
    if window is None:
        by_runs(0, i + 1, (True, False), mx)
    else:
        back = window // tq
        steady = ((False, True),) + ((False, False),) * (back - 1) + ((True, False),)
        lax.cond(i >= back, lambda m: run(i - back, steady, m),
                 lambda m: by_runs(0, i + 1, (True, True), m), mx)
    for r in range(NSA_GROUP):
        acc = acc_ref[r]
        o_ref[0, r * HEAD_DIM:(r + 1) * HEAD_DIM, :] = (acc[0:HEAD_DIM] / acc[HEAD_DIM:HEAD_DIM + 1]).astype(o_ref.dtype)


def _flash(qp, k_slc, vt_slc, k_win, vt_win, window):
    b, _, t, _ = qp.shape
    nq = t // TQ
    assert window % TQ == 0 and window // TQ + 1 <= FLASH_RUN
    kspec = pl.BlockSpec((1, 1, t, LANES), lambda bi, g, i: (bi, g, 0, 0))
    vspec = pl.BlockSpec((1, 1, VT_ROWS, t), lambda bi, g, i: (bi, g, 0, 0))
    ospec = pl.BlockSpec((1, NSA_GROUP * HEAD_DIM, TQ), lambda bi, g, i: (bi, g, i))
    oshape = jax.ShapeDtypeStruct((b, NSA_HEADS * HEAD_DIM, t), BF16)
    return pl.pallas_call(
        functools.partial(_flash_body, window=window),
        grid=(b, NSA_KV_HEADS, nq),
        in_specs=[pl.BlockSpec((1, NSA_GROUP, TQ, LANES), lambda bi, g, i: (bi, g, i, 0)), kspec, vspec, kspec, vspec],
        out_specs=[ospec, ospec], out_shape=[oshape, oshape],
        scratch_shapes=[pltpu.VMEM((NSA_GROUP, FLASH_RUN * TQ, TQ), F32), pltpu.VMEM((NSA_GROUP, VT_ROWS, TQ), F32)],
        compiler_params=_cparams(("arbitrary", "arbitrary", "arbitrary")),
        name="flash",
    )(qp, k_slc, vt_slc, k_win, vt_win)


def _diag_blocks(o_full, o_ref, s=0):
    rows = o_ref.shape[1] // NSA_KV_HEADS
    for g in range(NSA_KV_HEADS):
        o_ref[s, g * rows:(g + 1) * rows, :] = o_full[g * rows:(g + 1) * rows, g * HEAD_DIM:(g + 1) * HEAD_DIM]


def _cmp_sample_body(pt_ref, *refs, n_pages, past_len, dec, sb):
    fs_refs = refs[:sb * n_pages]
    qt_ref, ov_ref, msum_ref, oc_ref, at_ref = refs[sb * n_pages:]
    nq = qt_ref.shape[1]
    nblk = n_pages * fs_refs[0].shape[0]
    nj = ov_ref.shape[0]
    pos_l = past_len + (lax.broadcasted_iota(jnp.int32, (nblk, nq), 1) // NSA_GROUP) % dec
    n_c = lax.broadcasted_iota(jnp.int32, (nblk, nq), 0)
    vis_t = jnp.logical_and(n_c * CMP_STRIDE + 2 * CMP_STRIDE - 1 <= pos_l, n_c < nblk - 1)
    jr = lax.broadcasted_iota(jnp.int32, (nj, nq), 0)
    pos = past_len + (lax.broadcasted_iota(jnp.int32, (nj, nq), 1) // NSA_GROUP) % dec
    n_sel = -(-(past_len + dec) // SEL_BLOCK)
    for s in range(sb):
        fs = jnp.concatenate([r[...] for r in fs_refs[s * n_pages:(s + 1) * n_pages]], axis=0)
        kc, vc = _finish_compress(fs)
        qt = qt_ref[s]
        p_t = _softmax_masked(_nt(kc, qt), vis_t, 0)
        _diag_blocks(_tn(p_t.astype(BF16), vc), oc_ref, s)
        a1 = _split_dot(ov_ref[...], p_t)
        hi = a1.astype(BF16)
        lo = (a1 - hi.astype(F32)).astype(BF16)
        sc = _dot(hi, msum_ref[...]) + _dot(lo, msum_ref[...])
        at_ref[s] = _select_mask(sc, jr, pos, n_sel)


def _cmp_sample(page_table, fs_phys, qt, ov_t, msum, past_len, dec):
    nb, n_pages = page_table.shape
    sb = CMP_SEQS
    nj = ov_t.shape[0]
    fs_specs =[pl.BlockSpec((SUBLANES, 1024), functools.partial(lambda b, pt, s, k: (pt[b * sb + s, k], 0), s=s, k=k))
                for s in range(sb) for k in range(n_pages)]
    grid_spec = pltpu.PrefetchScalarGridSpec(
        num_scalar_prefetch=1,
        grid=(nb // sb,),
        in_specs=fs_specs + [pl.BlockSpec((sb, LANES, 256), lambda b, pt: (b, 0, 0)),
                             pl.BlockSpec(ov_t.shape, lambda b, pt: (0, 0)),
                             pl.BlockSpec(msum.shape, lambda b, pt: (0, 0))],
        out_specs=[pl.BlockSpec((sb, 64, HEAD_DIM), lambda b, pt: (b, 0, 0)),
                   pl.BlockSpec((sb, nj, LANES), lambda b, pt: (b, 0, 0))],
    )
    return pl.pallas_call(
        functools.partial(_cmp_sample_body, n_pages=n_pages, past_len=past_len, dec=dec, sb=sb),
        grid_spec=grid_spec,
        out_shape=[jax.ShapeDtypeStruct((nb, 64, HEAD_DIM), F32),
                   jax.ShapeDtypeStruct((nb, nj, LANES), F32)],
        compiler_params=_cparams(("arbitrary",)),
        name="cmp_sample",
    )(page_table, *([fs_phys] * (sb * n_pages)), qt, ov_t, msum)


def _attn_sample_body(*refs, n_blocks, sb, dec, has_pt, has_bm, has_pm, emit_state):
    refs = list(refs[1:] if has_pt else refs)
    n_kv = sb * n_blocks if has_pt else 1
    kv_refs = refs[:n_kv]
    rest = refs[n_kv:]
    new_ref, qt_ref = rest[0], rest[1]
    rest = rest[2:]
    bm_ref = rest.pop(0) if has_bm else None
    pm_ref = rest.pop(0) if has_pm else None
    nm_ref, o_ref = rest[0], rest[1]
    so_ref = rest[2] if emit_state else None
    st_ref = rest[-1]
    nq = qt_ref.shape[2]
    rows = kv_refs[0].shape[3]
    per = rows // SEL_BLOCK

    def kv(s, kb, which):
        return kv_refs[s * n_blocks + kb][0, which] if has_pt else kv_refs[0][s, which]

    for s in range(sb):
        qbd = qt_ref[s]
        m = jnp.full((1, nq), NEG, F32)
        for kb in range(n_blocks):
            sc = _tn(kv(s, kb, 0).astype(BF16), qbd)
            if has_pm:
                sc = sc + pm_ref[kb * rows:(kb + 1) * rows, :]
            if has_bm:
                sc = sc + jnp.concatenate(
                    [jnp.broadcast_to(bm_ref[s, kb * per + u:kb * per + u + 1, :], (SEL_BLOCK, nq)) for u in range(per)],
                    axis=0)
            st_ref[s, kb * rows:(kb + 1) * rows, :] = sc
            m = jnp.maximum(m, jnp.max(sc, axis=0, keepdims=True))
        s_new = _dot(new_ref[s, :, 0:256].astype(BF16), qbd) + nm_ref[...]
        if has_bm:
            s_new = s_new + bm_ref[s, n_blocks * per:n_blocks * per + 1, :]
        m = jnp.maximum(m, jnp.max(s_new, axis=0, keepdims=True))
        e_new = jnp.exp2(s_new - m)
        l = jnp.sum(e_new, axis=0, keepdims=True)
        acc = _tn(new_ref[s, :, 256:512].astype(BF16), e_new.astype(BF16))
        for kb in range(n_blocks):
            e = jnp.exp2(st_ref[s, kb * rows:(kb + 1) * rows, :] - m)
            l = l + jnp.sum(e, axis=0, keepdims=True)
            acc = acc + _dot(kv(s, kb, 1).astype(BF16), e.astype(BF16))
        o_ref[s] = acc / jnp.maximum(l, 1e-30)
        if emit_state:
            lane = lax.broadcasted_iota(jnp.int32, (256, LANES), 1)
            tail = pltpu.roll(new_ref[s], SUBLANES - dec, axis=0)
            for which in range(2):
                shifted = pltpu.roll(kv(s, 0, which), rows - dec, axis=1)
                fill = jnp.concatenate([jnp.zeros((LANES - SUBLANES, 256), F32),
                                        tail[:, which * 256:(which + 1) * 256]], axis=0).T
                last = jnp.where(lane < LANES - dec, shifted[:, rows - LANES:], fill)
                so_ref[s, which, :, 0:rows - LANES] = shifted[:, 0:rows - LANES]
                so_ref[s, which, :, rows - LANES:rows] = last


def _attn_sample(kv, slot_blk, n_blocks, new, qt, new_mask, sb, dec,
                 page_table=None, blk_mask=None, past_mask=None, emit_state=False):
    nb = qt.shape[0]
    kv_rows = kv.shape[3]
    has_pt = page_table is not None
    has_bm = blk_mask is not None
    has_pm = past_mask is not None
    if has_pt:
        kv_specs = [pl.BlockSpec((1, 2, 256, kv_rows),
                                 functools.partial(lambda b, pt, s, k: (pt[b * sb + s, k], slot_blk, 0, 0), s=s, k=k))
                    for s in range(sb) for k in range(n_blocks)]
        im = lambda b, pt: (b, 0, 0)
        im4 = lambda b, pt: (b, 0, 0, 0)
        cm = lambda b, pt: (0, 0)
    else:
        kv_specs = [pl.BlockSpec((sb, 2, 256, kv_rows), lambda b: (b, slot_blk, 0, 0))]
        im = lambda b: (b, 0, 0)
        im4 = lambda b: (b, 0, 0, 0)
        cm = lambda b: (0, 0)
    in_specs = kv_specs + [pl.BlockSpec((sb, SUBLANES, 512), im), pl.BlockSpec((sb, 256, LANES), im)]
    args = [kv] * len(kv_specs) + [new, qt]
    if has_bm:
        in_specs.append(pl.BlockSpec((sb,) + blk_mask.shape[1:], im))
        args.append(blk_mask)
    if has_pm:
        in_specs.append(pl.BlockSpec(past_mask.shape, cm))
        args.append(past_mask)
    in_specs.append(pl.BlockSpec(new_mask.shape, cm))
    args.append(new_mask)
    out_specs = [pl.BlockSpec((sb, 256, LANES), im)]
    out_shape = [jax.ShapeDtypeStruct((nb, 256, LANES), F32)]
    if emit_state:
        out_specs.append(pl.BlockSpec((sb, 2, 256, kv_rows), im4))
        out_shape.append(jax.ShapeDtypeStruct((nb, 2, 256, kv_rows), F32))
    scratch = [pltpu.VMEM((sb, n_blocks * kv_rows, LANES), F32)]
    body = functools.partial(_attn_sample_body, n_blocks=n_blocks, sb=sb, dec=dec, has_pt=has_pt, has_bm=has_bm,
                             has_pm=has_pm, emit_state=emit_state)
    if has_pt:
        gs = pltpu.PrefetchScalarGridSpec(num_scalar_prefetch=1, grid=(nb // sb,), in_specs=in_specs,
                                          out_specs=out_specs, scratch_shapes=scratch)
        return pl.pallas_call(body, grid_spec=gs, out_shape=out_shape, compiler_params=_cparams(("arbitrary",)),
                              name="slc_sample")(page_table, *args)
    return pl.pallas_call(body, grid=(nb // sb,), in_specs=in_specs, out_specs=out_specs, scratch_shapes=scratch,
                          out_shape=out_shape, compiler_params=_cparams(("arbitrary",)), name="win_sample")(*args)


def _mix_project(o_refs, gate_ref, w_ref, tm, feature_major):
    n_in = len(o_refs)
    gated = gate_ref is not None
    if feature_major:
        gates_t = gate_ref[...].T
        heads = []
        for hd in range(NSA_HEADS):
            acc = jnp.zeros((HEAD_DIM, tm), F32)
            for kbr in range(n_in):
                row = gates_t[hd * 3 + kbr:hd * 3 + kbr + 1, :]
                acc = acc + row * o_refs[kbr][0, hd * HEAD_DIM:(hd + 1) * HEAD_DIM, :].astype(F32)
            heads.append(acc.astype(BF16))
        return _tn(jnp.concatenate(heads, axis=0), w_ref[...])
    if gated:
        lane = lax.broadcasted_iota(jnp.int32, (tm, LANES), 1)
        gates = gate_ref[...]
        chunks = []
        for c in range(1024 // LANES):
            acc = jnp.zeros((tm, LANES), F32)
            for kbr in range(n_in):
                c0 = (2 * c) * 3 + kbr
                c1 = (2 * c + 1) * 3 + kbr
                gexp = jnp.where(lane < HEAD_DIM, gates[:, c0:c0 + 1], gates[:, c1:c1 + 1])
                acc = acc + gexp * o_refs[kbr][:, c * LANES:(c + 1) * LANES].astype(F32)
            chunks.append(acc.astype(BF16))
        o = jnp.concatenate(chunks, axis=1)
    else:
        o = o_refs[0][...].astype(BF16)
    return _dot(o, w_ref[...])


def _ffn_body(*refs, sample, tiles_per_seq, nb, n_in, gated, feature_major):
    o_refs = refs[:n_in]
    refs = refs[n_in:]
    gate_ref = None
    if gated:
        gate_ref, refs = refs[0], refs[1:]
    wo_ref, g1_ref, refs = refs[0], refs[1], refs[2:]
    if sample:
        x_ref, g2_ref, g3_ref, wu_ref, cw_ref, cb_ref, wd_ref, buf_ref, y_ref, tail_ref = refs
    else:
        x_ref, g2_ref, g3_ref, wu_ref, cw_ref, cb_ref, wd_ref, y_ref, tail_ref, carry_ref = refs
    tm = x_ref.shape[0]
    x = x_ref[...] + _rms(_mix_project(o_refs, gate_ref, wo_ref, tm, feature_major), g1_ref[...])
    f = wd_ref.shape[0]
    h = _rms(x, g2_ref[...]).astype(BF16)
    acc = jnp.zeros(x.shape, F32)
    if not sample:
        first = pl.program_id(0) % tiles_per_seq == 0
        row = lax.broadcasted_iota(jnp.int32, (tm, FFN_FCHUNK), 0)
    for fc in range(f // FFN_FCHUNK):
        sl = slice(fc * FFN_FCHUNK, (fc + 1) * FFN_FCHUNK)
        gate = _dot(h, wu_ref[:, sl])
        val = _dot(h, wu_ref[:, f + fc * FFN_FCHUNK:f + (fc + 1) * FFN_FCHUNK])
        if sample:
            prev1 = jnp.concatenate([buf_ref[nb:2 * nb, sl], gate[0:tm - nb]], axis=0)
            prev2 = jnp.concatenate([buf_ref[0:2 * nb, sl], gate[0:tm - 2 * nb]], axis=0)
            tail_ref[:, sl] = gate[tm - 2 * nb:tm]
        else:
            carry = carry_ref[:, sl]
            c6 = jnp.where(first, 0.0, carry[6:7, :])
            c7 = jnp.where(first, 0.0, carry[7:8, :])
            prev1 = jnp.where(row == 0, c7, pltpu.roll(gate, 1, axis=0))
            prev2 = jnp.where(row == 0, c6, jnp.where(row == 1, c7, pltpu.roll(gate, 2, axis=0)))
            carry_ref[:, sl] = gate[tm - SUBLANES:tm]
            tail_ref[:, sl] = gate[tm - SUBLANES:tm]
        conv = cb_ref[0:1, sl] + cw_ref[0:1, sl] * prev2 + cw_ref[1:2, sl] * prev1 + cw_ref[2:3, sl] * gate
        act = (jax.nn.gelu(conv) * val).astype(BF16)
        acc = acc + _dot(act, wd_ref[sl, :])
    y_ref[...] = x + _rms(acc, g3_ref[...])


def _ffn(x, mix, g, wu, cw, cb, wd, seq_len=None, buf=None):
    m, d = x.shape
    f = wd.shape[0]
    os_, gates, w_o, feature_major = mix
    sample = buf is not None
    row = lambda i: (i, 0)
    if sample:
        tm = m
        nb = buf.shape[0] // 2
        tail_shape, tail_block = (2 * nb, f), (2 * nb, f)
        scratch = []
    else:
        tm = FFN_TM
        nb = 0
        tail_shape, tail_block = (m // tm * SUBLANES, f), (SUBLANES, f)
        scratch = [pltpu.VMEM((SUBLANES, f), F32)]
    if feature_major:
        nt = seq_len // tm
        in_specs = [pl.BlockSpec((1, 1024, tm), lambda i: (i // nt, 0, i % nt)) for _ in os_]
    else:
        in_specs = [pl.BlockSpec((tm, 1024), row) for _ in os_]
    args = list(os_)
    if gates is not None:
        in_specs.append(pl.BlockSpec((tm, LANES), row))
        args.append(gates)
    g1, g2, g3 = g[1:2], g[2:3], g[3:4]
    in_specs += [_resident(w_o.shape), _resident(g1.shape),
                 pl.BlockSpec((tm, d), row), _resident(g2.shape), _resident(g3.shape), _resident(wu.shape),
                 _resident(cw.shape), _resident(cb.shape), _resident(wd.shape)]
    args += [w_o, g1, x, g2, g3, wu, cw, cb, wd]
    if sample:
        in_specs.append(_resident(buf.shape))
        args.append(buf)
    return pl.pallas_call(
        functools.partial(_ffn_body, sample=sample, tiles_per_seq=(seq_len // tm if not sample else 1), nb=nb,
                          n_in=len(os_), gated=gates is not None, feature_major=feature_major),
        grid=(m // tm,), in_specs=in_specs,
        out_specs=[pl.BlockSpec((tm, d), row), pl.BlockSpec(tail_block, row)],
        out_shape=[jax.ShapeDtypeStruct((m, d), F32), jax.ShapeDtypeStruct(tail_shape, F32)],
        scratch_shapes=scratch,
        compiler_params=_cparams(("arbitrary",)), name="ffn_sample" if sample else "ffn_prompt",
    )(*args)


def _gla_proj_body(x_ref, g_ref, w_ref, wg_ref, bg_ref, q_ref, k_ref, v_ref, sr_ref, la_ref):
    h = _rms(x_ref[...], g_ref[...]).astype(BF16)
    nk = GLA_HEADS * GLA_DK
    nv = GLA_HEADS * GLA_DV
    for c in range(nk // 256):
        q_ref[:, c * 256:(c + 1) * 256] = _dot(h, w_ref[:, c * 256:(c + 1) * 256]) * (GLA_DK ** -0.5)
        k_ref[:, c * 256:(c + 1) * 256] = _dot(h, w_ref[:, nk + c * 256:nk + (c + 1) * 256])
    for c in range(nv // 256):
        v_ref[:, c * 256:(c + 1) * 256] = _dot(h, w_ref[:, 2 * nk + c * 256:2 * nk + (c + 1) * 256]).astype(BF16)
        r = _dot(h, w_ref[:, 2 * nk + nv + c * 256:2 * nk + nv + (c + 1) * 256])
        sr_ref[:, c * 256:(c + 1) * 256] = jax.nn.silu(r).astype(BF16)
    low = _dot(h, w_ref[:, 2 * nk + 2 * nv:2 * nk + 2 * nv + LANES]).astype(BF16)
    gz = _dot(low, wg_ref[...]) + bg_ref[...]
    log_sig = jnp.minimum(gz, 0.0) - jnp.log1p(jnp.exp(-jnp.abs(gz)))
    la_ref[...] = log_sig / GLA_GATE_TEMP


def _gla_proj(x, g, w, wg, bg):
    m, d = x.shape
    nk = GLA_HEADS * GLA_DK
    nv = GLA_HEADS * GLA_DV
    row = lambda i: (i, 0)
    return pl.pallas_call(
        _gla_proj_body, grid=(m // TM,),
        in_specs=[pl.BlockSpec((TM, d), row), _resident(g.shape), _resident(w.shape),
                  _resident(wg.shape), _resident(bg.shape)],
        out_specs=[pl.BlockSpec((TM, nk), row), pl.BlockSpec((TM, nk), row), pl.BlockSpec((TM, nv), row),
                   pl.BlockSpec((TM, nv), row), pl.BlockSpec((TM, nk), row)],
        out_shape=[jax.ShapeDtypeStruct((m, nk), F32), jax.ShapeDtypeStruct((m, nk), F32),
                   jax.ShapeDtypeStruct((m, nv), BF16), jax.ShapeDtypeStruct((m, nv), BF16),
                   jax.ShapeDtypeStruct((m, nk), F32)],
        compiler_params=_cparams(("arbitrary",)), name="gla_proj",
    )(x, g, w, wg, bg)


def _cumsum_rows(x):
    n = x.shape[0]
    row = lax.broadcasted_iota(jnp.int32, x.shape, 0)
    sh = 1
    while sh < n:
        x = x + jnp.where(row >= sh, pltpu.roll(x, sh, axis=0), 0.0)
        sh *= 2
    return x


def _gla_chunk(q, k, la, v, s_old):
    cs = q.shape[0]
    row = lax.broadcasted_iota(jnp.int32, (cs, 1), 0)
    trow = lax.broadcasted_iota(jnp.int32, (cs, cs), 0)
    scol = lax.broadcasted_iota(jnp.int32, (cs, cs), 1)
    r8 = lax.broadcasted_iota(jnp.int32, (SUBLANES, 1), 0)
    lane8 = lax.broadcasted_iota(jnp.int32, (SUBLANES, cs), 1)
    cum = _cumsum_rows(la * math.log2(math.e))
    last = cum[cs - 1:cs, :]
    out = _dot((q * jnp.exp2(cum)).astype(BF16), s_old.astype(BF16))
    att = None
    hh = cs // 2
    while hh >= SUBLANES:
        nblk = cs // (2 * hh)
        ref = jnp.concatenate(
            [jnp.broadcast_to(cum[u * 2 * hh + hh - 1:u * 2 * hh + hh, :], (2 * hh, GLA_DK)) for u in range(nblk)],
            axis=0) if nblk > 1 else jnp.broadcast_to(cum[hh - 1:hh, :], (cs, GLA_DK))
        second = (row % (2 * hh)) >= hh
        qh = jnp.where(second, q * jnp.exp2(cum - ref), 0.0).astype(BF16)
        kh = jnp.where(second, 0.0, k * jnp.exp2(ref - cum)).astype(BF16)
        a = _nt(qh, kh)
        if nblk > 1:
            a = jnp.where(trow // (2 * hh) == scol // (2 * hh), a, 0.0)
        att = a if att is None else att + a
        hh //= 2
    slabs = []
    for g8 in range(cs // SUBLANES):
        r0 = g8 * SUBLANES
        cg, qg, kg = cum[r0:r0 + SUBLANES], q[r0:r0 + SUBLANES], k[r0:r0 + SUBLANES]
        slab = jnp.zeros((SUBLANES, GLA_DV if att is None else cs), F32)
        for s in range(SUBLANES):
            e = jnp.exp2(cg - cg[s:s + 1, :])
            col = jnp.sum(e * qg * kg[s:s + 1, :], axis=1, keepdims=True)
            col = jnp.where(r8 >= s, col, 0.0)
            if att is None:
                slab = slab + col * v[r0 + s:r0 + s + 1, :].astype(F32)
            else:
                slab = jnp.where(lane8 == r0 + s, col, slab)
        slabs.append(slab)
    if att is None:
        out = out + slabs[0]
    else:
        att = att + jnp.concatenate(slabs, axis=0)
        out = out + _dot(att.astype(BF16), v)
    kt = (k * jnp.exp2(last - cum)).astype(BF16)
    dcol = jnp.broadcast_to(jnp.exp2(last), (SUBLANES, GLA_DK)).T[:, 0:1]
    return out, dcol * s_old + _tn(kt, v)


def _gla_rec_prompt_body(q_ref, k_ref, la_ref, v_ref, sr_ref, gn_ref, o_ref, so_ref, st_ref):
    c = pl.program_id(1)

    @pl.when(c == 0)
    def _():
        st_ref[...] = jnp.zeros(st_ref.shape, F32)

    for hd in range(GLA_HEADS):
        ksl = slice(hd * GLA_DK, (hd + 1) * GLA_DK)
        vsl = slice(hd * GLA_DV, (hd + 1) * GLA_DV)
        out, s_new = _gla_chunk(q_ref[:, ksl], k_ref[:, ksl], la_ref[:, ksl], v_ref[:, vsl], st_ref[hd])
        st_ref[hd] = s_new
        o_ref[:, vsl] = (_rms(out, gn_ref[...]) * sr_ref[:, vsl].astype(F32)).astype(o_ref.dtype)

    @pl.when(c == pl.num_programs(1) - 1)
    def _():
        so_ref[0] = st_ref[...]


def _gla_rec_sample_body(q_ref, k_ref, la_ref, v_ref, sr_ref, gn_ref, s0_ref, o_ref, so_ref, *, sb):
    for s in range(sb):
        rsl = slice(s * SUBLANES, (s + 1) * SUBLANES)
        for hd in range(GLA_HEADS):
            ksl = slice(hd * GLA_DK, (hd + 1) * GLA_DK)
            vsl = slice(hd * GLA_DV, (hd + 1) * GLA_DV)
            out, s_new = _gla_chunk(q_ref[rsl, ksl], k_ref[rsl, ksl], la_ref[rsl, ksl], v_ref[rsl, vsl], s0_ref[s, hd])
            so_ref[s, hd] = s_new
            o_ref[rsl, vsl] = (_rms(out, gn_ref[...]) * sr_ref[rsl, vsl].astype(F32)).astype(o_ref.dtype)


def _gla_rec(q, k, la, v, sr, gn, nb, chunk, s0=None):
    m = q.shape[0]
    nk = GLA_HEADS * GLA_DK
    nv = GLA_HEADS * GLA_DV
    out_shape = [jax.ShapeDtypeStruct((m, nv), BF16), jax.ShapeDtypeStruct((nb, GLA_HEADS, GLA_DK, GLA_DV), F32)]
    if s0 is None:
        nc = m // nb // chunk
        row = lambda b, c: (b * nc + c, 0)
        st_spec = pl.BlockSpec((1, GLA_HEADS, GLA_DK, GLA_DV), lambda b, c: (b, 0, 0, 0))
        return pl.pallas_call(
            _gla_rec_prompt_body, grid=(nb, nc),
            in_specs=[pl.BlockSpec((chunk, nk), row), pl.BlockSpec((chunk, nk), row), pl.BlockSpec((chunk, nk), row),
                      pl.BlockSpec((chunk, nv), row), pl.BlockSpec((chunk, nv), row),
                      pl.BlockSpec(gn.shape, lambda b, c: (0, 0))],
            out_specs=[pl.BlockSpec((chunk, nv), row), st_spec], out_shape=out_shape,
            scratch_shapes=[pltpu.VMEM((GLA_HEADS, GLA_DK, GLA_DV), F32)],
            compiler_params=_cparams(("arbitrary", "arbitrary")), name="gla_rec_prompt",
        )(q, k, la, v, sr, gn)
    sb = GLA_SEQS
    rows = sb * chunk
    row = lambda b: (b, 0)
    st_spec = pl.BlockSpec((sb, GLA_HEADS, GLA_DK, GLA_DV), lambda b: (b, 0, 0, 0))
    return pl.pallas_call(
        functools.partial(_gla_rec_sample_body, sb=sb), grid=(nb // sb,),
        in_specs=[pl.BlockSpec((rows, nk), row), pl.BlockSpec((rows, nk), row), pl.BlockSpec((rows, nk), row),
                  pl.BlockSpec((rows, nv), row), pl.BlockSpec((rows, nv), row),
                  pl.BlockSpec(gn.shape, lambda b: (0, 0)), st_spec],
        out_specs=[pl.BlockSpec((rows, nv), row), st_spec], out_shape=out_shape,
        compiler_params=_cparams(("arbitrary",)), name="gla_rec_sample",
    )(q, k, la, v, sr, gn, s0)


def _rope_tables(pos):
    half = HEAD_DIM // 2
    inv = ROPE_THETA ** (-jnp.arange(half, dtype=F32) / half)
    ang = pos.astype(F32)[:, None] * inv[None, :]
    cos, sin = jnp.cos(ang), jnp.sin(ang)
    z = jnp.zeros_like(sin)
    cos_t = jnp.tile(cos, (1, 4))
    sa = jnp.tile(jnp.concatenate([-sin, z], axis=1), (1, 2))
    sb = jnp.tile(jnp.concatenate([z, sin], axis=1), (1, 2))
    return cos_t, sa, sb


def _pad_cols(w, n):
    return jnp.pad(w, ((0, 0), (0, n - w.shape[1])))


def _cmp_weights(cmp_pe, cmp_w):
    g = NSA_KV_HEADS
    w = cmp_w.reshape(2, 2, CMP_STRIDE, HEAD_DIM, HEAD_DIM)
    eye = jnp.eye(g, dtype=F32)
    wbd = jnp.einsum('shjde,gk->sjgdhke', w, eye)
    wbd = wbd.reshape(2, CMP_STRIDE, g * HEAD_DIM, 2 * g * HEAD_DIM).astype(BF16)
    pe = cmp_pe.reshape(2, 2, CMP_STRIDE, 1, 1, HEAD_DIM)
    pe_t = jnp.broadcast_to(pe, (2, 2, CMP_STRIDE, SUBLANES, g, HEAD_DIM)).reshape(2, 2, CMP_STRIDE, SUBLANES, g * HEAD_DIM)
    return wbd, pe_t


def _overlap_t(n_blk, n_sel, rows, cols):
    i = jnp.arange(cols)[None, :]
    j = jnp.arange(rows)[:, None]
    ov = (i * CMP_STRIDE + 2 * CMP_STRIDE > j * SEL_BLOCK) & (i * CMP_STRIDE < (j + 1) * SEL_BLOCK)
    ov = ov & (i < n_blk) & (j < n_sel)
    return ov.astype(BF16)


def _nsa_layer_prompt(x, g, w_in, wbd, cbias, w_o, tabs, b, t):
    q, gates, rows_t, win_t, k_slc, vt_slc, k_win, vt_win = _nsa_proj_t(x, g[0:1], w_in, *tabs, b, t)
    fs = _compress_pages(rows_t.reshape(b, 4, NSA_KV_HEADS * HEAD_DIM, t), wbd, cbias)
    n_blk = t // CMP_STRIDE - 1
    n_sel = -(-t // SEL_BLOCK)
    oc, qp = _cmp_prompt(q, fs, _overlap_t(n_blk, n_sel, n_sel, t // CMP_STRIDE), b, t)
    o_s, o_w = _flash(qp, k_slc, vt_slc, k_win, vt_win, WINDOW)
    return ([oc, o_s, o_w], gates, w_o, True), rows_t, win_t


def _to_rows(o, nb, dec):
    o = o.reshape(nb, NSA_KV_HEADS, dec, NSA_GROUP, HEAD_DIM).transpose(2, 0, 1, 3, 4)
    return o.reshape(dec * nb, NSA_HEADS * HEAD_DIM).astype(BF16)


def _nsa_layer_sample(x, g, w_in, wbd, cbias, w_o, tabs, cache, page_table, win_state, nb, dec):
    n_phys, page = cache.shape[0], cache.shape[1]
    n_pages = page_table.shape[1]
    past_len = n_pages * page
    q, rows, win, gates = _nsa_proj(x, g[0:1], w_in, *tabs)
    pages_t = cache.transpose(0, 2, 3, 4, 1).reshape(n_phys, 4, NSA_KV_HEADS * HEAD_DIM, page)
    wb = win_state.shape[1]
    win_t = win_state.transpose(0, 2, 3, 4, 1).reshape(nb, 2, NSA_KV_HEADS * HEAD_DIM, wb)
    fs_phys = _compress_pages(pages_t, wbd, cbias)
    q5 = q.reshape(dec, nb, NSA_KV_HEADS, NSA_GROUP, HEAD_DIM).transpose(1, 2, 0, 3, 4)
    eye = jnp.eye(NSA_KV_HEADS, dtype=BF16)
    qt = q5.reshape(nb, NSA_KV_HEADS, dec * NSA_GROUP, 1, HEAD_DIM) * eye[None, :, None, :, None]
    qt = qt.reshape(nb, NSA_KV_HEADS * dec * NSA_GROUP, NSA_KV_HEADS * HEAD_DIM)
    qt = jnp.pad(qt, ((0, 0), (0, LANES - qt.shape[1]), (0, 0)))
    n_blk = (past_len + dec) // CMP_STRIDE - 1
    n_sel = -(-(past_len + dec) // SEL_BLOCK)
    nj = -(-n_sel // SUBLANES) * SUBLANES
    lane = jnp.arange(LANES)
    msum = (lane[:, None] // NSA_GROUP == lane[None, :] // NSA_GROUP).astype(BF16)
    oc, a_t = _cmp_sample(page_table, fs_phys, qt, _overlap_t(n_blk, n_sel, nj, past_len // CMP_STRIDE), msum, past_len, dec)
    t_q = ((lane // NSA_GROUP) % dec)[None, :]
    j8 = jnp.arange(SUBLANES)[:, None]
    new_mask = jnp.where((j8 < dec) & (j8 <= t_q), 0.0, NEG).astype(F32)
    win_mask = jnp.where(jnp.arange(wb)[:, None] >= t_q + wb - WINDOW, 0.0, NEG).astype(F32)
    qbd = qt.transpose(0, 2, 1)

    def new_rows(a):
        a = a.reshape(dec, nb, 512).transpose(1, 0, 2)
        return jnp.pad(a, ((0, 0), (0, SUBLANES - dec), (0, 0)))

    def from_t(o):
        o = o[:, :, :NSA_KV_HEADS * dec * NSA_GROUP].reshape(nb, NSA_KV_HEADS, HEAD_DIM, NSA_KV_HEADS, dec, NSA_GROUP)
        o = jnp.diagonal(o, axis1=1, axis2=3)
        return o.transpose(2, 0, 4, 3, 1).reshape(dec * nb, NSA_HEADS * HEAD_DIM).astype(BF16)

    (o_s,) = _attn_sample(pages_t, 1, n_pages, new_rows(rows[:, 512:1024]), qbd, new_mask, SLC_SEQS, dec,
                          page_table=page_table, blk_mask=a_t)
    o_w, win_next = _attn_sample(win_t, 0, 1, new_rows(win), qbd, new_mask, WIN_SEQS, dec,
                                 past_mask=win_mask, emit_state=True)
    mix = ([_to_rows(oc, nb, dec), from_t(o_s), from_t(o_w)], gates, w_o, False)
    win_next = win_next.reshape(nb, 2, NSA_KV_HEADS, HEAD_DIM, wb).transpose(0, 4, 1, 2, 3)
    return mix, rows, win_next


def _gla_layer(x, g, w_in, wg, bg, gn, w_o, nb, seq, s0=None):
    q, k, v, sr, la = _gla_proj(x, g[0:1], w_in, wg, bg)
    if s0 is None:
        o, s = _gla_rec(q, k, la, v, sr, gn, nb, GLA_CHUNK)
    else:
        def seqs(a):
            a = a.reshape(seq, nb, a.shape[1]).transpose(1, 0, 2)
            return jnp.pad(a, ((0, 0), (0, SUBLANES - seq), (0, 0))).reshape(nb * SUBLANES, a.shape[2])
        o, s = _gla_rec(seqs(q), seqs(k), seqs(la), seqs(v), seqs(sr), gn, nb, SUBLANES, s0=s0)
        o = o.reshape(nb, SUBLANES, o.shape[1])[:, :seq].transpose(1, 0, 2).reshape(seq * nb, o.shape[1])
    return ([o], None, w_o, False), s


def kernel(x_prompt, x_sample, cache_nsa_kv, state_win_kv, state_gla, state_ffn_conv, page_table, norm_gain, nsa_w_in, nsa_cmp_pe, nsa_cmp_w, nsa_w_o, gla_w_in, gla_w_gate_up, gla_b_gate, gla_norm_gain, gla_w_o, ffn_w_up, ffn_conv_w, ffn_conv_b, ffn_w_down):
    b, t, d = x_prompt.shape
    nb, dec, _ = x_sample.shape
    depth = norm_gain.shape[0]
    f = ffn_w_down.shape[1]
    past_len = page_table.shape[1] * cache_nsa_kv.shape[2]

    xp = x_prompt.reshape(b * t, d)
    xs = x_sample.transpose(1, 0, 2).reshape(dec * nb, d)
    tabs_p = _rope_tables(jnp.arange(t, dtype=jnp.int32))
    tabs_s = _rope_tables(past_len + jnp.repeat(jnp.arange(dec, dtype=jnp.int32), nb))

    nsa_p, nsa_s, win_p, win_s, gla_p, gla_s, ffn_p, ffn_s = [], [], [], [], [], [], [], []
    for i in range(depth):
        g = norm_gain[i]
        a = i // 2
        if i % 2 == 0:
            w_in = _pad_cols(nsa_w_in[a], 2688).astype(BF16)
            w_o = nsa_w_o[a].astype(BF16)
            wbd, pe_t = _cmp_weights(nsa_cmp_pe[a], nsa_cmp_w[a])
            cbias = _cmp_bias(pe_t, wbd)
            mix_p, rows_t, win_t = _nsa_layer_prompt(xp, g, w_in, wbd, cbias, w_o, tabs_p, b, t)
            nsa_p.append(rows_t.reshape(b, 4, NSA_KV_HEADS, HEAD_DIM, t).transpose(0, 4, 1, 2, 3))
            nw = min(WINDOW, t)
            win_p.append(win_t.reshape(b, 2, NSA_KV_HEADS, HEAD_DIM, t)[..., t - nw:].transpose(0, 4, 1, 2, 3))
            mix_s, rows, win_next = _nsa_layer_sample(xs, g, w_in, wbd, cbias, w_o, tabs_s, cache_nsa_kv[a], page_table,
                                                      state_win_kv[a], nb, dec)
            nsa_s.append(rows.reshape(dec, nb, 4, NSA_KV_HEADS, HEAD_DIM).transpose(1, 0, 2, 3, 4))
            win_s.append(win_next)
        else:
            w_in = _pad_cols(gla_w_in[a], 3200).astype(BF16)
            wg = jnp.pad(gla_w_gate_up[a], ((0, LANES - GLA_GATE_RANK), (0, 0))).astype(BF16)
            bg = gla_b_gate[a][None, :]
            gn = gla_norm_gain[a][None, :]
            w_o = gla_w_o[a].astype(BF16)
            mix_p, s = _gla_layer(xp, g, w_in, wg, bg, gn, w_o, b, t)
            gla_p.append(s)
            mix_s, s = _gla_layer(xs, g, w_in, wg, bg, gn, w_o, nb, dec, s0=state_gla[a])
            gla_s.append(s)
        wu = ffn_w_up[i].astype(BF16)
        wd = ffn_w_down[i].astype(BF16)
        cw = jnp.pad(ffn_conv_w[i], ((0, SUBLANES - ffn_conv_w.shape[1]), (0, 0)))
        cb = ffn_conv_b[i][None, :]
        xp, tail = _ffn(xp, mix_p, g, wu, cw, cb, wd, seq_len=t)
        tail = tail.reshape(b, t // FFN_TM, SUBLANES, f)[:, -1, SUBLANES - 2:]
        ffn_p.append(tail)
        buf = state_ffn_conv[i].transpose(1, 0, 2).reshape(2 * nb, f)
        xs, tail = _ffn(xs, mix_s, g, wu, cw, cb, wd, buf=buf)
        ffn_s.append(tail.reshape(2, nb, f).transpose(1, 0, 2))

    y_prompt = xp.reshape(b, t, d)
    y_sample = xs.reshape(dec, nb, d).transpose(1, 0, 2)
    return (y_prompt, y_sample, jnp.stack(nsa_p), jnp.stack(nsa_s), jnp.stack(win_p), jnp.stack(win_s),
            jnp.stack(gla_p), jnp.stack(gla_s), jnp.stack(ffn_p), jnp.stack(ffn_s))
```
